```python
import jax, jax.numpy as jnp
from jax import lax
import numpy as np

D_MODEL = 2048
BATCH = 4
SEQ = 2048
DEPTH = 1
DEC_BATCH = 128
DEC_SEQ = 8
PAST_LEN = 16384
PAGE_SIZE = 128

E_A = D_MODEL // 2
HA = 4
DA = E_A // HA
CHUNK = 128
HB = 4
E_B = D_MODEL // 2
DV = E_B // HB
DK = DV // 2
K_B = HB * DK
GATE_RANK = 16
GATE_TAU = 16.0
GLA_CHUNK = 16
PLE_DIM = 256
EPS = 1e-6
SPLIT_SIZES = (E_A, E_A, E_A, K_B, K_B, E_B, E_B, GATE_RANK, D_MODEL, D_MODEL)
N_IN = sum(SPLIT_SIZES)

kernel_name = "hybrid_gmlp_gla_gated_merge_step"


def rms_norm(x, g):
    xf = x.astype(jnp.float32)
    y = xf * lax.rsqrt(jnp.mean(xf * xf, axis=-1, keepdims=True) + EPS)
    return (y * g.astype(jnp.float32)).astype(x.dtype)


def split_cols(t):
    outs, off = [], 0
    for s in SPLIT_SIZES:
        outs.append(t[..., off:off + s])
        off += s
    return outs


def chunk_spatial_gate(u, v, w_s, b_s):
    B, L, _ = v.shape
    n_c = -(-L // CHUNK)
    pad = n_c * CHUNK - L
    vp = jnp.pad(v, ((0, 0), (0, pad), (0, 0))).reshape(B, n_c, CHUNK, HA, DA)
    mask = jnp.tril(jnp.ones((CHUNK, CHUNK), dtype=bool))
    w = jnp.where(mask[None], w_s, jnp.zeros((), w_s.dtype))
    s = jnp.einsum('gts,bcsgd->bctgd', w, vp) + b_s.T[None, None, :, :, None]
    s = s.reshape(B, n_c * CHUNK, E_A)[:, :L]
    return u * s


def gla(q, k, v, log_a, s0):
    B, L = q.shape[:2]
    n_c = -(-L // GLA_CHUNK)
    pad = n_c * GLA_CHUNK - L

    def blocks(t):
        t = jnp.pad(t, ((0, 0), (0, pad), (0, 0), (0, 0)))
        return t.reshape(B, n_c, GLA_CHUNK, HB, t.shape[-1]).transpose(1, 0, 3, 2, 4)

    qc, kc, vc, ac = blocks(q), blocks(k), blocks(v), blocks(log_a)
    mask = jnp.tril(jnp.ones((GLA_CHUNK, GLA_CHUNK), dtype=bool))

    def step(S, inp):
        qb, kb, vb, ab = inp
        qb = qb.astype(jnp.float32)
        kb = kb.astype(jnp.float32)
        vb = vb.astype(jnp.float32)
        b = jnp.cumsum(ab.astype(jnp.float32), axis=-2)
        q_t = qb * jnp.exp(b)
        k_t = kb * jnp.exp(-b)
        att = jnp.where(mask, jnp.einsum('bhtd,bhsd->bhts', q_t, k_t), 0.0)
        o = jnp.einsum('bhts,bhsv->bhtv', att, vb) + jnp.einsum('bhtd,bhdv->bhtv', q_t, S)
        b_last = b[..., -1:, :]
        k_d = kb * jnp.exp(b_last - b)
        S = jnp.exp(b_last[..., 0, :])[..., None] * S + jnp.einsum('bhsd,bhsv->bhdv', k_d, vb)
        return S, o

    S, o = lax.scan(step, s0.astype(jnp.float32), (qc, kc, vc, ac))
    o = o.transpose(1, 0, 3, 2, 4).reshape(B, n_c * GLA_CHUNK, HB, DV)[:, :L]
    return o.astype(v.dtype), S.astype(s0.dtype)


def layer(x, p, s0, g_pre, w_in, g_v, w_s, b_s, w_a2, b_a, g_o, w_pa, w_pb, w_o, g_post, w_pg, w_pe):
    B, L, _ = x.shape
    xn = rms_norm(x, g_pre)
    u_a, v_a, z_a, q_b, k_b, v_b, z_b, a_lr, m_a, m_b = split_cols(xn @ w_in)
    v_a = rms_norm(v_a, g_v)
    y_a = chunk_spatial_gate(u_a, v_a, w_s, b_s) * jax.nn.silu(z_a)
    log_a = jax.nn.log_sigmoid((a_lr @ w_a2 + b_a).astype(jnp.float32)) / GATE_TAU
    q = q_b.reshape(B, L, HB, DK) * (DK ** -0.5)
    k = k_b.reshape(B, L, HB, DK)
    v = v_b.reshape(B, L, HB, DV)
    o, s_new = gla(q, k, v, log_a.reshape(B, L, HB, DK), s0)
    y_b = rms_norm(o, g_o).reshape(B, L, E_B) * jax.nn.silu(z_b)
    m = jax.nn.sigmoid(m_a) * (y_a @ w_pa) + jax.nn.sigmoid(m_b) * (y_b @ w_pb)
    h = x + rms_norm(m @ w_o, g_post)
    h = h + jax.nn.sigmoid(h @ w_pg) * (p @ w_pe)
    return h, s_new, v_a


def setup_inputs(seed: int = 0) -> dict:
    key = jax.random.key(seed)
    ks = jax.random.split(key, 20)
    f32 = jnp.float32
    nrm = lambda k, shape, scale: jax.random.normal(k, shape, f32) * scale
    return {
        "x_prompt": nrm(ks[0], (BATCH, SEQ, D_MODEL), 1.0),
        "x_sample": nrm(ks[1], (DEC_BATCH, DEC_SEQ, D_MODEL), 1.0),
        "p_prompt": nrm(ks[2], (DEPTH, BATCH, SEQ, PLE_DIM), 1.0),
        "p_sample": nrm(ks[3], (DEPTH, DEC_BATCH, DEC_SEQ, PLE_DIM), 1.0),
        "state_gla": nrm(ks[4], (DEPTH, DEC_BATCH, HB, DK, DV), 0.5),
        "g_pre": 1.0 + nrm(ks[5], (DEPTH, D_MODEL), 0.05),
        "w_in": nrm(ks[6], (DEPTH, D_MODEL, N_IN), D_MODEL ** -0.5),
        "g_v": 1.0 + nrm(ks[7], (DEPTH, E_A), 0.05),
        "w_s": nrm(ks[8], (DEPTH, HA, CHUNK, CHUNK), CHUNK ** -0.5),
        "b_s": 1.0 + nrm(ks[9], (DEPTH, HA, CHUNK), 0.1),
        "w_a2": nrm(ks[10], (DEPTH, GATE_RANK, K_B), GATE_RANK ** -0.5),
        "b_a": nrm(ks[11], (DEPTH, K_B), 0.1),
        "g_o": 1.0 + nrm(ks[12], (DEPTH, HB, DV), 0.05),
        "w_pa": nrm(ks[13], (DEPTH, E_A, D_MODEL), E_A ** -0.5),
        "w_pb": nrm(ks[14], (DEPTH, E_B, D_MODEL), E_B ** -0.5),
        "w_o": nrm(ks[15], (DEPTH, D_MODEL, D_MODEL), D_MODEL ** -0.5),
        "g_post": 1.0 + nrm(ks[16], (DEPTH, D_MODEL), 0.05),
        "w_pg": nrm(ks[17], (DEPTH, D_MODEL, D_MODEL), D_MODEL ** -0.5),
        "w_pe": nrm(ks[18], (DEPTH, PLE_DIM, D_MODEL), PLE_DIM ** -0.5),
    }


def reference(x_prompt, x_sample, p_prompt, p_sample, state_gla, g_pre, w_in, g_v, w_s, b_s,
              w_a2, b_a, g_o, w_pa, w_pb, w_o, g_post, w_pg, w_pe):
    hp, hs = x_prompt, x_sample
    sp_list, ss_list, cv_list = [], [], []
    for i in range(DEPTH):
        wts = (g_pre[i], w_in[i], g_v[i], w_s[i], b_s[i], w_a2[i], b_a[i], g_o[i],
               w_pa[i], w_pb[i], w_o[i], g_post[i], w_pg[i], w_pe[i])
        s0_prompt = jnp.zeros((hp.shape[0], HB, DK, DV), hp.dtype)
        hp, s_p, _ = layer(hp, p_prompt[i], s0_prompt, *wts)
        hs, s_s, v_s = layer(hs, p_sample[i], state_gla[i], *wts)
        sp_list.append(s_p)
        ss_list.append(s_s)
        cv_list.append(v_s)
    state_gla_prompt = jnp.stack(sp_list)
    state_gla_sample = jnp.stack(ss_list)
    chunk_v_sample = jnp.stack(cv_list)
    return (hp, hs, state_gla_prompt, state_gla_sample, chunk_v_sample)
```

```python
import functools

import jax
import jax.numpy as jnp
from jax import lax
from jax.experimental import pallas as pl
from jax.experimental.pallas import tpu as pltpu

D_MODEL = 2048
E_A = D_MODEL // 2
HA = 4
DA = E_A // HA
CHUNK = 128
HB = 4
E_B = D_MODEL // 2
DV = E_B // HB
DK = DV // 2
K_B = HB * DK
GATE_RANK = 16
GATE_TAU = 16.0
GLA_SUB = 16
PLE_DIM = 256
EPS = 1e-6

LANES = 128
OFF_U = 0
OFF_V = OFF_U + E_A
OFF_Z = OFF_V + E_A
OFF_Q = OFF_Z + E_A
OFF_K = OFF_Q + K_B
OFF_VB = OFF_K + K_B
OFF_ZB = OFF_VB + E_B
OFF_A = OFF_ZB + E_B
A_PAD = LANES
OFF_MA = OFF_A + A_PAD
OFF_MB = OFF_MA + D_MODEL
N_PAD = OFF_MB + D_MODEL

F32 = jnp.float32
BF16 = jnp.bfloat16

VMEM_LIMIT = 56 * 1024 * 1024


def _dot(a, b):
    return jnp.dot(a, b, preferred_element_type=F32)


def _dot_nt(a, b):
    return lax.dot_general(a, b, (((1,), (1,)), ((), ())), preferred_element_type=F32)


def _sigmoid(x):
    return 1.0 / (1.0 + jnp.exp(-x))


def _rms(x, g):
    return x * lax.rsqrt(jnp.mean(x * x, axis=-1, keepdims=True) + EPS) * g


def _split_bf16(x):
    hi = x.astype(BF16)
    lo = (x - hi.astype(F32)).astype(BF16)
    return hi, lo


def _log_decay(a_lr, w_a2, b_a):
    pre = _dot(a_lr, w_a2) + b_a
    return (jnp.minimum(pre, 0.0) - jnp.log(1.0 + jnp.exp(-jnp.abs(pre)))) * (1.0 / GATE_TAU)


def _bcast_rows(b, idxs, seg):
    parts = []
    for i in idxs:
        if i < 0:
            parts.append(jnp.zeros((seg, b.shape[1]), b.dtype))
        else:
            parts.append(jnp.broadcast_to(b[i:i + 1, :], (seg, b.shape[1])))
    return parts[0] if len(parts) == 1 else jnp.concatenate(parts, axis=0)


def _inproj_kernel(x_ref, g_ref, w_ref, o_ref, xn_ref):
    @pl.when(pl.program_id(1) == 0)
    def _():
        xn_ref[...] = _rms(x_ref[...], g_ref[...]).astype(BF16)

    o_ref[...] = _dot(xn_ref[...], w_ref[...]).astype(o_ref.dtype)


def _inproj(x2d, g_pre, w_in_p, tm, tn):
    t = x2d.shape[0]
    return pl.pallas_call(
        _inproj_kernel,
        grid=(t // tm, N_PAD // tn),
        in_specs=[
            pl.BlockSpec((tm, D_MODEL), lambda i, j: (i, 0)),
            pl.BlockSpec((1, D_MODEL), lambda i, j: (0, 0)),
            pl.BlockSpec((D_MODEL, tn), lambda i, j: (0, j)),
        ],
        out_specs=pl.BlockSpec((tm, tn), lambda i, j: (i, j)),
        out_shape=jax.ShapeDtypeStruct((t, N_PAD), BF16),
        scratch_shapes=[pltpu.VMEM((tm, D_MODEL), BF16)],
        compiler_params=pltpu.CompilerParams(
            dimension_semantics=("arbitrary", "arbitrary"),
            vmem_limit_bytes=VMEM_LIMIT),
        name="inproj",
    )(x2d, g_pre, w_in_p)


def _gla_levels(c):
    levels = []
    blk = c
    while blk > GLA_SUB:
        levels.append((blk, [j * blk + blk // 2 - 1 for j in range(c // blk)]))
        blk //= 2
    diag = [j * GLA_SUB - 1 for j in range(c // GLA_SUB)]
    return levels, diag


def _gla_masks(c):
    t = lax.broadcasted_iota(jnp.int32, (c, c), 0)
    s = lax.broadcasted_iota(jnp.int32, (c, c), 1)
    levels, _ = _gla_levels(c)
    masks = []
    for blk, _ in levels:
        sh = blk.bit_length() - 1
        half = blk // 2
        same = (t >> sh) == (s >> sh)
        masks.append(same & ((t & (blk - 1)) >= half) & ((s & (blk - 1)) < half))
    sh = GLA_SUB.bit_length() - 1
    masks.append(((t >> sh) == (s >> sh)) & (s <= t))
    return masks


def _gla_chunk(q, k, vb, log_a, s_heads, tril_ones, masks):
    c = q.shape[0]
    hi, lo = _split_bf16(log_a)
    b = _dot(tril_ones, hi) + _dot(tril_ones, lo)
    levels, diag = _gla_levels(c)

    qs, ks = [], []
    for blk, idxs in levels:
        ref = _bcast_rows(b, idxs, blk)
        qs.append((q * jnp.exp(jnp.minimum(b - ref, 0.0))).astype(BF16))
        ks.append((k * jnp.exp(jnp.minimum(ref - b, 0.0))).astype(BF16))
    ref = _bcast_rows(b, diag, GLA_SUB)
    qs.append((q * jnp.exp(b - ref)).astype(BF16))
    ks.append((k * jnp.exp(ref - b)).astype(BF16))

    q_in = (q * jnp.exp(b)).astype(BF16)
    b_last = b[c - 1:c, :]
    k_out = k * jnp.exp(b_last - b)

    outs, new_states = [], []
    for h in range(HB):
        kc = slice(h * DK, (h + 1) * DK)
        vc = slice(h * DV, (h + 1) * DV)
        att = jnp.zeros((c, c), F32)
        for ql, kl, m in zip(qs, ks, masks):
            att = jnp.where(m, _dot_nt(ql[:, kc], kl[:, kc]), att)
        v_h = vb[:, vc]
        s_h = s_heads[h]
        o_h = _dot(att.astype(BF16), v_h) + _dot(q_in[:, kc], s_h.astype(BF16))
        b_t = jnp.transpose(b[:, kc])
        decay = jnp.exp(jnp.broadcast_to(b_t[:, c - 1:c], (DK, DV)))
        k_t = jnp.transpose(k_out[:, kc]).astype(BF16)
        new_states.append(decay * s_h + _dot(k_t, v_h))
        outs.append(o_h)
    return jnp.concatenate(outs, axis=1), new_states


def _gla_sample_kernel(q_ref, k_ref, vb_ref, a_ref, w_a2_ref, b_a_ref, s0_ref,
                       o_ref, s1_ref, *, nb, seq):
    rows = nb * seq
    log_a = _log_decay(a_ref[...], w_a2_ref[...], b_a_ref[...])
    t = lax.broadcasted_iota(jnp.int32, (rows, rows), 0)
    s = lax.broadcasted_iota(jnp.int32, (rows, rows), 1)
    sh = seq.bit_length() - 1
    causal = ((t >> sh) == (s >> sh)) & (s <= t)
    tril_ones = jnp.where(causal, 1.0, 0.0).astype(BF16)
    hi, lo = _split_bf16(log_a)
    b = _dot(tril_ones, hi) + _dot(tril_ones, lo)

    q = q_ref[...].astype(F32) * (DK ** -0.5)
    k = k_ref[...].astype(F32)
    vb = vb_ref[...]
    q_t = q * jnp.exp(b)
    k_t = (k * jnp.exp(-b)).astype(BF16)
    b_last = _bcast_rows(b, [n * seq + seq - 1 for n in range(nb)], seq)
    k_out = k * jnp.exp(b_last - b)
    col = lax.broadcasted_iota(jnp.int32, (DK, rows), 1) >> sh

    outs = []
    for h in range(HB):
        kc = slice(h * DK, (h + 1) * DK)
        vc = slice(h * DV, (h + 1) * DV)
        v_h = vb[:, vc]
        att = jnp.where(causal, _dot_nt(q_t[:, kc].astype(BF16), k_t[:, kc]), 0.0)
        o_intra = _dot(att.astype(BF16), v_h)
        b_tr = jnp.transpose(b[:, kc])
        k_tr = jnp.transpose(k_out[:, kc])
        o_inter = []
        for n in range(nb):
            s_n = s0_ref[n, h]
            r0 = n * seq
            o_inter.append(_dot(q_t[r0:r0 + seq, kc], s_n))
            decay = jnp.exp(jnp.broadcast_to(b_tr[:, r0 + seq - 1:r0 + seq], (DK, DV)))
            k_n = jnp.where(col == n, k_tr, 0.0).astype(BF16)
            s1_ref[n, h] = decay * s_n + _dot(k_n, v_h)
        outs.append(o_intra + jnp.concatenate(o_inter, axis=0))
    o_ref[...] = jnp.concatenate(outs, axis=1)


def _gla_sample(proj, w_a2_p, b_a, s0, nb, seq):
    n_batch = s0.shape[0]
    rows = nb * seq
    kern = functools.partial(_gla_sample_kernel, nb=nb, seq=seq)
    return pl.pallas_call(
        kern,
        grid=(n_batch // nb,),
        in_specs=[
            pl.BlockSpec((rows, K_B), lambda i: (i, OFF_Q // K_B)),
            pl.BlockSpec((rows, K_B), lambda i: (i, OFF_K // K_B)),
            pl.BlockSpec((rows, E_B), lambda i: (i, OFF_VB // E_B)),
            pl.BlockSpec((rows, A_PAD), lambda i: (i, OFF_A // A_PAD)),
            pl.BlockSpec((A_PAD, K_B), lambda i: (0, 0)),
            pl.BlockSpec((1, K_B), lambda i: (0, 0)),
            pl.BlockSpec((nb, HB, DK, DV), lambda i: (i, 0, 0, 0)),
        ],
        out_specs=[
            pl.BlockSpec((rows, E_B), lambda i: (i, 0)),
            pl.BlockSpec((nb, HB, DK, DV), lambda i: (i, 0, 0, 0)),
        ],
        out_shape=[
            jax.ShapeDtypeStruct((n_batch * seq, E_B), F32),
            jax.ShapeDtypeStruct(s0.shape, F32),
        ],
        compiler_params=pltpu.CompilerParams(
            dimension_semantics=("arbitrary",),
            vmem_limit_bytes=VMEM_LIMIT),
        name="gla_sample",
    )(proj, proj, proj, proj, w_a2_p, b_a, s0)


def _post_kernel(*refs, tm, seq, prompt):
    if prompt:
        (proj_ref, x_ref, p_ref, g_v_ref, w_s_ref, b_col_ref, w_a2_ref, b_a_ref, g_o_ref,
         w_pa_ref, w_pb_ref, w_o_ref, g_post_ref, w_pg_ref, w_pe_ref,
         y_ref, s_out_ref, s_ref) = refs
    else:
        (proj_ref, x_ref, p_ref, o_ref, g_v_ref, w_s_ref, b_col_ref, g_o_ref,
         w_pa_ref, w_pb_ref, w_o_ref, g_post_ref, w_pg_ref, w_pe_ref,
         y_ref, vn_ref) = refs

    def cols(off, width):
        return proj_ref[:, off:off + width]

    vn = _rms(cols(OFF_V, E_A).astype(F32), g_v_ref[...])
    if not prompt:
        vn_ref[...] = vn
    vn_b = vn.astype(BF16)
    t = lax.broadcasted_iota(jnp.int32, (CHUNK, CHUNK), 0)
    s = lax.broadcasted_iota(jnp.int32, (CHUNK, CHUNK), 1)
    causal = s <= t
    if seq < CHUNK:
        sh = seq.bit_length() - 1
        causal = causal & ((t >> sh) == (s >> sh))
    w_sp = [jnp.where(causal, w_s_ref[g], 0.0).astype(BF16) for g in range(HA)]
    gate_rows = []
    for c in range(tm // CHUNK):
        r = slice(c * CHUNK, (c + 1) * CHUNK)
        gate_rows.append(jnp.concatenate(
            [_dot(w_sp[g], vn_b[r, g * DA:(g + 1) * DA]) + b_col_ref[:, g:g + 1]
             for g in range(HA)], axis=1))
    gate = gate_rows[0] if len(gate_rows) == 1 else jnp.concatenate(gate_rows, axis=0)
    z = cols(OFF_Z, E_A).astype(F32)
    y_a = cols(OFF_U, E_A).astype(F32) * gate * (z * _sigmoid(z))

    if prompt:
        @pl.when(pl.program_id(1) == 0)
        def _():
            s_ref[...] = jnp.zeros_like(s_ref)

        log_a = _log_decay(cols(OFF_A, A_PAD), w_a2_ref[...], b_a_ref[...])
        q = cols(OFF_Q, K_B).astype(F32) * (DK ** -0.5)
        k = cols(OFF_K, K_B).astype(F32)
        vb = cols(OFF_VB, E_B)
        tril_ones = jnp.where(s <= t, 1.0, 0.0).astype(BF16)
        masks = _gla_masks(CHUNK)
        states = [s_ref[h] for h in range(HB)]
        o_rows = []
        for c in range(tm // CHUNK):
            r = slice(c * CHUNK, (c + 1) * CHUNK)
            o_c, states = _gla_chunk(q[r], k[r], vb[r], log_a[r], states, tril_ones, masks)
            o_rows.append(o_c)
        for h in range(HB):
            s_ref[h] = states[h]
        o = o_rows[0] if len(o_rows) == 1 else jnp.concatenate(o_rows, axis=0)

        @pl.when(pl.program_id(1) == pl.num_programs(1) - 1)
        def _():
            s_out_ref[0] = s_ref[...]
    else:
        o = o_ref[...]

    g_o = g_o_ref[...]
    on = jnp.concatenate(
        [_rms(o[:, h * DV:(h + 1) * DV], g_o[:, h * DV:(h + 1) * DV]) for h in range(HB)], axis=1)
    zb = cols(OFF_ZB, E_B).astype(F32)
    y_b = on * (zb * _sigmoid(zb))

    m = (_sigmoid(cols(OFF_MA, D_MODEL).astype(F32)) * _dot(y_a.astype(BF16), w_pa_ref[...])
         + _sigmoid(cols(OFF_MB, D_MODEL).astype(F32)) * _dot(y_b.astype(BF16), w_pb_ref[...]))
    h_res = x_ref[...] + _rms(_dot(m.astype(BF16), w_o_ref[...]), g_post_ref[...])
    pg = _sigmoid(_dot(h_res.astype(BF16), w_pg_ref[...]))
    y_ref[...] = h_res + pg * _dot(p_ref[...].astype(BF16), w_pe_ref[...])


def _const_spec(shape):
    nd = len(shape)
    return pl.BlockSpec(shape, lambda *_: (0,) * nd, pipeline_mode=pl.Buffered(1))


def _post_prompt(proj, x2d, p2d, wts, n_batch, seq, tm):
    nt = seq // tm
    kern = functools.partial(_post_kernel, tm=tm, seq=seq, prompt=True)
    row = lambda b, i: (b * nt + i, 0)
    names = ("g_v", "w_s", "b_col", "w_a2", "b_a", "g_o", "w_pa", "w_pb", "w_o", "g_post", "w_pg", "w_pe")
    w_args = [wts[n] for n in names]
    return pl.pallas_call(
        kern,
        grid=(n_batch, nt),
        in_specs=[
            pl.BlockSpec((tm, N_PAD), row),
            pl.BlockSpec((tm, D_MODEL), row),
            pl.BlockSpec((tm, PLE_DIM), row),
        ] + [_const_spec(w.shape) for w in w_args],
        out_specs=[
            pl.BlockSpec((tm, D_MODEL), row),
            pl.BlockSpec((1, HB, DK, DV), lambda b, i: (b, 0, 0, 0)),
        ],
        out_shape=[
            jax.ShapeDtypeStruct((n_batch * seq, D_MODEL), F32),
            jax.ShapeDtypeStruct((n_batch, HB, DK, DV), F32),
        ],
        scratch_shapes=[pltpu.VMEM((HB, DK, DV), F32)],
        compiler_params=pltpu.CompilerParams(
            dimension_semantics=("arbitrary", "arbitrary"),
            vmem_limit_bytes=VMEM_LIMIT),
        name="post_prompt",
    )(proj, x2d, p2d, *w_args)


def _post_sample(proj, x2d, p2d, o_gla, wts, seq, tm):
    t = x2d.shape[0]
    kern = functools.partial(_post_kernel, tm=tm, seq=seq, prompt=False)
    row = lambda i: (i, 0)
    names = ("g_v", "w_s_bd", "b_col_bd", "g_o", "w_pa", "w_pb", "w_o", "g_post", "w_pg", "w_pe")
    w_args = [wts[n] for n in names]
    return pl.pallas_call(
        kern,
        grid=(t // tm,),
        in_specs=[
            pl.BlockSpec((tm, N_PAD), row),
            pl.BlockSpec((tm, D_MODEL), row),
            pl.BlockSpec((tm, PLE_DIM), row),
            pl.BlockSpec((tm, E_B), row),
        ] + [_const_spec(w.shape) for w in w_args],
        out_specs=[
            pl.BlockSpec((tm, D_MODEL), row),
            pl.BlockSpec((tm, E_A), row),
        ],
        out_shape=[
            jax.ShapeDtypeStruct((t, D_MODEL), F32),
            jax.ShapeDtypeStruct((t, E_A), F32),
        ],
        compiler_params=pltpu.CompilerParams(
            dimension_semantics=("arbitrary",),
            vmem_limit_bytes=VMEM_LIMIT),
        name="post_sample",
    )(proj, x2d, p2d, o_gla, *w_args)


def _layer_weights(i, dec_seq, g_pre, w_in, g_v, w_s, b_s, w_a2, b_a, g_o,
                   w_pa, w_pb, w_o, g_post, w_pg, w_pe):
    w = w_in[i]
    a_end = OFF_A + GATE_RANK
    w_in_p = jnp.concatenate(
        [w[:, :a_end], jnp.zeros((D_MODEL, A_PAD - GATE_RANK), w.dtype), w[:, a_end:]],
        axis=1).astype(BF16)
    w_a2_p = jnp.concatenate(
        [w_a2[i], jnp.zeros((A_PAD - GATE_RANK, K_B), w_a2.dtype)], axis=0).astype(BF16)
    reps = CHUNK // dec_seq
    return dict(
        g_pre=g_pre[i][None, :],
        w_in=w_in_p,
        g_v=g_v[i][None, :],
        w_s=w_s[i],
        b_col=b_s[i].T,
        w_s_bd=jnp.tile(w_s[i][:, :dec_seq, :dec_seq], (1, reps, reps)),
        b_col_bd=jnp.tile(b_s[i][:, :dec_seq].T, (reps, 1)),
        w_a2=w_a2_p,
        b_a=b_a[i][None, :],
        g_o=g_o[i].reshape(1, E_B),
        w_pa=w_pa[i].astype(BF16),
        w_pb=w_pb[i].astype(BF16),
        w_o=w_o[i].astype(BF16),
        g_post=g_post[i][None, :],
        w_pg=w_pg[i].astype(BF16),
        w_pe=w_pe[i].astype(BF16),
    )


def kernel(x_prompt, x_sample, p_prompt, p_sample, state_gla, g_pre, w_in, g_v, w_s, b_s,
           w_a2, b_a, g_o, w_pa, w_pb, w_o, g_post, w_pg, w_pe):
    depth = w_in.shape[0]
    n_batch, seq, _ = x_prompt.shape
    dec_batch, dec_seq, _ = x_sample.shape
    hp = x_prompt.reshape(n_batch * seq, D_MODEL)
    hs = x_sample.reshape(dec_batch * dec_seq, D_MODEL)
    sp_list, ss_list, cv_list = [], [], []
    for i in range(depth):
        wts = _layer_weights(i, dec_seq, g_pre, w_in, g_v, w_s, b_s, w_a2, b_a, g_o,
                             w_pa, w_pb, w_o, g_post, w_pg, w_pe)
        proj_p = _inproj(hp, wts["g_pre"], wts["w_in"], tm=1024, tn=1152)
        proj_s = _inproj(hs, wts["g_pre"], wts["w_in"], tm=1024, tn=1152)
        o_s, s_s = _gla_sample(proj_s, wts["w_a2"], wts["b_a"], state_gla[i],
                               nb=CHUNK // dec_seq, seq=dec_seq)
        hp, s_p = _post_prompt(proj_p, hp, p_prompt[i].reshape(n_batch * seq, PLE_DIM),
                               wts, n_batch, seq, tm=256)
        hs, v_s = _post_sample(proj_s, hs, p_sample[i].reshape(dec_batch * dec_seq, PLE_DIM),
                               o_s, wts, dec_seq, tm=256)
        sp_list.append(s_p)
        ss_list.append(s_s)
        cv_list.append(v_s.reshape(dec_batch, dec_seq, E_A))
    return (hp.reshape(n_batch, seq, D_MODEL),
            hs.reshape(dec_batch, dec_seq, D_MODEL),
            jnp.stack(sp_list), jnp.stack(ss_list), jnp.stack(cv_list))
```

```python
import functools

import jax
import jax.numpy as jnp
from jax import lax
from jax.experimental import pallas as pl
from jax.experimental.pallas import tpu as pltpu

D_MODEL = 2048
E_A = D_MODEL // 2
HA = 4
DA = E_A // HA
CHUNK = 128
HB = 4
E_B = D_MODEL // 2
DV = E_B // HB
DK = DV // 2
K_B = HB * DK
GATE_RANK = 16
GATE_TAU = 16.0
GLA_SUB = 16
PLE_DIM = 256
EPS = 1e-6

LANES = 128
OFF_U = 0
OFF_V = OFF_U + E_A
OFF_Z = OFF_V + E_A
OFF_Q = OFF_Z + E_A
OFF_K = OFF_Q + K_B
OFF_VB = OFF_K + K_B
OFF_ZB = OFF_VB + E_B
OFF_A = OFF_ZB + E_B
A_PAD = LANES
OFF_MA = OFF_A + GATE_RANK
OFF_MB = OFF_MA + D_MODEL
N_IN = OFF_MB + D_MODEL
N_PAD = -(-N_IN // LANES) * LANES
INPROJ_TN = 1152

F32 = jnp.float32
BF16 = jnp.bfloat16

VMEM_LIMIT = 56 * 1024 * 1024


def _dot(a, b):
    return jnp.dot(a, b, preferred_element_type=F32)


def _dot_nt(a, b):
    return lax.dot_general(a, b, (((1,), (1,)), ((), ())), preferred_element_type=F32)


def _sigmoid(x):
    return 1.0 / (1.0 + jnp.exp(-x))


def _rms(x, g):
    return x * lax.rsqrt(jnp.mean(x * x, axis=-1, keepdims=True) + EPS) * g


def _split_bf16(x):
    hi = x.astype(BF16)
    lo = (x - hi.astype(F32)).astype(BF16)
    return hi, lo


def _log_decay(a_lr, w_a2, b_a):
    pre = _dot(a_lr, w_a2) + b_a
    return (jnp.minimum(pre, 0.0) - jnp.log(1.0 + jnp.exp(-jnp.abs(pre)))) * (1.0 / GATE_TAU)


def _bcast_rows(b, idxs, seg):
    parts = []
    for i in idxs:
        if i < 0:
            parts.append(jnp.zeros((seg, b.shape[1]), b.dtype))
        else:
            parts.append(jnp.broadcast_to(b[i:i + 1, :], (seg, b.shape[1])))
    return parts[0] if len(parts) == 1 else jnp.concatenate(parts, axis=0)


def _inproj_kernel(x_ref, g_ref, w_ref, w_tail_ref, o_ref, xn_ref):
    j = pl.program_id(1)
    last = pl.num_programs(1) - 1

    @pl.when(j == 0)
    def _():
        xn_ref[...] = _rms(x_ref[...], g_ref[...]).astype(BF16)

    @pl.when(j < last)
    def _():
        o_ref[...] = _dot(xn_ref[...], w_ref[...].astype(BF16)).astype(o_ref.dtype)

    @pl.when(j == last)
    def _():
        o_ref[...] = _dot(xn_ref[...], w_tail_ref[...]).astype(o_ref.dtype)


def _inproj(x2d, g_pre, w_in, w_tail, tm, tn):
    t = x2d.shape[0]
    nj = N_PAD // tn
    return pl.pallas_call(
        _inproj_kernel,
        grid=(t // tm, nj),
        in_specs=[
            pl.BlockSpec((tm, D_MODEL), lambda i, j: (i, 0)),
            pl.BlockSpec((1, D_MODEL), lambda i, j: (0, 0)),
            pl.BlockSpec((D_MODEL, tn), lambda i, j: (0, jnp.minimum(j, nj - 2))),
            pl.BlockSpec((D_MODEL, tn), lambda i, j: (0, 0), pipeline_mode=pl.Buffered(1)),
        ],
        out_specs=pl.BlockSpec((tm, tn), lambda i, j: (i, j)),
        out_shape=jax.ShapeDtypeStruct((t, N_PAD), BF16),
        scratch_shapes=[pltpu.VMEM((tm, D_MODEL), BF16)],
        compiler_params=pltpu.CompilerParams(
            dimension_semantics=("arbitrary", "arbitrary"),
            vmem_limit_bytes=VMEM_LIMIT),
        name="inproj",
    )(x2d, g_pre, w_in, w_tail)


def _gla_levels(c):
    levels = []
    blk = c
    while blk > GLA_SUB:
        levels.append((blk, [j * blk + blk // 2 - 1 for j in range(c // blk)]))
        blk //= 2
    diag = [j * GLA_SUB - 1 for j in range(c // GLA_SUB)]
    return levels, diag


def _gla_masks(c):
    t = lax.broadcasted_iota(jnp.int32, (c, c), 0)
    s = lax.broadcasted_iota(jnp.int32, (c, c), 1)
    levels, _ = _gla_levels(c)
    masks = []
    for blk, _ in levels:
        sh = blk.bit_length() - 1
        half = blk // 2
        same = (t >> sh) == (s >> sh)
        masks.append(same & ((t & (blk - 1)) >= half) & ((s & (blk - 1)) < half))
    sh = GLA_SUB.bit_length() - 1
    masks.append(((t >> sh) == (s >> sh)) & (s <= t))
    return masks


def _gla_chunk(q, k, vb, log_a, s_heads, tril_ones, masks):
    c = q.shape[0]
    hi, lo = _split_bf16(log_a)
    b = _dot(tril_ones, hi) + _dot(tril_ones, lo)
    levels, diag = _gla_levels(c)

    qs, ks = [], []
    for blk, idxs in levels:
        ref = _bcast_rows(b, idxs, blk)
        qs.append((q * jnp.exp(jnp.minimum(b - ref, 0.0))).astype(BF16))
        ks.append((k * jnp.exp(jnp.minimum(ref - b, 0.0))).astype(BF16))
    ref = _bcast_rows(b, diag, GLA_SUB)
    qs.append((q * jnp.exp(b - ref)).astype(BF16))
    ks.append((k * jnp.exp(ref - b)).astype(BF16))

    q_in = (q * jnp.exp(b)).astype(BF16)
    b_last = b[c - 1:c, :]
    k_out = k * jnp.exp(b_last - b)

    outs, new_states = [], []
    for h in range(HB):
        kc = slice(h * DK, (h + 1) * DK)
        vc = slice(h * DV, (h + 1) * DV)
        att = jnp.zeros((c, c), F32)
        for ql, kl, m in zip(qs, ks, masks):
            att = jnp.where(m, _dot_nt(ql[:, kc], kl[:, kc]), att)
        v_h = vb[:, vc]
        s_h = s_heads[h]
        o_h = _dot(att.astype(BF16), v_h) + _dot(q_in[:, kc], s_h.astype(BF16))
        b_t = jnp.transpose(b[:, kc])
        decay = jnp.exp(jnp.broadcast_to(b_t[:, c - 1:c], (DK, DV)))
        k_t = jnp.transpose(k_out[:, kc]).astype(BF16)
        new_states.append(decay * s_h + _dot(k_t, v_h))
        outs.append(o_h)
    return jnp.concatenate(outs, axis=1), new_states


def _gla_sample_kernel(q_ref, k_ref, vb_ref, a_ref, w_a2_ref, b_a_ref, s0_ref,
                       o_ref, s1_ref, *, nb, seq):
    rows = nb * seq
    log_a = _log_decay(a_ref[...], w_a2_ref[...], b_a_ref[...])
    t = lax.broadcasted_iota(jnp.int32, (rows, rows), 0)
    s = lax.broadcasted_iota(jnp.int32, (rows, rows), 1)
    sh = seq.bit_length() - 1
    causal = ((t >> sh) == (s >> sh)) & (s <= t)
    tril_ones = jnp.where(causal, 1.0, 0.0).astype(BF16)
    hi, lo = _split_bf16(log_a)
    b = _dot(tril_ones, hi) + _dot(tril_ones, lo)

    q = q_ref[...].astype(F32) * (DK ** -0.5)
    k = k_ref[...].astype(F32)
    vb = vb_ref[...]
    q_t = q * jnp.exp(b)
    k_t = (k * jnp.exp(-b)).astype(BF16)
    b_last = _bcast_rows(b, [n * seq + seq - 1 for n in range(nb)], seq)
    k_out = k * jnp.exp(b_last - b)
    col = lax.broadcasted_iota(jnp.int32, (DK, rows), 1) >> sh

    outs = []
    for h in range(HB):
        kc = slice(h * DK, (h + 1) * DK)
        vc = slice(h * DV, (h + 1) * DV)
        v_h = vb[:, vc]
        att = jnp.where(causal, _dot_nt(q_t[:, kc].astype(BF16), k_t[:, kc]), 0.0)
        o_intra = _dot(att.astype(BF16), v_h)
        b_tr = jnp.transpose(b[:, kc])
        k_tr = jnp.transpose(k_out[:, kc])
        o_inter = []
        for n in range(nb):
            s_n = s0_ref[n, h]
            r0 = n * seq
            o_inter.append(_dot(q_t[r0:r0 + seq, kc], s_n))
            decay = jnp.exp(jnp.broadcast_to(b_tr[:, r0 + seq - 1:r0 + seq], (DK, DV)))
            k_n = jnp.where(col == n, k_tr, 0.0).astype(BF16)
            s1_ref[n, h] = decay * s_n + _dot(k_n, v_h)
        outs.append(o_intra + jnp.concatenate(o_inter, axis=0))
    o_ref[...] = jnp.concatenate(outs, axis=1)


def _gla_sample(proj, w_a2_p, b_a, s0, nb, seq):
    n_batch = s0.shape[0]
    rows = nb * seq
    kern = functools.partial(_gla_sample_kernel, nb=nb, seq=seq)
    return pl.pallas_call(
        kern,
        grid=(n_batch // nb,),
        in_specs=[
            pl.BlockSpec((rows, K_B), lambda i: (i, OFF_Q // K_B)),
            pl.BlockSpec((rows, K_B), lambda i: (i, OFF_K // K_B)),
            pl.BlockSpec((rows, E_B), lambda i: (i, OFF_VB // E_B)),
            pl.BlockSpec((rows, A_PAD), lambda i: (i, OFF_A // A_PAD)),
            pl.BlockSpec((A_PAD, K_B), lambda i: (0, 0)),
            pl.BlockSpec((1, K_B), lambda i: (0, 0)),
            pl.BlockSpec((nb, HB, DK, DV), lambda i: (i, 0, 0, 0)),
        ],
        out_specs=[
            pl.BlockSpec((rows, E_B), lambda i: (i, 0)),
            pl.BlockSpec((nb, HB, DK, DV), lambda i: (i, 0, 0, 0)),
        ],
        out_shape=[
            jax.ShapeDtypeStruct((n_batch * seq, E_B), F32),
            jax.ShapeDtypeStruct(s0.shape, F32),
        ],
        compiler_params=pltpu.CompilerParams(
            dimension_semantics=("arbitrary",),
            vmem_limit_bytes=VMEM_LIMIT),
        name="gla_sample",
    )(proj, proj, proj, proj, w_a2_p, b_a, s0)


def _post_kernel(*refs, tm, seq, prompt):
    if prompt:
        (proj_ref, x_ref, p_ref, g_v_ref, w_s_ref, b_col_ref, w_a2_ref, b_a_ref, g_o_ref,
         w_pa_ref, w_pb_ref, w_o_ref, g_post_ref, w_pg_ref, w_pe_ref,
         y_ref, s_out_ref, s_ref) = refs
    else:
        (proj_ref, x_ref, p_ref, o_ref, g_v_ref, w_s_ref, b_col_ref, g_o_ref,
         w_pa_ref, w_pb_ref, w_o_ref, g_post_ref, w_pg_ref, w_pe_ref,
         y_ref, vn_ref) = refs

    def cols(off, width):
        return proj_ref[:, off:off + width]

    vn = _rms(cols(OFF_V, E_A).astype(F32), g_v_ref[...])
    if not prompt:
        vn_ref[...] = vn
    vn_b = vn.astype(BF16)
    t = lax.broadcasted_iota(jnp.int32, (CHUNK, CHUNK), 0)
    s = lax.broadcasted_iota(jnp.int32, (CHUNK, CHUNK), 1)
    causal = s <= t
    if seq < CHUNK:
        sh = seq.bit_length() - 1
        causal = causal & ((t >> sh) == (s >> sh))
    w_sp = [jnp.where(causal, w_s_ref[g], 0.0).astype(BF16) for g in range(HA)]
    gate_rows = []
    for c in range(tm // CHUNK):
        r = slice(c * CHUNK, (c + 1) * CHUNK)
        gate_rows.append(jnp.concatenate(
            [_dot(w_sp[g], vn_b[r, g * DA:(g + 1) * DA]) + b_col_ref[:, g:g + 1]
             for g in range(HA)], axis=1))
    gate = gate_rows[0] if len(gate_rows) == 1 else jnp.concatenate(gate_rows, axis=0)
    z = cols(OFF_Z, E_A).astype(F32)
    y_a = cols(OFF_U, E_A).astype(F32) * gate * (z * _sigmoid(z))

    if prompt:
        @pl.when(pl.program_id(1) == 0)
        def _():
            s_ref[...] = jnp.zeros_like(s_ref)

        log_a = _log_decay(cols(OFF_A, A_PAD), w_a2_ref[...], b_a_ref[...])
        q = cols(OFF_Q, K_B).astype(F32) * (DK ** -0.5)
        k = cols(OFF_K, K_B).astype(F32)
        vb = cols(OFF_VB, E_B)
        tril_ones = jnp.where(s <= t, 1.0, 0.0).astype(BF16)
        masks = _gla_masks(CHUNK)
        states = [s_ref[h] for h in range(HB)]
        o_rows = []
        for c in range(tm // CHUNK):
            r = slice(c * CHUNK, (c + 1) * CHUNK)
            o_c, states = _gla_chunk(q[r], k[r], vb[r], log_a[r], states, tril_ones, masks)
            o_rows.append(o_c)
        for h in range(HB):
            s_ref[h] = states[h]
        o = o_rows[0] if len(o_rows) == 1 else jnp.concatenate(o_rows, axis=0)

        @pl.when(pl.program_id(1) == pl.num_programs(1) - 1)
        def _():
            s_out_ref[0] = s_ref[...]
    else:
        o = o_ref[...]

    g_o = g_o_ref[...]
    on = jnp.concatenate(
        [_rms(o[:, h * DV:(h + 1) * DV], g_o[:, h * DV:(h + 1) * DV]) for h in range(HB)], axis=1)
    zb = cols(OFF_ZB, E_B).astype(F32)
    y_b = on * (zb * _sigmoid(zb))

    m = (_sigmoid(cols(OFF_MA, D_MODEL).astype(F32)) * _dot(y_a.astype(BF16), w_pa_ref[...])
         + _sigmoid(cols(OFF_MB, D_MODEL).astype(F32)) * _dot(y_b.astype(BF16), w_pb_ref[...]))
    h_res = x_ref[...] + _rms(_dot(m.astype(BF16), w_o_ref[...]), g_post_ref[...])
    pg = _sigmoid(_dot(h_res.astype(BF16), w_pg_ref[...]))
    y_ref[...] = h_res + pg * _dot(p_ref[...].astype(BF16), w_pe_ref[...])


def _const_spec(shape):
    nd = len(shape)
    return pl.BlockSpec(shape, lambda *_: (0,) * nd, pipeline_mode=pl.Buffered(1))


def _post_prompt(proj, x2d, p2d, wts, n_batch, seq, tm):
    nt = seq // tm
    kern = functools.partial(_post_kernel, tm=tm, seq=seq, prompt=True)
    row = lambda b, i: (b * nt + i, 0)
    names = ("g_v", "w_s", "b_col", "w_a2", "b_a", "g_o", "w_pa", "w_pb", "w_o", "g_post", "w_pg", "w_pe")
    w_args = [wts[n] for n in names]
    return pl.pallas_call(
        kern,
        grid=(n_batch, nt),
        in_specs=[
            pl.BlockSpec((tm, N_PAD), row),
            pl.BlockSpec((tm, D_MODEL), row),
            pl.BlockSpec((tm, PLE_DIM), row),
        ] + [_const_spec(w.shape) for w in w_args],
        out_specs=[
            pl.BlockSpec((tm, D_MODEL), row),
            pl.BlockSpec((1, HB, DK, DV), lambda b, i: (b, 0, 0, 0)),
        ],
        out_shape=[
            jax.ShapeDtypeStruct((n_batch * seq, D_MODEL), F32),
            jax.ShapeDtypeStruct((n_batch, HB, DK, DV), F32),
        ],
        scratch_shapes=[pltpu.VMEM((HB, DK, DV), F32)],
        compiler_params=pltpu.CompilerParams(
            dimension_semantics=("arbitrary", "arbitrary"),
            vmem_limit_bytes=VMEM_LIMIT),
        name="post_prompt",
    )(proj, x2d, p2d, *w_args)


def _post_sample(proj, x2d, p2d, o_gla, wts, seq, tm):
    t = x2d.shape[0]
    kern = functools.partial(_post_kernel, tm=tm, seq=seq, prompt=False)
    row = lambda i: (i, 0)
    names = ("g_v", "w_s_bd", "b_col_bd", "g_o", "w_pa", "w_pb", "w_o", "g_post", "w_pg", "w_pe")
    w_args = [wts[n] for n in names]
    return pl.pallas_call(
        kern,
        grid=(t // tm,),
        in_specs=[
            pl.BlockSpec((tm, N_PAD), row),
            pl.BlockSpec((tm, D_MODEL), row),
            pl.BlockSpec((tm, PLE_DIM), row),
            pl.BlockSpec((tm, E_B), row),
        ] + [_const_spec(w.shape) for w in w_args],
        out_specs=[
            pl.BlockSpec((tm, D_MODEL), row),
            pl.BlockSpec((tm, E_A), row),
        ],
        out_shape=[
            jax.ShapeDtypeStruct((t, D_MODEL), F32),
            jax.ShapeDtypeStruct((t, E_A), F32),
        ],
        compiler_params=pltpu.CompilerParams(
            dimension_semantics=("arbitrary",),
            vmem_limit_bytes=VMEM_LIMIT),
        name="post_sample",
    )(proj, x2d, p2d, o_gla, *w_args)


def _layer_weights(i, dec_seq, g_pre, w_in, g_v, w_s, b_s, w_a2, b_a, g_o,
                   w_pa, w_pb, w_o, g_post, w_pg, w_pe):
    w = w_in[i]
    tail0 = N_PAD - INPROJ_TN
    w_tail = jnp.pad(w[:, tail0:], ((0, 0), (0, N_PAD - N_IN))).astype(BF16)
    w_a2_p = jnp.concatenate(
        [w_a2[i], jnp.zeros((A_PAD - GATE_RANK, K_B), w_a2.dtype)], axis=0).astype(BF16)
    reps = CHUNK // dec_seq
    return dict(
        g_pre=g_pre[i][None, :],
        w_in=w,
        w_tail=w_tail,
        g_v=g_v[i][None, :],
        w_s=w_s[i],
        b_col=b_s[i].T,
        w_s_bd=jnp.tile(w_s[i][:, :dec_seq, :dec_seq], (1, reps, reps)),
        b_col_bd=jnp.tile(b_s[i][:, :dec_seq].T, (reps, 1)),
        w_a2=w_a2_p,
        b_a=b_a[i][None, :],
        g_o=g_o[i].reshape(1, E_B),
        w_pa=w_pa[i].astype(BF16),
        w_pb=w_pb[i].astype(BF16),
        w_o=w_o[i].astype(BF16),
        g_post=g_post[i][None, :],
        w_pg=w_pg[i].astype(BF16),
        w_pe=w_pe[i].astype(BF16),
    )


def kernel(x_prompt, x_sample, p_prompt, p_sample, state_gla, g_pre, w_in, g_v, w_s, b_s,
           w_a2, b_a, g_o, w_pa, w_pb, w_o, g_post, w_pg, w_pe):
    depth = w_in.shape[0]
    n_batch, seq, _ = x_prompt.shape
    dec_batch, dec_seq, _ = x_sample.shape
    hp = x_prompt.reshape(n_batch * seq, D_MODEL)
    hs = x_sample.reshape(dec_batch * dec_seq, D_MODEL)
    sp_list, ss_list, cv_list = [], [], []
    for i in range(depth):
        wts = _layer_weights(i, dec_seq, g_pre, w_in, g_v, w_s, b_s, w_a2, b_a, g_o,
                             w_pa, w_pb, w_o, g_post, w_pg, w_pe)
        proj_p = _inproj(hp, wts["g_pre"], wts["w_in"], wts["w_tail"], tm=1024, tn=INPROJ_TN)
        proj_s = _inproj(hs, wts["g_pre"], wts["w_in"], wts["w_tail"], tm=1024, tn=INPROJ_TN)
        o_s, s_s = _gla_sample(proj_s, wts["w_a2"], wts["b_a"], state_gla[i],
                               nb=CHUNK // dec_seq, seq=dec_seq)
        hp, s_p = _post_prompt(proj_p, hp, p_prompt[i].reshape(n_batch * seq, PLE_DIM),
                               wts, n_batch, seq, tm=256)
        hs, v_s = _post_sample(proj_s, hs, p_sample[i].reshape(dec_batch * dec_seq, PLE_DIM),
                               o_s, wts, dec_seq, tm=256)
        sp_list.append(s_p)
        ss_list.append(s_s)
        cv_list.append(v_s.reshape(dec_batch, dec_seq, E_A))
    stack = (lambda xs: xs[0][None]) if depth == 1 else jnp.stack
    return (hp.reshape(n_batch, seq, D_MODEL),
            hs.reshape(dec_batch, dec_seq, D_MODEL),
            stack(sp_list), stack(ss_list), stack(cv_list))
```

```python
import functools

import jax
import jax.numpy as jnp
from jax import lax
from jax.experimental import pallas as pl
from jax.experimental.pallas import tpu as pltpu

D_MODEL = 2048
E_A = D_MODEL // 2
HA = 4
DA = E_A // HA
CHUNK = 128
HB = 4
E_B = D_MODEL // 2
DV = E_B // HB
DK = DV // 2
K_B = HB * DK
GATE_RANK = 16
GATE_TAU = 16.0
GLA_SUB = 16
PLE_DIM = 256
EPS = 1e-6

LANES = 128
OFF_U = 0
OFF_V = OFF_U + E_A
OFF_Z = OFF_V + E_A
OFF_Q = OFF_Z + E_A
OFF_K = OFF_Q + K_B
OFF_VB = OFF_K + K_B
OFF_ZB = OFF_VB + E_B
OFF_MA = OFF_ZB + E_B
OFF_MB = OFF_MA + D_MODEL
N_MAIN = OFF_MB + D_MODEL
W_IN_A = OFF_MA
A_PAD = LANES
INPROJ_TN = 1024

F32 = jnp.float32
BF16 = jnp.bfloat16

VMEM_LIMIT = 56 * 1024 * 1024


def _dot(a, b):
    return jnp.dot(a, b, preferred_element_type=F32)


def _dot_nt(a, b):
    return lax.dot_general(a, b, (((1,), (1,)), ((), ())), preferred_element_type=F32)


def _sigmoid(x):
    return 1.0 / (1.0 + jnp.exp(-x))


def _rms(x, g):
    return x * lax.rsqrt(jnp.mean(x * x, axis=-1, keepdims=True) + EPS) * g


def _split_bf16(x):
    hi = x.astype(BF16)
    lo = (x - hi.astype(F32)).astype(BF16)
    return hi, lo


def _log_decay(a_lr, w_a2, b_a):
    pre = _dot(a_lr, w_a2) + b_a
    return (jnp.minimum(pre, 0.0) - jnp.log(1.0 + jnp.exp(-jnp.abs(pre)))) * (1.0 / GATE_TAU)


def _bcast_rows(b, idxs, seg):
    parts = []
    for i in idxs:
        if i < 0:
            parts.append(jnp.zeros((seg, b.shape[1]), b.dtype))
        else:
            parts.append(jnp.broadcast_to(b[i:i + 1, :], (seg, b.shape[1])))
    return parts[0] if len(parts) == 1 else jnp.concatenate(parts, axis=0)


def _inproj_kernel(x_ref, g_ref, w_ref, w_a_ref, o_ref, o_a_ref, xn_ref):
    @pl.when(pl.program_id(1) == 0)
    def _():
        xn = _rms(x_ref[...], g_ref[...]).astype(BF16)
        xn_ref[...] = xn
        o_a_ref[...] = _dot_nt(xn, w_a_ref[...].astype(BF16)).astype(o_a_ref.dtype)

    o_ref[...] = _dot_nt(xn_ref[...], w_ref[...].astype(BF16)).astype(o_ref.dtype)


def _inproj(x2d, g_pre, w_in_t, tm):
    t = x2d.shape[0]
    tn = INPROJ_TN

    def w_rows(i, j):
        row0 = j * tn + jnp.where(j * tn >= OFF_MA, GATE_RANK, 0)
        return (pl.multiple_of(row0, GATE_RANK), 0)

    return pl.pallas_call(
        _inproj_kernel,
        grid=(t // tm, N_MAIN // tn),
        in_specs=[
            pl.BlockSpec((tm, D_MODEL), lambda i, j: (i, 0)),
            pl.BlockSpec((1, D_MODEL), lambda i, j: (0, 0)),
            pl.BlockSpec((pl.Element(tn), pl.Element(D_MODEL)), w_rows),
            pl.BlockSpec((A_PAD, D_MODEL), lambda i, j: (W_IN_A // A_PAD, 0)),
        ],
        out_specs=[
            pl.BlockSpec((tm, tn), lambda i, j: (i, j)),
            pl.BlockSpec((tm, A_PAD), lambda i, j: (i, 0)),
        ],
        out_shape=[
            jax.ShapeDtypeStruct((t, N_MAIN), BF16),
            jax.ShapeDtypeStruct((t, A_PAD), BF16),
        ],
        scratch_shapes=[pltpu.VMEM((tm, D_MODEL), BF16)],
        compiler_params=pltpu.CompilerParams(
            dimension_semantics=("arbitrary", "arbitrary"),
            vmem_limit_bytes=VMEM_LIMIT),
        name="inproj",
    )(x2d, g_pre, w_in_t, w_in_t)


def _gla_levels(c):
    levels = []
    blk = c
    while blk > GLA_SUB:
        levels.append((blk, [j * blk + blk // 2 - 1 for j in range(c // blk)]))
        blk //= 2
    diag = [j * GLA_SUB - 1 for j in range(c // GLA_SUB)]
    return levels, diag


def _gla_masks(c):
    t = lax.broadcasted_iota(jnp.int32, (c, c), 0)
    s = lax.broadcasted_iota(jnp.int32, (c, c), 1)
    levels, _ = _gla_levels(c)
    masks = []
    for blk, _ in levels:
        sh = blk.bit_length() - 1
        half = blk // 2
        same = (t >> sh) == (s >> sh)
        masks.append(same & ((t & (blk - 1)) >= half) & ((s & (blk - 1)) < half))
    sh = GLA_SUB.bit_length() - 1
    masks.append(((t >> sh) == (s >> sh)) & (s <= t))
    return masks


def _gla_chunk(q, k, vb, log_a, s_heads, tril_ones, masks):
    c = q.shape[0]
    hi, lo = _split_bf16(log_a)
    b = _dot(tril_ones, hi) + _dot(tril_ones, lo)
    levels, diag = _gla_levels(c)

    qs, ks = [], []
    for blk, idxs in levels:
        ref = _bcast_rows(b, idxs, blk)
        qs.append((q * jnp.exp(jnp.minimum(b - ref, 0.0))).astype(BF16))
        ks.append((k * jnp.exp(jnp.minimum(ref - b, 0.0))).astype(BF16))
    ref = _bcast_rows(b, diag, GLA_SUB)
    qs.append((q * jnp.exp(b - ref)).astype(BF16))
    ks.append((k * jnp.exp(ref - b)).astype(BF16))

    q_in = (q * jnp.exp(b)).astype(BF16)
    b_last = b[c - 1:c, :]
    k_out = k * jnp.exp(b_last - b)

    outs, new_states = [], []
    for h in range(HB):
        kc = slice(h * DK, (h + 1) * DK)
        vc = slice(h * DV, (h + 1) * DV)
        att = jnp.zeros((c, c), F32)
        for ql, kl, m in zip(qs, ks, masks):
            att = jnp.where(m, _dot_nt(ql[:, kc], kl[:, kc]), att)
        v_h = vb[:, vc]
        s_h = s_heads[h]
        o_h = _dot(att.astype(BF16), v_h) + _dot(q_in[:, kc], s_h.astype(BF16))
        b_t = jnp.transpose(b[:, kc])
        decay = jnp.exp(jnp.broadcast_to(b_t[:, c - 1:c], (DK, DV)))
        k_t = jnp.transpose(k_out[:, kc]).astype(BF16)
        new_states.append(decay * s_h + _dot(k_t, v_h))
        outs.append(o_h)
    return jnp.concatenate(outs, axis=1), new_states


def _gla_sample_kernel(q_ref, k_ref, vb_ref, a_ref, w_a2_ref, b_a_ref, s0_ref,
                       o_ref, s1_ref, *, nb, seq):
    rows = nb * seq
    log_a = _log_decay(a_ref[...], w_a2_ref[...], b_a_ref[...])
    t = lax.broadcasted_iota(jnp.int32, (rows, rows), 0)
    s = lax.broadcasted_iota(jnp.int32, (rows, rows), 1)
    sh = seq.bit_length() - 1
    causal = ((t >> sh) == (s >> sh)) & (s <= t)
    tril_ones = jnp.where(causal, 1.0, 0.0).astype(BF16)
    hi, lo = _split_bf16(log_a)
    b = _dot(tril_ones, hi) + _dot(tril_ones, lo)

    q = q_ref[...].astype(F32) * (DK ** -0.5)
    k = k_ref[...].astype(F32)
    vb = vb_ref[...]
    q_t = q * jnp.exp(b)
    k_t = (k * jnp.exp(-b)).astype(BF16)
    b_last = _bcast_rows(b, [n * seq + seq - 1 for n in range(nb)], seq)
    k_out = k * jnp.exp(b_last - b)
    col = lax.broadcasted_iota(jnp.int32, (DK, rows), 1) >> sh

    outs = []
    for h in range(HB):
        kc = slice(h * DK, (h + 1) * DK)
        vc = slice(h * DV, (h + 1) * DV)
        v_h = vb[:, vc]
        att = jnp.where(causal, _dot_nt(q_t[:, kc].astype(BF16), k_t[:, kc]), 0.0)
        o_intra = _dot(att.astype(BF16), v_h)
        b_tr = jnp.transpose(b[:, kc])
        k_tr = jnp.transpose(k_out[:, kc])
        o_inter = []
        for n in range(nb):
            s_n = s0_ref[n, h]
            r0 = n * seq
            o_inter.append(_dot(q_t[r0:r0 + seq, kc], s_n))
            decay = jnp.exp(jnp.broadcast_to(b_tr[:, r0 + seq - 1:r0 + seq], (DK, DV)))
            k_n = jnp.where(col == n, k_tr, 0.0).astype(BF16)
            s1_ref[n, h] = decay * s_n + _dot(k_n, v_h)
        outs.append(o_intra + jnp.concatenate(o_inter, axis=0))
    o_ref[...] = jnp.concatenate(outs, axis=1)


def _gla_sample(proj, proj_a, w_a2_p, b_a, s0, nb, seq):
    n_batch = s0.shape[0]
    rows = nb * seq
    kern = functools.partial(_gla_sample_kernel, nb=nb, seq=seq)
    return pl.pallas_call(
        kern,
        grid=(n_batch // nb,),
        in_specs=[
            pl.BlockSpec((rows, K_B), lambda i: (i, OFF_Q // K_B)),
            pl.BlockSpec((rows, K_B), lambda i: (i, OFF_K // K_B)),
            pl.BlockSpec((rows, E_B), lambda i: (i, OFF_VB // E_B)),
            pl.BlockSpec((rows, A_PAD), lambda i: (i, 0)),
            pl.BlockSpec((A_PAD, K_B), lambda i: (0, 0)),
            pl.BlockSpec((1, K_B), lambda i: (0, 0)),
            pl.BlockSpec((nb, HB, DK, DV), lambda i: (i, 0, 0, 0)),
        ],
        out_specs=[
            pl.BlockSpec((rows, E_B), lambda i: (i, 0)),
            pl.BlockSpec((nb, HB, DK, DV), lambda i: (i, 0, 0, 0)),
        ],
        out_shape=[
            jax.ShapeDtypeStruct((n_batch * seq, E_B), F32),
            jax.ShapeDtypeStruct(s0.shape, F32),
        ],
        compiler_params=pltpu.CompilerParams(
            dimension_semantics=("arbitrary",),
            vmem_limit_bytes=VMEM_LIMIT),
        name="gla_sample",
    )(proj, proj, proj, proj_a, w_a2_p, b_a, s0)


def _post_kernel(*refs, tm, seq, prompt):
    if prompt:
        (proj_ref, proj_a_ref, x_ref, p_ref, g_v_ref, w_s_ref, b_col_ref, w_a2_ref, b_a_ref, g_o_ref,
         w_pa_ref, w_pb_ref, w_o_ref, g_post_ref, w_pg_ref, w_pe_ref,
         y_ref, s_out_ref, s_ref) = refs
    else:
        (proj_ref, x_ref, p_ref, o_ref, g_v_ref, w_s_ref, b_col_ref, g_o_ref,
         w_pa_ref, w_pb_ref, w_o_ref, g_post_ref, w_pg_ref, w_pe_ref,
         y_ref, vn_ref) = refs

    def cols(off, width):
        return proj_ref[:, off:off + width]

    vn = _rms(cols(OFF_V, E_A).astype(F32), g_v_ref[...])
    if not prompt:
        vn_ref[...] = vn
    vn_b = vn.astype(BF16)
    t = lax.broadcasted_iota(jnp.int32, (CHUNK, CHUNK), 0)
    s = lax.broadcasted_iota(jnp.int32, (CHUNK, CHUNK), 1)
    causal = s <= t
    if seq < CHUNK:
        sh = seq.bit_length() - 1
        causal = causal & ((t >> sh) == (s >> sh))
    w_sp = [jnp.where(causal, w_s_ref[g], 0.0).astype(BF16) for g in range(HA)]
    gate_rows = []
    for c in range(tm // CHUNK):
        r = slice(c * CHUNK, (c + 1) * CHUNK)
        gate_rows.append(jnp.concatenate(
            [_dot(w_sp[g], vn_b[r, g * DA:(g + 1) * DA]) + b_col_ref[:, g:g + 1]
             for g in range(HA)], axis=1))
    gate = gate_rows[0] if len(gate_rows) == 1 else jnp.concatenate(gate_rows, axis=0)
    z = cols(OFF_Z, E_A).astype(F32)
    y_a = cols(OFF_U, E_A).astype(F32) * gate * (z * _sigmoid(z))

    if prompt:
        @pl.when(pl.program_id(1) == 0)
        def _():
            s_ref[...] = jnp.zeros_like(s_ref)

        log_a = _log_decay(proj_a_ref[...], w_a2_ref[...], b_a_ref[...])
        q = cols(OFF_Q, K_B).astype(F32) * (DK ** -0.5)
        k = cols(OFF_K, K_B).astype(F32)
        vb = cols(OFF_VB, E_B)
        tril_ones = jnp.where(s <= t, 1.0, 0.0).astype(BF16)
        masks = _gla_masks(CHUNK)
        states = [s_ref[h] for h in range(HB)]
        o_rows = []
        for c in range(tm // CHUNK):
            r = slice(c * CHUNK, (c + 1) * CHUNK)
            o_c, states = _gla_chunk(q[r], k[r], vb[r], log_a[r], states, tril_ones, masks)
            o_rows.append(o_c)
        for h in range(HB):
            s_ref[h] = states[h]
        o = o_rows[0] if len(o_rows) == 1 else jnp.concatenate(o_rows, axis=0)

        @pl.when(pl.program_id(1) == pl.num_programs(1) - 1)
        def _():
            s_out_ref[0] = s_ref[...]
    else:
        o = o_ref[...]

    g_o = g_o_ref[...]
    on = jnp.concatenate(
        [_rms(o[:, h * DV:(h + 1) * DV], g_o[:, h * DV:(h + 1) * DV]) for h in range(HB)], axis=1)
    zb = cols(OFF_ZB, E_B).astype(F32)
    y_b = on * (zb * _sigmoid(zb))

    m = (_sigmoid(cols(OFF_MA, D_MODEL).astype(F32)) * _dot(y_a.astype(BF16), w_pa_ref[...])
         + _sigmoid(cols(OFF_MB, D_MODEL).astype(F32)) * _dot(y_b.astype(BF16), w_pb_ref[...]))
    h_res = x_ref[...] + _rms(_dot(m.astype(BF16), w_o_ref[...]), g_post_ref[...])
    pg = _sigmoid(_dot(h_res.astype(BF16), w_pg_ref[...]))
    y_ref[...] = h_res + pg * _dot(p_ref[...].astype(BF16), w_pe_ref[...])


def _const_spec(shape):
    nd = len(shape)
    return pl.BlockSpec(shape, lambda *_: (0,) * nd, pipeline_mode=pl.Buffered(1))


def _post_prompt(proj, proj_a, x2d, p2d, wts, n_batch, seq, tm):
    nt = seq // tm
    kern = functools.partial(_post_kernel, tm=tm, seq=seq, prompt=True)
    row = lambda b, i: (b * nt + i, 0)
    names = ("g_v", "w_s", "b_col", "w_a2", "b_a", "g_o", "w_pa", "w_pb", "w_o", "g_post", "w_pg", "w_pe")
    w_args = [wts[n] for n in names]
    return pl.pallas_call(
        kern,
        grid=(n_batch, nt),
        in_specs=[
            pl.BlockSpec((tm, N_MAIN), row),
            pl.BlockSpec((tm, A_PAD), row),
            pl.BlockSpec((tm, D_MODEL), row),
            pl.BlockSpec((tm, PLE_DIM), row),
        ] + [_const_spec(w.shape) for w in w_args],
        out_specs=[
            pl.BlockSpec((tm, D_MODEL), row),
            pl.BlockSpec((1, HB, DK, DV), lambda b, i: (b, 0, 0, 0)),
        ],
        out_shape=[
            jax.ShapeDtypeStruct((n_batch * seq, D_MODEL), F32),
            jax.ShapeDtypeStruct((n_batch, HB, DK, DV), F32),
        ],
        scratch_shapes=[pltpu.VMEM((HB, DK, DV), F32)],
        compiler_params=pltpu.CompilerParams(
            dimension_semantics=("arbitrary", "arbitrary"),
            vmem_limit_bytes=VMEM_LIMIT),
        name="post_prompt",
    )(proj, proj_a, x2d, p2d, *w_args)


def _post_sample(proj, x2d, p2d, o_gla, wts, seq, tm):
    t = x2d.shape[0]
    kern = functools.partial(_post_kernel, tm=tm, seq=seq, prompt=False)
    row = lambda i: (i, 0)
    names = ("g_v", "w_s_bd", "b_col_bd", "g_o", "w_pa", "w_pb", "w_o", "g_post", "w_pg", "w_pe")
    w_args = [wts[n] for n in names]
    return pl.pallas_call(
        kern,
        grid=(t // tm,),
        in_specs=[
            pl.BlockSpec((tm, N_MAIN), row),
            pl.BlockSpec((tm, D_MODEL), row),
            pl.BlockSpec((tm, PLE_DIM), row),
            pl.BlockSpec((tm, E_B), row),
        ] + [_const_spec(w.shape) for w in w_args],
        out_specs=[
            pl.BlockSpec((tm, D_MODEL), row),
            pl.BlockSpec((tm, E_A), row),
        ],
        out_shape=[
            jax.ShapeDtypeStruct((t, D_MODEL), F32),
            jax.ShapeDtypeStruct((t, E_A), F32),
        ],
        compiler_params=pltpu.CompilerParams(
            dimension_semantics=("arbitrary",),
            vmem_limit_bytes=VMEM_LIMIT),
        name="post_sample",
    )(proj, x2d, p2d, o_gla, *w_args)


def _layer_weights(i, dec_seq, g_pre, w_in, g_v, w_s, b_s, w_a2, b_a, g_o,
                   w_pa, w_pb, w_o, g_post, w_pg, w_pe):
    w_a2_p = jnp.concatenate(
        [w_a2[i], jnp.zeros((A_PAD - GATE_RANK, K_B), w_a2.dtype)], axis=0).astype(BF16)
    reps = CHUNK // dec_seq
    return dict(
        g_pre=g_pre[i][None, :],
        w_in_t=w_in[i].T,
        g_v=g_v[i][None, :],
        w_s=w_s[i],
        b_col=b_s[i].T,
        w_s_bd=jnp.tile(w_s[i][:, :dec_seq, :dec_seq], (1, reps, reps)),
        b_col_bd=jnp.tile(b_s[i][:, :dec_seq].T, (reps, 1)),
        w_a2=w_a2_p,
        b_a=b_a[i][None, :],
        g_o=g_o[i].reshape(1, E_B),
        w_pa=w_pa[i].astype(BF16),
        w_pb=w_pb[i].astype(BF16),
        w_o=w_o[i].astype(BF16),
        g_post=g_post[i][None, :],
        w_pg=w_pg[i].astype(BF16),
        w_pe=w_pe[i].astype(BF16),
    )


def kernel(x_prompt, x_sample, p_prompt, p_sample, state_gla, g_pre, w_in, g_v, w_s, b_s,
           w_a2, b_a, g_o, w_pa, w_pb, w_o, g_post, w_pg, w_pe):
    depth = w_in.shape[0]
    n_batch, seq, _ = x_prompt.shape
    dec_batch, dec_seq, _ = x_sample.shape
    hp = x_prompt.reshape(n_batch * seq, D_MODEL)
    hs = x_sample.reshape(dec_batch * dec_seq, D_MODEL)
    sp_list, ss_list, cv_list = [], [], []
    for i in range(depth):
        wts = _layer_weights(i, dec_seq, g_pre, w_in, g_v, w_s, b_s, w_a2, b_a, g_o,
                             w_pa, w_pb, w_o, g_post, w_pg, w_pe)
        proj_p, proj_a_p = _inproj(hp, wts["g_pre"], wts["w_in_t"], tm=1024)
        proj_s, proj_a_s = _inproj(hs, wts["g_pre"], wts["w_in_t"], tm=1024)
        o_s, s_s = _gla_sample(proj_s, proj_a_s, wts["w_a2"], wts["b_a"], state_gla[i],
                               nb=CHUNK // dec_seq, seq=dec_seq)
        hp, s_p = _post_prompt(proj_p, proj_a_p, hp, p_prompt[i].reshape(n_batch * seq, PLE_DIM),
                               wts, n_batch, seq, tm=256)
        hs, v_s = _post_sample(proj_s, hs, p_sample[i].reshape(dec_batch * dec_seq, PLE_DIM),
                               o_s, wts, dec_seq, tm=256)
        sp_list.append(s_p)
        ss_list.append(s_s)
        cv_list.append(v_s.reshape(dec_batch, dec_seq, E_A))
    stack = (lambda xs: xs[0][None]) if depth == 1 else jnp.stack
    return (hp.reshape(n_batch, seq, D_MODEL),
            hs.reshape(dec_batch, dec_seq, D_MODEL),
            stack(sp_list), stack(ss_list), stack(cv_list))
```

```python
import functools

import jax
import jax.numpy as jnp
from jax import lax
from jax.experimental import pallas as pl
from jax.experimental.pallas import tpu as pltpu

D_MODEL = 2048
E_A = D_MODEL // 2
HA = 4
DA = E_A // HA
CHUNK = 128
HB = 4
E_B = D_MODEL // 2
DV = E_B // HB
DK = DV // 2
K_B = HB * DK
GATE_RANK = 16
GATE_TAU = 16.0
GLA_SUB = 16
PLE_DIM = 256
EPS = 1e-6

LANES = 128
OFF_U = 0
OFF_V = OFF_U + E_A
OFF_Z = OFF_V + E_A
OFF_Q = OFF_Z + E_A
OFF_K = OFF_Q + K_B
OFF_VB = OFF_K + K_B
OFF_ZB = OFF_VB + E_B
OFF_MA = OFF_ZB + E_B
OFF_MB = OFF_MA + D_MODEL
N_MAIN = OFF_MB + D_MODEL
W_IN_A = OFF_MA
A_PAD = LANES
INPROJ_TN = 1024

F32 = jnp.float32
BF16 = jnp.bfloat16

VMEM_LIMIT = 56 * 1024 * 1024


def _dot(a, b):
    return jnp.dot(a, b, preferred_element_type=F32)


def _dot_nt(a, b):
    return lax.dot_general(a, b, (((1,), (1,)), ((), ())), preferred_element_type=F32)


def _sigmoid(x):
    return 1.0 / (1.0 + jnp.exp(-x))


def _rms(x, g):
    return x * lax.rsqrt(jnp.mean(x * x, axis=-1, keepdims=True) + EPS) * g


def _split_bf16(x):
    hi = x.astype(BF16)
    lo = (x - hi.astype(F32)).astype(BF16)
    return hi, lo


def _log_decay(a_lr, w_a2, b_a):
    pre = _dot(a_lr, w_a2) + b_a
    return (jnp.minimum(pre, 0.0) - jnp.log(1.0 + jnp.exp(-jnp.abs(pre)))) * (1.0 / GATE_TAU)


def _bcast_rows(b, idxs, seg):
    parts = []
    for i in idxs:
        if i < 0:
            parts.append(jnp.zeros((seg, b.shape[1]), b.dtype))
        else:
            parts.append(jnp.broadcast_to(b[i:i + 1, :], (seg, b.shape[1])))
    return parts[0] if len(parts) == 1 else jnp.concatenate(parts, axis=0)


def _inproj_kernel(x_ref, g_ref, w_ref, w_a_ref, o_ref, o_a_ref, xn_ref):
    @pl.when(pl.program_id(1) == 0)
    def _():
        xn = _rms(x_ref[...], g_ref[...]).astype(BF16)
        xn_ref[...] = xn
        o_a_ref[...] = _dot_nt(xn, w_a_ref[...].astype(BF16)).astype(o_a_ref.dtype)

    o_ref[...] = _dot_nt(xn_ref[...], w_ref[...].astype(BF16)).astype(o_ref.dtype)


def _inproj(x2d, g_pre, w_in_t, tm):
    t = x2d.shape[0]
    tn = INPROJ_TN

    def w_rows(i, j):
        row0 = j * tn + jnp.where(j * tn >= OFF_MA, GATE_RANK, 0)
        return (pl.multiple_of(row0, GATE_RANK), 0)

    return pl.pallas_call(
        _inproj_kernel,
        grid=(t // tm, N_MAIN // tn),
        in_specs=[
            pl.BlockSpec((tm, D_MODEL), lambda i, j: (i, 0)),
            pl.BlockSpec((1, D_MODEL), lambda i, j: (0, 0)),
            pl.BlockSpec((pl.Element(tn), pl.Element(D_MODEL)), w_rows),
            pl.BlockSpec((A_PAD, D_MODEL), lambda i, j: (W_IN_A // A_PAD, 0)),
        ],
        out_specs=[
            pl.BlockSpec((tm, tn), lambda i, j: (i, j)),
            pl.BlockSpec((tm, A_PAD), lambda i, j: (i, 0)),
        ],
        out_shape=[
            jax.ShapeDtypeStruct((t, N_MAIN), BF16),
            jax.ShapeDtypeStruct((t, A_PAD), BF16),
        ],
        scratch_shapes=[pltpu.VMEM((tm, D_MODEL), BF16)],
        compiler_params=pltpu.CompilerParams(
            dimension_semantics=("arbitrary", "arbitrary"),
            vmem_limit_bytes=VMEM_LIMIT),
        name="inproj",
    )(x2d, g_pre, w_in_t, w_in_t)


def _gla_levels(c):
    levels = []
    blk = c
    while blk > GLA_SUB:
        levels.append((blk, [j * blk + blk // 2 - 1 for j in range(c // blk)]))
        blk //= 2
    diag = [j * GLA_SUB - 1 for j in range(c // GLA_SUB)]
    return levels, diag


def _gla_masks(c):
    t = lax.broadcasted_iota(jnp.int32, (c, c), 0)
    s = lax.broadcasted_iota(jnp.int32, (c, c), 1)
    levels, _ = _gla_levels(c)
    masks = []
    for blk, _ in levels:
        sh = blk.bit_length() - 1
        half = blk // 2
        same = (t >> sh) == (s >> sh)
        masks.append(same & ((t & (blk - 1)) >= half) & ((s & (blk - 1)) < half))
    sh = GLA_SUB.bit_length() - 1
    masks.append(((t >> sh) == (s >> sh)) & (s <= t))
    return masks


def _run_interleaved(*gens):
    results = [None] * len(gens)
    live = list(range(len(gens)))
    while live:
        for i in list(live):
            try:
                next(gens[i])
            except StopIteration as stop:
                results[i] = stop.value
                live.remove(i)
    return results


def _gla_chunk(q, k, vb, log_a, s_heads, tril_ones, masks):
    c = q.shape[0]
    hi, lo = _split_bf16(log_a)
    b = _dot(tril_ones, hi) + _dot(tril_ones, lo)
    levels, diag = _gla_levels(c)

    qs, ks = [], []
    for blk, idxs in levels:
        ref = _bcast_rows(b, idxs, blk)
        qs.append((q * jnp.exp(jnp.minimum(b - ref, 0.0))).astype(BF16))
        ks.append((k * jnp.exp(jnp.minimum(ref - b, 0.0))).astype(BF16))
    ref = _bcast_rows(b, diag, GLA_SUB)
    qs.append((q * jnp.exp(b - ref)).astype(BF16))
    ks.append((k * jnp.exp(ref - b)).astype(BF16))

    q_in = (q * jnp.exp(b)).astype(BF16)
    b_last = b[c - 1:c, :]
    k_out = k * jnp.exp(b_last - b)
    yield

    outs, new_states = [], []
    for h in range(HB):
        kc = slice(h * DK, (h + 1) * DK)
        vc = slice(h * DV, (h + 1) * DV)
        att = jnp.zeros((c, c), F32)
        for ql, kl, m in zip(qs, ks, masks):
            att = jnp.where(m, _dot_nt(ql[:, kc], kl[:, kc]), att)
        v_h = vb[:, vc]
        s_h = s_heads[h]
        o_h = _dot(att.astype(BF16), v_h) + _dot(q_in[:, kc], s_h.astype(BF16))
        b_t = jnp.transpose(b[:, kc])
        decay = jnp.exp(jnp.broadcast_to(b_t[:, c - 1:c], (DK, DV)))
        k_t = jnp.transpose(k_out[:, kc]).astype(BF16)
        new_states.append(decay * s_h + _dot(k_t, v_h))
        outs.append(o_h)
        yield
    return jnp.concatenate(outs, axis=1), new_states


def _gla_sample_kernel(q_ref, k_ref, vb_ref, a_ref, w_a2_ref, b_a_ref, s0_ref,
                       o_ref, s1_ref, *, nb, seq):
    rows = nb * seq
    log_a = _log_decay(a_ref[...], w_a2_ref[...], b_a_ref[...])
    t = lax.broadcasted_iota(jnp.int32, (rows, rows), 0)
    s = lax.broadcasted_iota(jnp.int32, (rows, rows), 1)
    sh = seq.bit_length() - 1
    causal = ((t >> sh) == (s >> sh)) & (s <= t)
    tril_ones = jnp.where(causal, 1.0, 0.0).astype(BF16)
    hi, lo = _split_bf16(log_a)
    b = _dot(tril_ones, hi) + _dot(tril_ones, lo)

    q = q_ref[...].astype(F32) * (DK ** -0.5)
    k = k_ref[...].astype(F32)
    vb = vb_ref[...]
    q_t = q * jnp.exp(b)
    k_t = (k * jnp.exp(-b)).astype(BF16)
    b_last = _bcast_rows(b, [n * seq + seq - 1 for n in range(nb)], seq)
    k_out = k * jnp.exp(b_last - b)
    col = lax.broadcasted_iota(jnp.int32, (DK, rows), 1) >> sh

    outs = []
    for h in range(HB):
        kc = slice(h * DK, (h + 1) * DK)
        vc = slice(h * DV, (h + 1) * DV)
        v_h = vb[:, vc]
        att = jnp.where(causal, _dot_nt(q_t[:, kc].astype(BF16), k_t[:, kc]), 0.0)
        o_intra = _dot(att.astype(BF16), v_h)
        b_tr = jnp.transpose(b[:, kc])
        k_tr = jnp.transpose(k_out[:, kc])
        o_inter = []
        for n in range(nb):
            s_n = s0_ref[n, h]
            r0 = n * seq
            o_inter.append(_dot(q_t[r0:r0 + seq, kc], s_n))
            decay = jnp.exp(jnp.broadcast_to(b_tr[:, r0 + seq - 1:r0 + seq], (DK, DV)))
            k_n = jnp.where(col == n, k_tr, 0.0).astype(BF16)
            s1_ref[n, h] = decay * s_n + _dot(k_n, v_h)
        outs.append(o_intra + jnp.concatenate(o_inter, axis=0))
    o_ref[...] = jnp.concatenate(outs, axis=1)


def _gla_sample(proj, proj_a, w_a2_p, b_a, s0, nb, seq):
    n_batch = s0.shape[0]
    rows = nb * seq
    kern = functools.partial(_gla_sample_kernel, nb=nb, seq=seq)
    return pl.pallas_call(
        kern,
        grid=(n_batch // nb,),
        in_specs=[
            pl.BlockSpec((rows, K_B), lambda i: (i, OFF_Q // K_B)),
            pl.BlockSpec((rows, K_B), lambda i: (i, OFF_K // K_B)),
            pl.BlockSpec((rows, E_B), lambda i: (i, OFF_VB // E_B)),
            pl.BlockSpec((rows, A_PAD), lambda i: (i, 0)),
            pl.BlockSpec((A_PAD, K_B), lambda i: (0, 0)),
            pl.BlockSpec((1, K_B), lambda i: (0, 0)),
            pl.BlockSpec((nb, HB, DK, DV), lambda i: (i, 0, 0, 0)),
        ],
        out_specs=[
            pl.BlockSpec((rows, E_B), lambda i: (i, 0)),
            pl.BlockSpec((nb, HB, DK, DV), lambda i: (i, 0, 0, 0)),
        ],
        out_shape=[
            jax.ShapeDtypeStruct((n_batch * seq, E_B), F32),
            jax.ShapeDtypeStruct(s0.shape, F32),
        ],
        compiler_params=pltpu.CompilerParams(
            dimension_semantics=("arbitrary",),
            vmem_limit_bytes=VMEM_LIMIT),
        name="gla_sample",
    )(proj, proj, proj, proj_a, w_a2_p, b_a, s0)


def _branch_a(proj_ref, g_v_ref, w_s_ref, b_col_ref, tm, seq):
    vn = _rms(proj_ref[:, OFF_V:OFF_V + E_A].astype(F32), g_v_ref[...])
    vn_b = vn.astype(BF16)
    t = lax.broadcasted_iota(jnp.int32, (CHUNK, CHUNK), 0)
    s = lax.broadcasted_iota(jnp.int32, (CHUNK, CHUNK), 1)
    causal = s <= t
    if seq < CHUNK:
        sh = seq.bit_length() - 1
        causal = causal & ((t >> sh) == (s >> sh))
    w_sp = [jnp.where(causal, w_s_ref[g], 0.0).astype(BF16) for g in range(HA)]
    yield
    y_rows = []
    for c in range(tm // CHUNK):
        r = slice(c * CHUNK, (c + 1) * CHUNK)
        gate = jnp.concatenate(
            [_dot(w_sp[g], vn_b[r, g * DA:(g + 1) * DA]) + b_col_ref[:, g:g + 1]
             for g in range(HA)], axis=1)
        z = proj_ref[r, OFF_Z:OFF_Z + E_A].astype(F32)
        y_rows.append((proj_ref[r, OFF_U:OFF_U + E_A].astype(F32) * gate
                       * (z * _sigmoid(z))).astype(BF16))
        yield
    y_a = y_rows[0] if len(y_rows) == 1 else jnp.concatenate(y_rows, axis=0)
    return y_a, vn


def _prompt_gla(proj_ref, proj_a_ref, w_a2_ref, b_a_ref, g_o_ref, states, tm):
    log_a = _log_decay(proj_a_ref[...], w_a2_ref[...], b_a_ref[...])
    t = lax.broadcasted_iota(jnp.int32, (CHUNK, CHUNK), 0)
    s = lax.broadcasted_iota(jnp.int32, (CHUNK, CHUNK), 1)
    tril_ones = jnp.where(s <= t, 1.0, 0.0).astype(BF16)
    masks = _gla_masks(CHUNK)
    yield
    y_rows = []
    for c in range(tm // CHUNK):
        r = slice(c * CHUNK, (c + 1) * CHUNK)
        q = proj_ref[r, OFF_Q:OFF_Q + K_B].astype(F32) * (DK ** -0.5)
        k = proj_ref[r, OFF_K:OFF_K + K_B].astype(F32)
        vb = proj_ref[r, OFF_VB:OFF_VB + E_B]
        o_c, states = yield from _gla_chunk(q, k, vb, log_a[r], states, tril_ones, masks)
        y_rows.append(_branch_b_out(o_c, proj_ref[r, OFF_ZB:OFF_ZB + E_B], g_o_ref).astype(BF16))
        yield
    y_b = y_rows[0] if len(y_rows) == 1 else jnp.concatenate(y_rows, axis=0)
    return y_b, states


def _branch_b_out(o, zb, g_o_ref):
    g_o = g_o_ref[...]
    on = jnp.concatenate(
        [_rms(o[:, h * DV:(h + 1) * DV], g_o[:, h * DV:(h + 1) * DV]) for h in range(HB)], axis=1)
    zb = zb.astype(F32)
    return on * (zb * _sigmoid(zb))


BACK_BLOCKS = 4


def _back(y_a, y_b, m_a_ref, m_b_ref, x_ref, p_ref,
          w_pa_ref, w_pb_ref, w_o_ref, g_post_ref, w_pg_ref, w_pe_ref, col0=(0, 0)):
    bw = D_MODEL // BACK_BLOCKS
    blocks = [slice(j * bw, (j + 1) * bw) for j in range(BACK_BLOCKS)]
    m_parts = []
    for c in blocks:
        pa = _dot(y_a, w_pa_ref[:, c])
        yield
        pb = _dot(y_b, w_pb_ref[:, c])
        m_a = m_a_ref[:, col0[0] + c.start:col0[0] + c.stop].astype(F32)
        m_b = m_b_ref[:, col0[1] + c.start:col0[1] + c.stop].astype(F32)
        m_parts.append((_sigmoid(m_a) * pa + _sigmoid(m_b) * pb).astype(BF16))
        yield
    m = jnp.concatenate(m_parts, axis=1)
    mo_parts = []
    for c in blocks:
        mo_parts.append(_dot(m, w_o_ref[:, c]))
        yield
    h_res = x_ref[...] + _rms(jnp.concatenate(mo_parts, axis=1), g_post_ref[...])
    h_b = h_res.astype(BF16)
    p_b = p_ref[...].astype(BF16)
    yield
    y_parts = []
    for c in blocks:
        pg = _sigmoid(_dot(h_b, w_pg_ref[:, c]))
        y_parts.append(h_res[:, c] + pg * _dot(p_b, w_pe_ref[:, c]))
        yield
    return jnp.concatenate(y_parts, axis=1)


def _post_prompt_kernel(proj_ref, proj_a_ref, m_a_ref, m_b_ref, x_ref, p_ref,
                        g_v_ref, w_s_ref, b_col_ref, w_a2_ref, b_a_ref, g_o_ref,
                        w_pa_ref, w_pb_ref, w_o_ref, g_post_ref, w_pg_ref, w_pe_ref,
                        y_ref, s_out_ref, s_ref, ya_ref, yb_ref, *, tm, seq, n_tiles):
    step = pl.program_id(0)
    nt = seq // tm

    def front():
        keep = lax.rem(step, nt) != 0
        states = [jnp.where(keep, s_ref[h], 0.0) for h in range(HB)]
        y_a, _ = yield from _branch_a(proj_ref, g_v_ref, w_s_ref, b_col_ref, tm, seq)
        y_b, states = yield from _prompt_gla(proj_ref, proj_a_ref, w_a2_ref, b_a_ref, g_o_ref,
                                             states, tm)
        return y_a, y_b, states

    def back():
        return _back(ya_ref[...], yb_ref[...], m_a_ref, m_b_ref, x_ref, p_ref,
                     w_pa_ref, w_pb_ref, w_o_ref, g_post_ref, w_pg_ref, w_pe_ref)

    def store_front(y_a, y_b, states):
        ya_ref[...] = y_a
        yb_ref[...] = y_b
        for h in range(HB):
            s_ref[h] = states[h]
            s_out_ref[0, h] = states[h]

    @pl.when(step == 0)
    def _():
        store_front(*_run_interleaved(front())[0])

    @pl.when((step > 0) & (step < n_tiles))
    def _():
        y, new = _run_interleaved(back(), front())
        y_ref[...] = y
        store_front(*new)

    @pl.when(step == n_tiles)
    def _():
        y_ref[...] = _run_interleaved(back())[0]


def _post_sample_kernel(proj_ref, x_ref, p_ref, o_ref, g_v_ref, w_s_ref, b_col_ref, g_o_ref,
                        w_pa_ref, w_pb_ref, w_o_ref, g_post_ref, w_pg_ref, w_pe_ref,
                        y_ref, vn_ref, *, tm, seq):
    (y_a, vn), = _run_interleaved(_branch_a(proj_ref, g_v_ref, w_s_ref, b_col_ref, tm, seq))
    vn_ref[...] = vn
    y_b = _branch_b_out(o_ref[...], proj_ref[:, OFF_ZB:OFF_ZB + E_B], g_o_ref).astype(BF16)
    y_ref[...], = _run_interleaved(
        _back(y_a, y_b, proj_ref, proj_ref, x_ref, p_ref,
              w_pa_ref, w_pb_ref, w_o_ref, g_post_ref, w_pg_ref, w_pe_ref, col0=(OFF_MA, OFF_MB)))


def _const_spec(shape):
    nd = len(shape)
    return pl.BlockSpec(shape, lambda *_: (0,) * nd, pipeline_mode=pl.Buffered(1))


def _post_prompt(proj, proj_a, x2d, p2d, wts, n_batch, seq, tm):
    nt = seq // tm
    n_tiles = n_batch * nt
    kern = functools.partial(_post_prompt_kernel, tm=tm, seq=seq, n_tiles=n_tiles)
    cur = lambda s: (jnp.minimum(s, n_tiles - 1), 0)
    prev = lambda s: (jnp.maximum(s - 1, 0), 0)
    names = ("g_v", "w_s", "b_col", "w_a2", "b_a", "g_o", "w_pa", "w_pb", "w_o", "g_post", "w_pg", "w_pe")
    w_args = [wts[n] for n in names]
    return pl.pallas_call(
        kern,
        grid=(n_tiles + 1,),
        in_specs=[
            pl.BlockSpec((tm, OFF_MA), cur),
            pl.BlockSpec((tm, A_PAD), cur),
            pl.BlockSpec((tm, D_MODEL), lambda s: (jnp.maximum(s - 1, 0), OFF_MA // D_MODEL)),
            pl.BlockSpec((tm, D_MODEL), lambda s: (jnp.maximum(s - 1, 0), OFF_MB // D_MODEL)),
            pl.BlockSpec((tm, D_MODEL), prev),
            pl.BlockSpec((tm, PLE_DIM), prev),
        ] + [_const_spec(w.shape) for w in w_args],
        out_specs=[
            pl.BlockSpec((tm, D_MODEL), prev),
            pl.BlockSpec((1, HB, DK, DV), lambda s: (jnp.minimum(s, n_tiles - 1) // nt, 0, 0, 0)),
        ],
        out_shape=[
            jax.ShapeDtypeStruct((n_batch * seq, D_MODEL), F32),
            jax.ShapeDtypeStruct((n_batch, HB, DK, DV), F32),
        ],
        scratch_shapes=[
            pltpu.VMEM((HB, DK, DV), F32),
            pltpu.VMEM((tm, E_A), BF16),
            pltpu.VMEM((tm, E_B), BF16),
        ],
        compiler_params=pltpu.CompilerParams(
            dimension_semantics=("arbitrary",),
            vmem_limit_bytes=VMEM_LIMIT),
        name="post_prompt",
    )(proj, proj_a, proj, proj, x2d, p2d, *w_args)


def _post_sample(proj, x2d, p2d, o_gla, wts, seq, tm):
    t = x2d.shape[0]
    kern = functools.partial(_post_sample_kernel, tm=tm, seq=seq)
    row = lambda i: (i, 0)
    names = ("g_v", "w_s_bd", "b_col_bd", "g_o", "w_pa", "w_pb", "w_o", "g_post", "w_pg", "w_pe")
    w_args = [wts[n] for n in names]
    return pl.pallas_call(
        kern,
        grid=(t // tm,),
        in_specs=[
            pl.BlockSpec((tm, N_MAIN), row),
            pl.BlockSpec((tm, D_MODEL), row),
            pl.BlockSpec((tm, PLE_DIM), row),
            pl.BlockSpec((tm, E_B), row),
        ] + [_const_spec(w.shape) for w in w_args],
        out_specs=[
            pl.BlockSpec((tm, D_MODEL), row),
            pl.BlockSpec((tm, E_A), row),
        ],
        out_shape=[
            jax.ShapeDtypeStruct((t, D_MODEL), F32),
            jax.ShapeDtypeStruct((t, E_A), F32),
        ],
        compiler_params=pltpu.CompilerParams(
            dimension_semantics=("arbitrary",),
            vmem_limit_bytes=VMEM_LIMIT),
        name="post_sample",
    )(proj, x2d, p2d, o_gla, *w_args)


def _layer_weights(i, dec_seq, g_pre, w_in, g_v, w_s, b_s, w_a2, b_a, g_o,
                   w_pa, w_pb, w_o, g_post, w_pg, w_pe):
    w_a2_p = jnp.concatenate(
        [w_a2[i], jnp.zeros((A_PAD - GATE_RANK, K_B), w_a2.dtype)], axis=0).astype(BF16)
    reps = CHUNK // dec_seq
    return dict(
        g_pre=g_pre[i][None, :],
        w_in_t=w_in[i].T,
        g_v=g_v[i][None, :],
        w_s=w_s[i],
        b_col=b_s[i].T,
        w_s_bd=jnp.tile(w_s[i][:, :dec_seq, :dec_seq], (1, reps, reps)),
        b_col_bd=jnp.tile(b_s[i][:, :dec_seq].T, (reps, 1)),
        w_a2=w_a2_p,
        b_a=b_a[i][None, :],
        g_o=g_o[i].reshape(1, E_B),
        w_pa=w_pa[i].astype(BF16),
        w_pb=w_pb[i].astype(BF16),
        w_o=w_o[i].astype(BF16),
        g_post=g_post[i][None, :],
        w_pg=w_pg[i].astype(BF16),
        w_pe=w_pe[i].astype(BF16),
    )


def kernel(x_prompt, x_sample, p_prompt, p_sample, state_gla, g_pre, w_in, g_v, w_s, b_s,
           w_a2, b_a, g_o, w_pa, w_pb, w_o, g_post, w_pg, w_pe):
    depth = w_in.shape[0]
    n_batch, seq, _ = x_prompt.shape
    dec_batch, dec_seq, _ = x_sample.shape
    hp = x_prompt.reshape(n_batch * seq, D_MODEL)
    hs = x_sample.reshape(dec_batch * dec_seq, D_MODEL)
    sp_list, ss_list, cv_list = [], [], []
    for i in range(depth):
        wts = _layer_weights(i, dec_seq, g_pre, w_in, g_v, w_s, b_s, w_a2, b_a, g_o,
                             w_pa, w_pb, w_o, g_post, w_pg, w_pe)
        proj_p, proj_a_p = _inproj(hp, wts["g_pre"], wts["w_in_t"], tm=1024)
        proj_s, proj_a_s = _inproj(hs, wts["g_pre"], wts["w_in_t"], tm=1024)
        o_s, s_s = _gla_sample(proj_s, proj_a_s, wts["w_a2"], wts["b_a"], state_gla[i],
                               nb=CHUNK // dec_seq, seq=dec_seq)
        hp, s_p = _post_prompt(proj_p, proj_a_p, hp, p_prompt[i].reshape(n_batch * seq, PLE_DIM),
                               wts, n_batch, seq, tm=256)
        hs, v_s = _post_sample(proj_s, hs, p_sample[i].reshape(dec_batch * dec_seq, PLE_DIM),
                               o_s, wts, dec_seq, tm=256)
        sp_list.append(s_p)
        ss_list.append(s_s)
        cv_list.append(v_s.reshape(dec_batch, dec_seq, E_A))
    stack = (lambda xs: xs[0][None]) if depth == 1 else jnp.stack
    return (hp.reshape(n_batch, seq, D_MODEL),
            hs.reshape(dec_batch, dec_seq, D_MODEL),
            stack(sp_list), stack(ss_list), stack(cv_list))
```

```python
import functools

import jax
import jax.numpy as jnp
from jax import lax
from jax.experimental import pallas as pl
from jax.experimental.pallas import tpu as pltpu

D_MODEL = 2048
E_A = D_MODEL // 2
HA = 4
DA = E_A // HA
CHUNK = 128
HB = 4
E_B = D_MODEL // 2
DV = E_B // HB
DK = DV // 2
K_B = HB * DK
GATE_RANK = 16
GATE_TAU = 16.0
GLA_SUB = 16
PLE_DIM = 256
EPS = 1e-6

LANES = 128
OFF_U = 0
OFF_V = OFF_U + E_A
OFF_Z = OFF_V + E_A
OFF_Q = OFF_Z + E_A
OFF_K = OFF_Q + K_B
OFF_VB = OFF_K + K_B
OFF_ZB = OFF_VB + E_B
OFF_MA = OFF_ZB + E_B
OFF_MB = OFF_MA + D_MODEL
N_MAIN = OFF_MB + D_MODEL
W_IN_A = OFF_MA
A_PAD = LANES
INPROJ_TN = 1024

F32 = jnp.float32
BF16 = jnp.bfloat16

VMEM_LIMIT = 56 * 1024 * 1024


def _dot(a, b):
    return jnp.dot(a, b, preferred_element_type=F32)


def _dot_nt(a, b):
    return lax.dot_general(a, b, (((1,), (1,)), ((), ())), preferred_element_type=F32)


def _sigmoid(x):
    return 1.0 / (1.0 + jnp.exp(-x))


def _rms(x, g):
    return x * lax.rsqrt(jnp.mean(x * x, axis=-1, keepdims=True) + EPS) * g


def _split_bf16(x):
    hi = x.astype(BF16)
    lo = (x - hi.astype(F32)).astype(BF16)
    return hi, lo


def _log_decay(a_lr, w_a2, b_a):
    pre = _dot(a_lr, w_a2) + b_a
    return (jnp.minimum(pre, 0.0) - jnp.log(1.0 + jnp.exp(-jnp.abs(pre)))) * (1.0 / GATE_TAU)


def _bcast_rows(b, idxs, seg):
    parts = []
    for i in idxs:
        if i < 0:
            parts.append(jnp.zeros((seg, b.shape[1]), b.dtype))
        else:
            parts.append(jnp.broadcast_to(b[i:i + 1, :], (seg, b.shape[1])))
    return parts[0] if len(parts) == 1 else jnp.concatenate(parts, axis=0)


BF16_SUBLANES = 16


def _inproj_kernel(*refs, n_cast):
    x_ref, g_ref, w_ref, w_a_ref = refs[:4]
    cast_in = refs[4:4 + n_cast]
    o_ref, o_a_ref = refs[4 + n_cast:6 + n_cast]
    cast_out = refs[6 + n_cast:6 + 2 * n_cast]
    xn_ref = refs[6 + 2 * n_cast]

    @pl.when(pl.program_id(1) == 0)
    def _():
        xn = _rms(x_ref[...], g_ref[...]).astype(BF16)
        xn_ref[...] = xn
        o_a_ref[...] = _dot_nt(xn, w_a_ref[...].astype(BF16)).astype(o_a_ref.dtype)

    for src, dst in zip(cast_in, cast_out):
        dst[...] = src[...].astype(dst.dtype)
    o_ref[...] = _dot_nt(xn_ref[...], w_ref[...].astype(BF16)).astype(o_ref.dtype)


def _inproj(x2d, g_pre, w_in_t, tm, cast=()):
    t = x2d.shape[0]
    tn = INPROJ_TN
    n_i, n_j = t // tm, N_MAIN // tn

    def w_rows(i, j):
        row0 = j * tn + jnp.where(j * tn >= OFF_MA, GATE_RANK, 0)
        return (pl.multiple_of(row0, GATE_RANK), 0)

    cast_specs = []
    for w in cast:
        per_i = w.shape[0] // n_i
        assert per_i * n_i == w.shape[0] and per_i % BF16_SUBLANES == 0
        n_sub = max(n for n in range(1, n_j + 1)
                    if per_i % n == 0 and (per_i // n) % BF16_SUBLANES == 0)
        cast_specs.append(pl.BlockSpec(
            (per_i // n_sub, w.shape[1]),
            lambda i, j, n_sub=n_sub: (i * n_sub + jnp.minimum(j, n_sub - 1), 0)))

    kern = functools.partial(_inproj_kernel, n_cast=len(cast))
    return pl.pallas_call(
        kern,
        grid=(n_i, n_j),
        in_specs=[
            pl.BlockSpec((tm, D_MODEL), lambda i, j: (i, 0)),
            pl.BlockSpec((1, D_MODEL), lambda i, j: (0, 0)),
            pl.BlockSpec((pl.Element(tn), pl.Element(D_MODEL)), w_rows),
            pl.BlockSpec((A_PAD, D_MODEL), lambda i, j: (W_IN_A // A_PAD, 0)),
        ] + cast_specs,
        out_specs=[
            pl.BlockSpec((tm, tn), lambda i, j: (i, j)),
            pl.BlockSpec((tm, A_PAD), lambda i, j: (i, 0)),
        ] + cast_specs,
        out_shape=[
            jax.ShapeDtypeStruct((t, N_MAIN), BF16),
            jax.ShapeDtypeStruct((t, A_PAD), BF16),
        ] + [jax.ShapeDtypeStruct(w.shape, BF16) for w in cast],
        scratch_shapes=[pltpu.VMEM((tm, D_MODEL), BF16)],
        compiler_params=pltpu.CompilerParams(
            dimension_semantics=("arbitrary", "arbitrary"),
            vmem_limit_bytes=VMEM_LIMIT),
        name="inproj",
    )(x2d, g_pre, w_in_t, w_in_t, *cast)


def _gla_levels(c):
    levels = []
    blk = c
    while blk > GLA_SUB:
        levels.append((blk, [j * blk + blk // 2 - 1 for j in range(c // blk)]))
        blk //= 2
    diag = [j * GLA_SUB - 1 for j in range(c // GLA_SUB)]
    return levels, diag


def _gla_masks(c):
    t = lax.broadcasted_iota(jnp.int32, (c, c), 0)
    s = lax.broadcasted_iota(jnp.int32, (c, c), 1)
    levels, _ = _gla_levels(c)
    masks = []
    for blk, _ in levels:
        sh = blk.bit_length() - 1
        half = blk // 2
        same = (t >> sh) == (s >> sh)
        masks.append(same & ((t & (blk - 1)) >= half) & ((s & (blk - 1)) < half))
    sh = GLA_SUB.bit_length() - 1
    masks.append(((t >> sh) == (s >> sh)) & (s <= t))
    return masks


def _run_interleaved(*gens):
    results = [None] * len(gens)
    live = list(range(len(gens)))
    while live:
        for i in list(live):
            try:
                next(gens[i])
            except StopIteration as stop:
                results[i] = stop.value
                live.remove(i)
    return results


def _gla_chunk(q, k, vb, log_a, s_heads, tril_ones, masks):
    c = q.shape[0]
    hi, lo = _split_bf16(log_a)
    b = _dot(tril_ones, hi) + _dot(tril_ones, lo)
    levels, diag = _gla_levels(c)

    qs, ks = [], []
    for blk, idxs in levels:
        ref = _bcast_rows(b, idxs, blk)
        qs.append((q * jnp.exp(jnp.minimum(b - ref, 0.0))).astype(BF16))
        ks.append((k * jnp.exp(jnp.minimum(ref - b, 0.0))).astype(BF16))
    ref = _bcast_rows(b, diag, GLA_SUB)
    qs.append((q * jnp.exp(b - ref)).astype(BF16))
    ks.append((k * jnp.exp(ref - b)).astype(BF16))

    q_in = (q * jnp.exp(b)).astype(BF16)
    b_last = b[c - 1:c, :]
    k_out = k * jnp.exp(b_last - b)
    yield

    outs, new_states = [], []
    for h in range(HB):
        kc = slice(h * DK, (h + 1) * DK)
        vc = slice(h * DV, (h + 1) * DV)
        att = jnp.zeros((c, c), F32)
        for ql, kl, m in zip(qs, ks, masks):
            att = jnp.where(m, _dot_nt(ql[:, kc], kl[:, kc]), att)
        v_h = vb[:, vc]
        s_h = s_heads[h]
        o_h = _dot(att.astype(BF16), v_h) + _dot(q_in[:, kc], s_h.astype(BF16))
        b_t = jnp.transpose(b[:, kc])
        decay = jnp.exp(jnp.broadcast_to(b_t[:, c - 1:c], (DK, DV)))
        k_t = jnp.transpose(k_out[:, kc]).astype(BF16)
        new_states.append(decay * s_h + _dot(k_t, v_h))
        outs.append(o_h)
        yield
    return jnp.concatenate(outs, axis=1), new_states


def _gla_sample_kernel(q_ref, k_ref, vb_ref, a_ref, w_a2_ref, b_a_ref, s0_ref,
                       o_ref, s1_ref, *, nb, seq):
    rows = nb * seq
    log_a = _log_decay(a_ref[...], w_a2_ref[...], b_a_ref[...])
    t = lax.broadcasted_iota(jnp.int32, (rows, rows), 0)
    s = lax.broadcasted_iota(jnp.int32, (rows, rows), 1)
    sh = seq.bit_length() - 1
    causal = ((t >> sh) == (s >> sh)) & (s <= t)
    tril_ones = jnp.where(causal, 1.0, 0.0).astype(BF16)
    hi, lo = _split_bf16(log_a)
    b = _dot(tril_ones, hi) + _dot(tril_ones, lo)

    q = q_ref[...].astype(F32) * (DK ** -0.5)
    k = k_ref[...].astype(F32)
    vb = vb_ref[...]
    q_t = q * jnp.exp(b)
    k_t = (k * jnp.exp(-b)).astype(BF16)
    b_last = _bcast_rows(b, [n * seq + seq - 1 for n in range(nb)], seq)
    k_out = k * jnp.exp(b_last - b)
    col = lax.broadcasted_iota(jnp.int32, (DK, rows), 1) >> sh

    outs = []
    for h in range(HB):
        kc = slice(h * DK, (h + 1) * DK)
        vc = slice(h * DV, (h + 1) * DV)
        v_h = vb[:, vc]
        att = jnp.where(causal, _dot_nt(q_t[:, kc].astype(BF16), k_t[:, kc]), 0.0)
        o_intra = _dot(att.astype(BF16), v_h)
        b_tr = jnp.transpose(b[:, kc])
        k_tr = jnp.transpose(k_out[:, kc])
        o_inter = []
        for n in range(nb):
            s_n = s0_ref[n, h]
            r0 = n * seq
            o_inter.append(_dot(q_t[r0:r0 + seq, kc], s_n))
            decay = jnp.exp(jnp.broadcast_to(b_tr[:, r0 + seq - 1:r0 + seq], (DK, DV)))
            k_n = jnp.where(col == n, k_tr, 0.0).astype(BF16)
            s1_ref[n, h] = decay * s_n + _dot(k_n, v_h)
        outs.append(o_intra + jnp.concatenate(o_inter, axis=0))
    o_ref[...] = jnp.concatenate(outs, axis=1)


def _gla_sample(proj, proj_a, w_a2_p, b_a, s0, nb, seq):
    n_batch = s0.shape[0]
    rows = nb * seq
    kern = functools.partial(_gla_sample_kernel, nb=nb, seq=seq)
    return pl.pallas_call(
        kern,
        grid=(n_batch // nb,),
        in_specs=[
            pl.BlockSpec((rows, K_B), lambda i: (i, OFF_Q // K_B)),
            pl.BlockSpec((rows, K_B), lambda i: (i, OFF_K // K_B)),
            pl.BlockSpec((rows, E_B), lambda i: (i, OFF_VB // E_B)),
            pl.BlockSpec((rows, A_PAD), lambda i: (i, 0)),
            pl.BlockSpec((A_PAD, K_B), lambda i: (0, 0)),
            pl.BlockSpec((1, K_B), lambda i: (0, 0)),
            pl.BlockSpec((nb, HB, DK, DV), lambda i: (i, 0, 0, 0)),
        ],
        out_specs=[
            pl.BlockSpec((rows, E_B), lambda i: (i, 0)),
            pl.BlockSpec((nb, HB, DK, DV), lambda i: (i, 0, 0, 0)),
        ],
        out_shape=[
            jax.ShapeDtypeStruct((n_batch * seq, E_B), F32),
            jax.ShapeDtypeStruct(s0.shape, F32),
        ],
        compiler_params=pltpu.CompilerParams(
            dimension_semantics=("arbitrary",),
            vmem_limit_bytes=VMEM_LIMIT),
        name="gla_sample",
    )(proj, proj, proj, proj_a, w_a2_p, b_a, s0)


def _branch_a(proj_ref, g_v_ref, w_s_ref, b_col_ref, tm, seq):
    vn = _rms(proj_ref[:, OFF_V:OFF_V + E_A].astype(F32), g_v_ref[...])
    vn_b = vn.astype(BF16)
    t = lax.broadcasted_iota(jnp.int32, (CHUNK, CHUNK), 0)
    s = lax.broadcasted_iota(jnp.int32, (CHUNK, CHUNK), 1)
    causal = s <= t
    if seq < CHUNK:
        sh = seq.bit_length() - 1
        causal = causal & ((t >> sh) == (s >> sh))
    w_sp = [jnp.where(causal, w_s_ref[g], 0.0).astype(BF16) for g in range(HA)]
    yield
    y_rows = []
    for c in range(tm // CHUNK):
        r = slice(c * CHUNK, (c + 1) * CHUNK)
        gate = jnp.concatenate(
            [_dot(w_sp[g], vn_b[r, g * DA:(g + 1) * DA]) + b_col_ref[:, g:g + 1]
             for g in range(HA)], axis=1)
        z = proj_ref[r, OFF_Z:OFF_Z + E_A].astype(F32)
        y_rows.append((proj_ref[r, OFF_U:OFF_U + E_A].astype(F32) * gate
                       * (z * _sigmoid(z))).astype(BF16))
        yield
    y_a = y_rows[0] if len(y_rows) == 1 else jnp.concatenate(y_rows, axis=0)
    return y_a, vn


def _prompt_gla(proj_ref, proj_a_ref, w_a2_ref, b_a_ref, g_o_ref, states, tm):
    log_a = _log_decay(proj_a_ref[...], w_a2_ref[...], b_a_ref[...])
    t = lax.broadcasted_iota(jnp.int32, (CHUNK, CHUNK), 0)
    s = lax.broadcasted_iota(jnp.int32, (CHUNK, CHUNK), 1)
    tril_ones = jnp.where(s <= t, 1.0, 0.0).astype(BF16)
    masks = _gla_masks(CHUNK)
    yield
    y_rows = []
    for c in range(tm // CHUNK):
        r = slice(c * CHUNK, (c + 1) * CHUNK)
        q = proj_ref[r, OFF_Q:OFF_Q + K_B].astype(F32) * (DK ** -0.5)
        k = proj_ref[r, OFF_K:OFF_K + K_B].astype(F32)
        vb = proj_ref[r, OFF_VB:OFF_VB + E_B]
        o_c, states = yield from _gla_chunk(q, k, vb, log_a[r], states, tril_ones, masks)
        y_rows.append(_branch_b_out(o_c, proj_ref[r, OFF_ZB:OFF_ZB + E_B], g_o_ref).astype(BF16))
        yield
    y_b = y_rows[0] if len(y_rows) == 1 else jnp.concatenate(y_rows, axis=0)
    return y_b, states


def _branch_b_out(o, zb, g_o_ref):
    g_o = g_o_ref[...]
    on = jnp.concatenate(
        [_rms(o[:, h * DV:(h + 1) * DV], g_o[:, h * DV:(h + 1) * DV]) for h in range(HB)], axis=1)
    zb = zb.astype(F32)
    return on * (zb * _sigmoid(zb))


BACK_BLOCKS = 4


def _back(y_a, y_b, m_a_ref, m_b_ref, x_ref, p_ref,
          w_pa_ref, w_pb_ref, w_o_ref, g_post_ref, w_pg_ref, w_pe_ref, col0=(0, 0)):
    bw = D_MODEL // BACK_BLOCKS
    blocks = [slice(j * bw, (j + 1) * bw) for j in range(BACK_BLOCKS)]
    m_parts = []
    for c in blocks:
        pa = _dot(y_a, w_pa_ref[:, c])
        yield
        pb = _dot(y_b, w_pb_ref[:, c])
        m_a = m_a_ref[:, col0[0] + c.start:col0[0] + c.stop].astype(F32)
        m_b = m_b_ref[:, col0[1] + c.start:col0[1] + c.stop].astype(F32)
        m_parts.append((_sigmoid(m_a) * pa + _sigmoid(m_b) * pb).astype(BF16))
        yield
    m = jnp.concatenate(m_parts, axis=1)
    mo_parts = []
    for c in blocks:
        mo_parts.append(_dot(m, w_o_ref[:, c]))
        yield
    h_res = x_ref[...] + _rms(jnp.concatenate(mo_parts, axis=1), g_post_ref[...])
    h_b = h_res.astype(BF16)
    p_b = p_ref[...].astype(BF16)
    yield
    y_parts = []
    for c in blocks:
        pg = _sigmoid(_dot(h_b, w_pg_ref[:, c]))
        y_parts.append(h_res[:, c] + pg * _dot(p_b, w_pe_ref[:, c]))
        yield
    return jnp.concatenate(y_parts, axis=1)


def _post_prompt_kernel(proj_ref, proj_a_ref, m_a_ref, m_b_ref, x_ref, p_ref,
                        g_v_ref, w_s_ref, b_col_ref, w_a2_ref, b_a_ref, g_o_ref,
                        w_pa_ref, w_pb_ref, w_o_ref, g_post_ref, w_pg_ref, w_pe_ref,
                        y_ref, s_out_ref, s_ref, ya_ref, yb_ref, *, tm, seq, n_tiles):
    step = pl.program_id(0)
    nt = seq // tm

    def front():
        keep = lax.rem(step, nt) != 0
        states = [jnp.where(keep, s_ref[h], 0.0) for h in range(HB)]
        y_a, _ = yield from _branch_a(proj_ref, g_v_ref, w_s_ref, b_col_ref, tm, seq)
        y_b, states = yield from _prompt_gla(proj_ref, proj_a_ref, w_a2_ref, b_a_ref, g_o_ref,
                                             states, tm)
        return y_a, y_b, states

    def back():
        return _back(ya_ref[...], yb_ref[...], m_a_ref, m_b_ref, x_ref, p_ref,
                     w_pa_ref, w_pb_ref, w_o_ref, g_post_ref, w_pg_ref, w_pe_ref)

    def store_front(y_a, y_b, states):
        ya_ref[...] = y_a
        yb_ref[...] = y_b
        for h in range(HB):
            s_ref[h] = states[h]
            s_out_ref[0, h] = states[h]

    @pl.when(step == 0)
    def _():
        store_front(*_run_interleaved(front())[0])

    @pl.when((step > 0) & (step < n_tiles))
    def _():
        y, new = _run_interleaved(back(), front())
        y_ref[...] = y
        store_front(*new)

    @pl.when(step == n_tiles)
    def _():
        y_ref[...] = _run_interleaved(back())[0]


def _post_sample_kernel(proj_ref, x_ref, p_ref, o_ref, g_v_ref, w_s_ref, b_col_ref, g_o_ref,
                        w_pa_ref, w_pb_ref, w_o_ref, g_post_ref, w_pg_ref, w_pe_ref,
                        y_ref, vn_ref, *, tm, seq):
    (y_a, vn), = _run_interleaved(_branch_a(proj_ref, g_v_ref, w_s_ref, b_col_ref, tm, seq))
    vn_ref[...] = vn
    y_b = _branch_b_out(o_ref[...], proj_ref[:, OFF_ZB:OFF_ZB + E_B], g_o_ref).astype(BF16)
    y_ref[...], = _run_interleaved(
        _back(y_a, y_b, proj_ref, proj_ref, x_ref, p_ref,
              w_pa_ref, w_pb_ref, w_o_ref, g_post_ref, w_pg_ref, w_pe_ref, col0=(OFF_MA, OFF_MB)))


def _const_spec(shape):
    nd = len(shape)
    return pl.BlockSpec(shape, lambda *_: (0,) * nd, pipeline_mode=pl.Buffered(1))


def _post_prompt(proj, proj_a, x2d, p2d, wts, n_batch, seq, tm):
    nt = seq // tm
    n_tiles = n_batch * nt
    kern = functools.partial(_post_prompt_kernel, tm=tm, seq=seq, n_tiles=n_tiles)
    cur = lambda s: (jnp.minimum(s, n_tiles - 1), 0)
    prev = lambda s: (jnp.maximum(s - 1, 0), 0)
    names = ("g_v", "w_s", "b_col", "w_a2", "b_a", "g_o", "w_pa", "w_pb", "w_o", "g_post", "w_pg", "w_pe")
    w_args = [wts[n] for n in names]
    return pl.pallas_call(
        kern,
        grid=(n_tiles + 1,),
        in_specs=[
            pl.BlockSpec((tm, OFF_MA), cur),
            pl.BlockSpec((tm, A_PAD), cur),
            pl.BlockSpec((tm, D_MODEL), lambda s: (jnp.maximum(s - 1, 0), OFF_MA // D_MODEL)),
            pl.BlockSpec((tm, D_MODEL), lambda s: (jnp.maximum(s - 1, 0), OFF_MB // D_MODEL)),
            pl.BlockSpec((tm, D_MODEL), prev),
            pl.BlockSpec((tm, PLE_DIM), prev),
        ] + [_const_spec(w.shape) for w in w_args],
        out_specs=[
            pl.BlockSpec((tm, D_MODEL), prev),
            pl.BlockSpec((1, HB, DK, DV), lambda s: (jnp.minimum(s, n_tiles - 1) // nt, 0, 0, 0)),
        ],
        out_shape=[
            jax.ShapeDtypeStruct((n_batch * seq, D_MODEL), F32),
            jax.ShapeDtypeStruct((n_batch, HB, DK, DV), F32),
        ],
        scratch_shapes=[
            pltpu.VMEM((HB, DK, DV), F32),
            pltpu.VMEM((tm, E_A), BF16),
            pltpu.VMEM((tm, E_B), BF16),
        ],
        compiler_params=pltpu.CompilerParams(
            dimension_semantics=("arbitrary",),
            vmem_limit_bytes=VMEM_LIMIT),
        name="post_prompt",
    )(proj, proj_a, proj, proj, x2d, p2d, *w_args)


def _post_sample(proj, x2d, p2d, o_gla, wts, seq, tm):
    t = x2d.shape[0]
    kern = functools.partial(_post_sample_kernel, tm=tm, seq=seq)
    row = lambda i: (i, 0)
    names = ("g_v", "w_s_bd", "b_col_bd", "g_o", "w_pa", "w_pb", "w_o", "g_post", "w_pg", "w_pe")
    w_args = [wts[n] for n in names]
    return pl.pallas_call(
        kern,
        grid=(t // tm,),
        in_specs=[
            pl.BlockSpec((tm, N_MAIN), row),
            pl.BlockSpec((tm, D_MODEL), row),
            pl.BlockSpec((tm, PLE_DIM), row),
            pl.BlockSpec((tm, E_B), row),
        ] + [_const_spec(w.shape) for w in w_args],
        out_specs=[
            pl.BlockSpec((tm, D_MODEL), row),
            pl.BlockSpec((tm, E_A), row),
        ],
        out_shape=[
            jax.ShapeDtypeStruct((t, D_MODEL), F32),
            jax.ShapeDtypeStruct((t, E_A), F32),
        ],
        compiler_params=pltpu.CompilerParams(
            dimension_semantics=("arbitrary",),
            vmem_limit_bytes=VMEM_LIMIT),
        name="post_sample",
    )(proj, x2d, p2d, o_gla, *w_args)


def _layer_weights(i, dec_seq, g_pre, w_in, g_v, w_s, b_s, w_a2, b_a, g_o,
                   w_pa, w_pb, w_o, g_post, w_pg, w_pe):
    w_a2_p = jnp.concatenate(
        [w_a2[i], jnp.zeros((A_PAD - GATE_RANK, K_B), w_a2.dtype)], axis=0).astype(BF16)
    reps = CHUNK // dec_seq
    return dict(
        g_pre=g_pre[i][None, :],
        w_in_t=w_in[i].T,
        g_v=g_v[i][None, :],
        w_s=w_s[i],
        b_col=b_s[i].T,
        w_s_bd=jnp.tile(w_s[i][:, :dec_seq, :dec_seq], (1, reps, reps)),
        b_col_bd=jnp.tile(b_s[i][:, :dec_seq].T, (reps, 1)),
        w_a2=w_a2_p,
        b_a=b_a[i][None, :],
        g_o=g_o[i].reshape(1, E_B),
        g_post=g_post[i][None, :],
    )


SIDE_WEIGHTS = ("w_pa", "w_pb", "w_o", "w_pg", "w_pe")


def kernel(x_prompt, x_sample, p_prompt, p_sample, state_gla, g_pre, w_in, g_v, w_s, b_s,
           w_a2, b_a, g_o, w_pa, w_pb, w_o, g_post, w_pg, w_pe):
    depth = w_in.shape[0]
    n_batch, seq, _ = x_prompt.shape
    dec_batch, dec_seq, _ = x_sample.shape
    hp = x_prompt.reshape(n_batch * seq, D_MODEL)
    hs = x_sample.reshape(dec_batch * dec_seq, D_MODEL)
    sp_list, ss_list, cv_list = [], [], []
    for i in range(depth):
        wts = _layer_weights(i, dec_seq, g_pre, w_in, g_v, w_s, b_s, w_a2, b_a, g_o,
                             w_pa, w_pb, w_o, g_post, w_pg, w_pe)
        side_f32 = dict(w_pa=w_pa[i], w_pb=w_pb[i], w_o=w_o[i], w_pg=w_pg[i], w_pe=w_pe[i])
        proj_p, proj_a_p, *side = _inproj(hp, wts["g_pre"], wts["w_in_t"], tm=1024,
                                          cast=[side_f32[n] for n in SIDE_WEIGHTS])
        wts.update(zip(SIDE_WEIGHTS, side))
        proj_s, proj_a_s = _inproj(hs, wts["g_pre"], wts["w_in_t"], tm=1024)
        o_s, s_s = _gla_sample(proj_s, proj_a_s, wts["w_a2"], wts["b_a"], state_gla[i],
                               nb=CHUNK // dec_seq, seq=dec_seq)
        hp, s_p = _post_prompt(proj_p, proj_a_p, hp, p_prompt[i].reshape(n_batch * seq, PLE_DIM),
                               wts, n_batch, seq, tm=256)
        hs, v_s = _post_sample(proj_s, hs, p_sample[i].reshape(dec_batch * dec_seq, PLE_DIM),
                               o_s, wts, dec_seq, tm=256)
        sp_list.append(s_p)
        ss_list.append(s_s)
        cv_list.append(v_s.reshape(dec_batch, dec_seq, E_A))
    stack = (lambda xs: xs[0][None]) if depth == 1 else jnp.stack
    return (hp.reshape(n_batch, seq, D_MODEL),
            hs.reshape(dec_batch, dec_seq, D_MODEL),
            stack(sp_list), stack(ss_list), stack(cv_list))
```

```python
import functools

import jax
import jax.numpy as jnp
from jax import lax
from jax.experimental import pallas as pl
from jax.experimental.pallas import tpu as pltpu

D_MODEL = 2048
E_A = D_MODEL // 2
HA = 4
DA = E_A // HA
CHUNK = 128
HB = 4
E_B = D_MODEL // 2
DV = E_B // HB
DK = DV // 2
K_B = HB * DK
GATE_RANK = 16
GATE_TAU = 16.0
GLA_SUB = 16
PLE_DIM = 256
EPS = 1e-6

LANES = 128
OFF_U = 0
OFF_V = OFF_U + E_A
OFF_Z = OFF_V + E_A
OFF_Q = OFF_Z + E_A
OFF_K = OFF_Q + K_B
OFF_VB = OFF_K + K_B
OFF_ZB = OFF_VB + E_B
OFF_MA = OFF_ZB + E_B
OFF_MB = OFF_MA + D_MODEL
N_MAIN = OFF_MB + D_MODEL
W_IN_A = OFF_MA
A_PAD = LANES
INPROJ_TN = 1024

F32 = jnp.float32
BF16 = jnp.bfloat16

VMEM_LIMIT = 56 * 1024 * 1024


def _dot(a, b):
    return jnp.dot(a, b, preferred_element_type=F32)


def _dot_nt(a, b):
    return lax.dot_general(a, b, (((1,), (1,)), ((), ())), preferred_element_type=F32)


def _sigmoid(x):
    return 1.0 / (1.0 + jnp.exp(-x))


def _rms(x, g):
    return x * lax.rsqrt(jnp.mean(x * x, axis=-1, keepdims=True) + EPS) * g


def _split_bf16(x):
    hi = x.astype(BF16)
    lo = (x - hi.astype(F32)).astype(BF16)
    return hi, lo


def _log_decay(a_lr, w_a2, b_a):
    pre = _dot(a_lr, w_a2) + b_a
    return (jnp.minimum(pre, 0.0) - jnp.log(1.0 + jnp.exp(-jnp.abs(pre)))) * (1.0 / GATE_TAU)


def _bcast_rows(b, idxs, seg):
    parts = []
    for i in idxs:
        if i < 0:
            parts.append(jnp.zeros((seg, b.shape[1]), b.dtype))
        else:
            parts.append(jnp.broadcast_to(b[i:i + 1, :], (seg, b.shape[1])))
    return parts[0] if len(parts) == 1 else jnp.concatenate(parts, axis=0)


BF16_SUBLANES = 16


def _inproj_kernel(*refs, n_cast):
    x_ref, g_ref, w_ref, w_a_ref = refs[:4]
    cast_in = refs[4:4 + n_cast]
    o_ref, o_a_ref = refs[4 + n_cast:6 + n_cast]
    cast_out = refs[6 + n_cast:6 + 2 * n_cast]
    xn_ref = refs[6 + 2 * n_cast]

    @pl.when(pl.program_id(1) == 0)
    def _():
        xn = _rms(x_ref[...], g_ref[...]).astype(BF16)
        xn_ref[...] = xn
        o_a_ref[...] = _dot_nt(xn, w_a_ref[...].astype(BF16)).astype(o_a_ref.dtype)

    for src, dst in zip(cast_in, cast_out):
        dst[...] = src[...].astype(dst.dtype)
    o_ref[...] = _dot_nt(xn_ref[...], w_ref[...].astype(BF16)).astype(o_ref.dtype)


def _inproj(x2d, g_pre, w_in_t, tm, cast=()):
    t = x2d.shape[0]
    tn = INPROJ_TN
    n_i, n_j = t // tm, N_MAIN // tn

    def w_rows(i, j):
        row0 = j * tn + jnp.where(j * tn >= OFF_MA, GATE_RANK, 0)
        return (pl.multiple_of(row0, GATE_RANK), 0)

    cast_specs = []
    for w in cast:
        per_i = w.shape[0] // n_i
        assert per_i * n_i == w.shape[0] and per_i % BF16_SUBLANES == 0
        n_sub = max(n for n in range(1, n_j + 1)
                    if per_i % n == 0 and (per_i // n) % BF16_SUBLANES == 0)
        cast_specs.append(pl.BlockSpec(
            (per_i // n_sub, w.shape[1]),
            lambda i, j, n_sub=n_sub: (i * n_sub + jnp.minimum(j, n_sub - 1), 0)))

    kern = functools.partial(_inproj_kernel, n_cast=len(cast))
    return pl.pallas_call(
        kern,
        grid=(n_i, n_j),
        in_specs=[
            pl.BlockSpec((tm, D_MODEL), lambda i, j: (i, 0)),
            pl.BlockSpec((1, D_MODEL), lambda i, j: (0, 0)),
            pl.BlockSpec((pl.Element(tn), pl.Element(D_MODEL)), w_rows),
            pl.BlockSpec((A_PAD, D_MODEL), lambda i, j: (W_IN_A // A_PAD, 0)),
        ] + cast_specs,
        out_specs=[
            pl.BlockSpec((tm, tn), lambda i, j: (i, j)),
            pl.BlockSpec((tm, A_PAD), lambda i, j: (i, 0)),
        ] + cast_specs,
        out_shape=[
            jax.ShapeDtypeStruct((t, N_MAIN), BF16),
            jax.ShapeDtypeStruct((t, A_PAD), BF16),
        ] + [jax.ShapeDtypeStruct(w.shape, BF16) for w in cast],
        scratch_shapes=[pltpu.VMEM((tm, D_MODEL), BF16)],
        compiler_params=pltpu.CompilerParams(
            dimension_semantics=("arbitrary", "arbitrary"),
            vmem_limit_bytes=VMEM_LIMIT),
        name="inproj",
    )(x2d, g_pre, w_in_t, w_in_t, *cast)


def _gla_levels(c):
    levels = []
    blk = c
    while blk > GLA_SUB:
        per = blk // GLA_SUB
        levels.append((blk, [(j // per) * per + per // 2 - 1 for j in range(c // GLA_SUB)]))
        blk //= 2
    return levels


def _gla_masks(c):
    t = lax.broadcasted_iota(jnp.int32, (c, c), 0)
    s = lax.broadcasted_iota(jnp.int32, (c, c), 1)
    levels = _gla_levels(c)
    masks = []
    for blk, _ in levels:
        sh = blk.bit_length() - 1
        half = blk // 2
        same = (t >> sh) == (s >> sh)
        masks.append(same & ((t & (blk - 1)) >= half) & ((s & (blk - 1)) < half))
    sh = GLA_SUB.bit_length() - 1
    masks.append(((t >> sh) == (s >> sh)) & (s <= t))
    return masks


def _run_interleaved(*gens):
    results = [None] * len(gens)
    live = list(range(len(gens)))
    while live:
        for i in list(live):
            try:
                next(gens[i])
            except StopIteration as stop:
                results[i] = stop.value
                live.remove(i)
    return results


def _gla_chunk(q, k, vb, log_a, s_heads, tril_ones, masks):
    c = q.shape[0]
    hi, lo = _split_bf16(log_a)
    b = _dot(tril_ones, hi) + _dot(tril_ones, lo)
    n = c // GLA_SUB

    def rep(x):
        return _bcast_rows(x, list(range(n)), GLA_SUB)

    ends = jnp.concatenate([b[(j + 1) * GLA_SUB - 1:(j + 1) * GLA_SUB, :] for j in range(n)], axis=0)
    starts = jnp.concatenate([jnp.zeros((1, b.shape[1]), F32), ends[:n - 1]], axis=0)
    starts_r = rep(starts)
    q_e = q * jnp.exp(b - starts_r)
    k_e = k * jnp.exp(rep(ends) - b)

    qs, ks = [], []
    for _, ref_blocks in _gla_levels(c):
        ref = jnp.concatenate([ends[r:r + 1] for r in ref_blocks], axis=0)
        qs.append((q_e * rep(jnp.exp(jnp.minimum(starts - ref, 0.0)))).astype(BF16))
        ks.append((k_e * rep(jnp.exp(jnp.minimum(ref - ends, 0.0)))).astype(BF16))
    qs.append(q_e.astype(BF16))
    ks.append((k * jnp.exp(starts_r - b)).astype(BF16))

    q_in = (q_e * rep(jnp.exp(starts))).astype(BF16)
    k_out = k_e * rep(jnp.exp(ends[n - 1:n] - ends))
    yield

    outs, new_states = [], []
    for h in range(HB):
        kc = slice(h * DK, (h + 1) * DK)
        vc = slice(h * DV, (h + 1) * DV)
        att = jnp.zeros((c, c), F32)
        for ql, kl, m in zip(qs, ks, masks):
            att = jnp.where(m, _dot_nt(ql[:, kc], kl[:, kc]), att)
        v_h = vb[:, vc]
        s_h = s_heads[h]
        o_h = _dot(att.astype(BF16), v_h) + _dot(q_in[:, kc], s_h.astype(BF16))
        b_t = jnp.transpose(b[:, kc])
        decay = jnp.exp(jnp.broadcast_to(b_t[:, c - 1:c], (DK, DV)))
        k_t = jnp.transpose(k_out[:, kc]).astype(BF16)
        new_states.append(decay * s_h + _dot(k_t, v_h))
        outs.append(o_h)
        yield
    return jnp.concatenate(outs, axis=1), new_states


def _gla_sample_kernel(q_ref, k_ref, vb_ref, a_ref, w_a2_ref, b_a_ref, s0_ref,
                       o_ref, s1_ref, *, nb, seq):
    rows = nb * seq
    log_a = _log_decay(a_ref[...], w_a2_ref[...], b_a_ref[...])
    t = lax.broadcasted_iota(jnp.int32, (rows, rows), 0)
    s = lax.broadcasted_iota(jnp.int32, (rows, rows), 1)
    sh = seq.bit_length() - 1
    causal = ((t >> sh) == (s >> sh)) & (s <= t)
    tril_ones = jnp.where(causal, 1.0, 0.0).astype(BF16)
    hi, lo = _split_bf16(log_a)
    b = _dot(tril_ones, hi) + _dot(tril_ones, lo)

    q = q_ref[...].astype(F32) * (DK ** -0.5)
    k = k_ref[...].astype(F32)
    vb = vb_ref[...]
    q_t = q * jnp.exp(b)
    k_t = (k * jnp.exp(-b)).astype(BF16)
    b_last = _bcast_rows(b, [n * seq + seq - 1 for n in range(nb)], seq)
    k_out = k * jnp.exp(b_last - b)
    col = lax.broadcasted_iota(jnp.int32, (DK, rows), 1) >> sh

    outs = []
    for h in range(HB):
        kc = slice(h * DK, (h + 1) * DK)
        vc = slice(h * DV, (h + 1) * DV)
        v_h = vb[:, vc]
        att = jnp.where(causal, _dot_nt(q_t[:, kc].astype(BF16), k_t[:, kc]), 0.0)
        o_intra = _dot(att.astype(BF16), v_h)
        b_tr = jnp.transpose(b[:, kc])
        k_tr = jnp.transpose(k_out[:, kc])
        o_inter = []
        for n in range(nb):
            s_n = s0_ref[n, h]
            r0 = n * seq
            o_inter.append(_dot(q_t[r0:r0 + seq, kc], s_n))
            decay = jnp.exp(jnp.broadcast_to(b_tr[:, r0 + seq - 1:r0 + seq], (DK, DV)))
            k_n = jnp.where(col == n, k_tr, 0.0).astype(BF16)
            s1_ref[n, h] = decay * s_n + _dot(k_n, v_h)
        outs.append(o_intra + jnp.concatenate(o_inter, axis=0))
    o_ref[...] = jnp.concatenate(outs, axis=1)


def _gla_sample(proj, proj_a, w_a2_p, b_a, s0, nb, seq):
    n_batch = s0.shape[0]
    rows = nb * seq
    kern = functools.partial(_gla_sample_kernel, nb=nb, seq=seq)
    return pl.pallas_call(
        kern,
        grid=(n_batch // nb,),
        in_specs=[
            pl.BlockSpec((rows, K_B), lambda i: (i, OFF_Q // K_B)),
            pl.BlockSpec((rows, K_B), lambda i: (i, OFF_K // K_B)),
            pl.BlockSpec((rows, E_B), lambda i: (i, OFF_VB // E_B)),
            pl.BlockSpec((rows, A_PAD), lambda i: (i, 0)),
            pl.BlockSpec((A_PAD, K_B), lambda i: (0, 0)),
            pl.BlockSpec((1, K_B), lambda i: (0, 0)),
            pl.BlockSpec((nb, HB, DK, DV), lambda i: (i, 0, 0, 0)),
        ],
        out_specs=[
            pl.BlockSpec((rows, E_B), lambda i: (i, 0)),
            pl.BlockSpec((nb, HB, DK, DV), lambda i: (i, 0, 0, 0)),
        ],
        out_shape=[
            jax.ShapeDtypeStruct((n_batch * seq, E_B), F32),
            jax.ShapeDtypeStruct(s0.shape, F32),
        ],
        compiler_params=pltpu.CompilerParams(
            dimension_semantics=("arbitrary",),
            vmem_limit_bytes=VMEM_LIMIT),
        name="gla_sample",
    )(proj, proj, proj, proj_a, w_a2_p, b_a, s0)


def _branch_a(proj_ref, g_v_ref, w_s_ref, b_col_ref, tm, seq):
    vn = _rms(proj_ref[:, OFF_V:OFF_V + E_A].astype(F32), g_v_ref[...])
    vn_b = vn.astype(BF16)
    t = lax.broadcasted_iota(jnp.int32, (CHUNK, CHUNK), 0)
    s = lax.broadcasted_iota(jnp.int32, (CHUNK, CHUNK), 1)
    causal = s <= t
    if seq < CHUNK:
        sh = seq.bit_length() - 1
        causal = causal & ((t >> sh) == (s >> sh))
    w_sp = [jnp.where(causal, w_s_ref[g], 0.0).astype(BF16) for g in range(HA)]
    yield
    y_rows = []
    for c in range(tm // CHUNK):
        r = slice(c * CHUNK, (c + 1) * CHUNK)
        gate = jnp.concatenate(
            [_dot(w_sp[g], vn_b[r, g * DA:(g + 1) * DA]) + b_col_ref[:, g:g + 1]
             for g in range(HA)], axis=1)
        z = proj_ref[r, OFF_Z:OFF_Z + E_A].astype(F32)
        y_rows.append((proj_ref[r, OFF_U:OFF_U + E_A].astype(F32) * gate
                       * (z * _sigmoid(z))).astype(BF16))
        yield
    y_a = y_rows[0] if len(y_rows) == 1 else jnp.concatenate(y_rows, axis=0)
    return y_a, vn


def _prompt_gla(proj_ref, proj_a_ref, w_a2_ref, b_a_ref, g_o_ref, states, tm):
    log_a = _log_decay(proj_a_ref[...], w_a2_ref[...], b_a_ref[...])
    t = lax.broadcasted_iota(jnp.int32, (CHUNK, CHUNK), 0)
    s = lax.broadcasted_iota(jnp.int32, (CHUNK, CHUNK), 1)
    tril_ones = jnp.where(s <= t, 1.0, 0.0).astype(BF16)
    masks = _gla_masks(CHUNK)
    yield
    y_rows = []
    for c in range(tm // CHUNK):
        r = slice(c * CHUNK, (c + 1) * CHUNK)
        q = proj_ref[r, OFF_Q:OFF_Q + K_B].astype(F32) * (DK ** -0.5)
        k = proj_ref[r, OFF_K:OFF_K + K_B].astype(F32)
        vb = proj_ref[r, OFF_VB:OFF_VB + E_B]
        o_c, states = yield from _gla_chunk(q, k, vb, log_a[r], states, tril_ones, masks)
        y_rows.append(_branch_b_out(o_c, proj_ref[r, OFF_ZB:OFF_ZB + E_B], g_o_ref).astype(BF16))
        yield
    y_b = y_rows[0] if len(y_rows) == 1 else jnp.concatenate(y_rows, axis=0)
    return y_b, states


def _branch_b_out(o, zb, g_o_ref):
    g_o = g_o_ref[...]
    on = jnp.concatenate(
        [_rms(o[:, h * DV:(h + 1) * DV], g_o[:, h * DV:(h + 1) * DV]) for h in range(HB)], axis=1)
    zb = zb.astype(F32)
    return on * (zb * _sigmoid(zb))


BACK_BLOCKS = 4


def _back(y_a, y_b, m_a_ref, m_b_ref, x_ref, p_ref,
          w_pa_ref, w_pb_ref, w_o_ref, g_post_ref, w_pg_ref, w_pe_ref, col0=(0, 0)):
    bw = D_MODEL // BACK_BLOCKS
    blocks = [slice(j * bw, (j + 1) * bw) for j in range(BACK_BLOCKS)]
    m_parts = []
    for c in blocks:
        pa = _dot(y_a, w_pa_ref[:, c])
        yield
        pb = _dot(y_b, w_pb_ref[:, c])
        m_a = m_a_ref[:, col0[0] + c.start:col0[0] + c.stop].astype(F32)
        m_b = m_b_ref[:, col0[1] + c.start:col0[1] + c.stop].astype(F32)
        m_parts.append((_sigmoid(m_a) * pa + _sigmoid(m_b) * pb).astype(BF16))
        yield
    m = jnp.concatenate(m_parts, axis=1)
    mo_parts = []
    for c in blocks:
        mo_parts.append(_dot(m, w_o_ref[:, c]))
        yield
    h_res = x_ref[...] + _rms(jnp.concatenate(mo_parts, axis=1), g_post_ref[...])
    h_b = h_res.astype(BF16)
    p_b = p_ref[...].astype(BF16)
    yield
    y_parts = []
    for c in blocks:
        pg = _sigmoid(_dot(h_b, w_pg_ref[:, c]))
        y_parts.append(h_res[:, c] + pg * _dot(p_b, w_pe_ref[:, c]))
        yield
    return jnp.concatenate(y_parts, axis=1)


def _post_prompt_kernel(proj_ref, proj_a_ref, m_a_ref, m_b_ref, x_ref, p_ref,
                        g_v_ref, w_s_ref, b_col_ref, w_a2_ref, b_a_ref, g_o_ref,
                        w_pa_ref, w_pb_ref, w_o_ref, g_post_ref, w_pg_ref, w_pe_ref,
                        y_ref, s_out_ref, s_ref, ya_ref, yb_ref, *, tm, seq, n_tiles):
    step = pl.program_id(0)
    nt = seq // tm

    def front():
        keep = lax.rem(step, nt) != 0
        states = [jnp.where(keep, s_ref[h], 0.0) for h in range(HB)]
        y_a, _ = yield from _branch_a(proj_ref, g_v_ref, w_s_ref, b_col_ref, tm, seq)
        y_b, states = yield from _prompt_gla(proj_ref, proj_a_ref, w_a2_ref, b_a_ref, g_o_ref,
                                             states, tm)
        return y_a, y_b, states

    def back():
        return _back(ya_ref[...], yb_ref[...], m_a_ref, m_b_ref, x_ref, p_ref,
                     w_pa_ref, w_pb_ref, w_o_ref, g_post_ref, w_pg_ref, w_pe_ref)

    def store_front(y_a, y_b, states):
        ya_ref[...] = y_a
        yb_ref[...] = y_b
        for h in range(HB):
            s_ref[h] = states[h]
            s_out_ref[0, h] = states[h]

    @pl.when(step == 0)
    def _():
        store_front(*_run_interleaved(front())[0])

    @pl.when((step > 0) & (step < n_tiles))
    def _():
        y, new = _run_interleaved(back(), front())
        y_ref[...] = y
        store_front(*new)

    @pl.when(step == n_tiles)
    def _():
        y_ref[...] = _run_interleaved(back())[0]


def _post_sample_kernel(proj_ref, x_ref, p_ref, o_ref, g_v_ref, w_s_ref, b_col_ref, g_o_ref,
                        w_pa_ref, w_pb_ref, w_o_ref, g_post_ref, w_pg_ref, w_pe_ref,
                        y_ref, vn_ref, *, tm, seq):
    (y_a, vn), = _run_interleaved(_branch_a(proj_ref, g_v_ref, w_s_ref, b_col_ref, tm, seq))
    vn_ref[...] = vn
    y_b = _branch_b_out(o_ref[...], proj_ref[:, OFF_ZB:OFF_ZB + E_B], g_o_ref).astype(BF16)
    y_ref[...], = _run_interleaved(
        _back(y_a, y_b, proj_ref, proj_ref, x_ref, p_ref,
              w_pa_ref, w_pb_ref, w_o_ref, g_post_ref, w_pg_ref, w_pe_ref, col0=(OFF_MA, OFF_MB)))


def _const_spec(shape):
    nd = len(shape)
    return pl.BlockSpec(shape, lambda *_: (0,) * nd, pipeline_mode=pl.Buffered(1))


def _post_prompt(proj, proj_a, x2d, p2d, wts, n_batch, seq, tm):
    nt = seq // tm
    n_tiles = n_batch * nt
    kern = functools.partial(_post_prompt_kernel, tm=tm, seq=seq, n_tiles=n_tiles)
    cur = lambda s: (jnp.minimum(s, n_tiles - 1), 0)
    prev = lambda s: (jnp.maximum(s - 1, 0), 0)
    names = ("g_v", "w_s", "b_col", "w_a2", "b_a", "g_o", "w_pa", "w_pb", "w_o", "g_post", "w_pg", "w_pe")
    w_args = [wts[n] for n in names]
    return pl.pallas_call(
        kern,
        grid=(n_tiles + 1,),
        in_specs=[
            pl.BlockSpec((tm, OFF_MA), cur),
            pl.BlockSpec((tm, A_PAD), cur),
            pl.BlockSpec((tm, D_MODEL), lambda s: (jnp.maximum(s - 1, 0), OFF_MA // D_MODEL)),
            pl.BlockSpec((tm, D_MODEL), lambda s: (jnp.maximum(s - 1, 0), OFF_MB // D_MODEL)),
            pl.BlockSpec((tm, D_MODEL), prev),
            pl.BlockSpec((tm, PLE_DIM), prev),
        ] + [_const_spec(w.shape) for w in w_args],
        out_specs=[
            pl.BlockSpec((tm, D_MODEL), prev),
            pl.BlockSpec((1, HB, DK, DV), lambda s: (jnp.minimum(s, n_tiles - 1) // nt, 0, 0, 0)),
        ],
        out_shape=[
            jax.ShapeDtypeStruct((n_batch * seq, D_MODEL), F32),
            jax.ShapeDtypeStruct((n_batch, HB, DK, DV), F32),
        ],
        scratch_shapes=[
            pltpu.VMEM((HB, DK, DV), F32),
            pltpu.VMEM((tm, E_A), BF16),
            pltpu.VMEM((tm, E_B), BF16),
        ],
        compiler_params=pltpu.CompilerParams(
            dimension_semantics=("arbitrary",),
            vmem_limit_bytes=VMEM_LIMIT),
        name="post_prompt",
    )(proj, proj_a, proj, proj, x2d, p2d, *w_args)


def _post_sample(proj, x2d, p2d, o_gla, wts, seq, tm):
    t = x2d.shape[0]
    kern = functools.partial(_post_sample_kernel, tm=tm, seq=seq)
    row = lambda i: (i, 0)
    names = ("g_v", "w_s_bd", "b_col_bd", "g_o", "w_pa", "w_pb", "w_o", "g_post", "w_pg", "w_pe")
    w_args = [wts[n] for n in names]
    return pl.pallas_call(
        kern,
        grid=(t // tm,),
        in_specs=[
            pl.BlockSpec((tm, N_MAIN), row),
            pl.BlockSpec((tm, D_MODEL), row),
            pl.BlockSpec((tm, PLE_DIM), row),
            pl.BlockSpec((tm, E_B), row),
        ] + [_const_spec(w.shape) for w in w_args],
        out_specs=[
            pl.BlockSpec((tm, D_MODEL), row),
            pl.BlockSpec((tm, E_A), row),
        ],
        out_shape=[
            jax.ShapeDtypeStruct((t, D_MODEL), F32),
            jax.ShapeDtypeStruct((t, E_A), F32),
        ],
        compiler_params=pltpu.CompilerParams(
            dimension_semantics=("arbitrary",),
            vmem_limit_bytes=VMEM_LIMIT),
        name="post_sample",
    )(proj, x2d, p2d, o_gla, *w_args)


def _layer_weights(i, dec_seq, g_pre, w_in, g_v, w_s, b_s, w_a2, b_a, g_o,
                   w_pa, w_pb, w_o, g_post, w_pg, w_pe):
    w_a2_p = jnp.concatenate(
        [w_a2[i], jnp.zeros((A_PAD - GATE_RANK, K_B), w_a2.dtype)], axis=0).astype(BF16)
    reps = CHUNK // dec_seq
    return dict(
        g_pre=g_pre[i][None, :],
        w_in_t=w_in[i].T,
        g_v=g_v[i][None, :],
        w_s=w_s[i],
        b_col=b_s[i].T,
        w_s_bd=jnp.tile(w_s[i][:, :dec_seq, :dec_seq], (1, reps, reps)),
        b_col_bd=jnp.tile(b_s[i][:, :dec_seq].T, (reps, 1)),
        w_a2=w_a2_p,
        b_a=b_a[i][None, :],
        g_o=g_o[i].reshape(1, E_B),
        g_post=g_post[i][None, :],
    )


SIDE_WEIGHTS = ("w_pa", "w_pb", "w_o", "w_pg", "w_pe")


def kernel(x_prompt, x_sample, p_prompt, p_sample, state_gla, g_pre, w_in, g_v, w_s, b_s,
           w_a2, b_a, g_o, w_pa, w_pb, w_o, g_post, w_pg, w_pe):
    depth = w_in.shape[0]
    n_batch, seq, _ = x_prompt.shape
    dec_batch, dec_seq, _ = x_sample.shape
    hp = x_prompt.reshape(n_batch * seq, D_MODEL)
    hs = x_sample.reshape(dec_batch * dec_seq, D_MODEL)
    sp_list, ss_list, cv_list = [], [], []
    for i in range(depth):
        wts = _layer_weights(i, dec_seq, g_pre, w_in, g_v, w_s, b_s, w_a2, b_a, g_o,
                             w_pa, w_pb, w_o, g_post, w_pg, w_pe)
        side_f32 = dict(w_pa=w_pa[i], w_pb=w_pb[i], w_o=w_o[i], w_pg=w_pg[i], w_pe=w_pe[i])
        proj_p, proj_a_p, *side = _inproj(hp, wts["g_pre"], wts["w_in_t"], tm=1024,
                                          cast=[side_f32[n] for n in SIDE_WEIGHTS])
        wts.update(zip(SIDE_WEIGHTS, side))
        proj_s, proj_a_s = _inproj(hs, wts["g_pre"], wts["w_in_t"], tm=1024)
        o_s, s_s = _gla_sample(proj_s, proj_a_s, wts["w_a2"], wts["b_a"], state_gla[i],
                               nb=CHUNK // dec_seq, seq=dec_seq)
        hp, s_p = _post_prompt(proj_p, proj_a_p, hp, p_prompt[i].reshape(n_batch * seq, PLE_DIM),
                               wts, n_batch, seq, tm=256)
        hs, v_s = _post_sample(proj_s, hs, p_sample[i].reshape(dec_batch * dec_seq, PLE_DIM),
                               o_s, wts, dec_seq, tm=256)
        sp_list.append(s_p)
        ss_list.append(s_s)
        cv_list.append(v_s.reshape(dec_batch, dec_seq, E_A))
    stack = (lambda xs: xs[0][None]) if depth == 1 else jnp.stack
    return (hp.reshape(n_batch, seq, D_MODEL),
            hs.reshape(dec_batch, dec_seq, D_MODEL),
            stack(sp_list), stack(ss_list), stack(cv_list))
```

```python
import functools

import jax
import jax.numpy as jnp
from jax import lax
from jax.experimental import pallas as pl
from jax.experimental.pallas import tpu as pltpu

D_MODEL = 2048
E_A = D_MODEL // 2
HA = 4
DA = E_A // HA
CHUNK = 128
HB = 4
E_B = D_MODEL // 2
DV = E_B // HB
DK = DV // 2
K_B = HB * DK
GATE_RANK = 16
GATE_TAU = 16.0
GLA_SUB = 16
PLE_DIM = 256
EPS = 1e-6

LANES = 128
OFF_U = 0
OFF_V = OFF_U + E_A
OFF_Z = OFF_V + E_A
OFF_Q = OFF_Z + E_A
OFF_K = OFF_Q + K_B
OFF_VB = OFF_K + K_B
OFF_ZB = OFF_VB + E_B
OFF_MA = OFF_ZB + E_B
OFF_MB = OFF_MA + D_MODEL
N_MAIN = OFF_MB + D_MODEL
W_IN_A = OFF_MA
A_PAD = LANES
INPROJ_TN = 1024
INPROJ_TN_BF16 = 2048

F32 = jnp.float32
BF16 = jnp.bfloat16

VMEM_LIMIT = 56 * 1024 * 1024


def _dot(a, b):
    return jnp.dot(a, b, preferred_element_type=F32)


def _dot_nt(a, b):
    return lax.dot_general(a, b, (((1,), (1,)), ((), ())), preferred_element_type=F32)


def _sigmoid(x):
    return 1.0 / (1.0 + jnp.exp(-x))


def _rms(x, g):
    return x * lax.rsqrt(jnp.mean(x * x, axis=-1, keepdims=True) + EPS) * g


def _split_bf16(x):
    hi = x.astype(BF16)
    lo = (x - hi.astype(F32)).astype(BF16)
    return hi, lo


def _log_decay(a_lr, w_a2, b_a):
    pre = _dot(a_lr, w_a2) + b_a
    return (jnp.minimum(pre, 0.0) - jnp.log(1.0 + jnp.exp(-jnp.abs(pre)))) * (1.0 / GATE_TAU)


def _bcast_rows(b, idxs, seg):
    parts = []
    for i in idxs:
        if i < 0:
            parts.append(jnp.zeros((seg, b.shape[1]), b.dtype))
        else:
            parts.append(jnp.broadcast_to(b[i:i + 1, :], (seg, b.shape[1])))
    return parts[0] if len(parts) == 1 else jnp.concatenate(parts, axis=0)


BF16_SUBLANES = 16


def _inproj_kernel(*refs, n_cast, emit_w):
    x_ref, g_ref, w_ref, w_a_ref = refs[:4]
    cast_in = refs[4:4 + n_cast]
    o_ref, o_a_ref = refs[4 + n_cast:6 + n_cast]
    cast_out = refs[6 + n_cast:6 + 2 * n_cast]
    rest = refs[6 + 2 * n_cast:]
    xn_ref = rest[-1]

    @pl.when(pl.program_id(1) == 0)
    def _():
        xn = _rms(x_ref[...], g_ref[...]).astype(BF16)
        xn_ref[...] = xn
        w_a = w_a_ref[...].astype(BF16)
        if emit_w:
            rest[1][...] = w_a
        o_a_ref[...] = _dot_nt(xn, w_a).astype(o_a_ref.dtype)

    for src, dst in zip(cast_in, cast_out):
        dst[...] = src[...].astype(dst.dtype)
    w = w_ref[...].astype(BF16)
    if emit_w:
        rest[0][...] = w
    o_ref[...] = _dot_nt(xn_ref[...], w).astype(o_ref.dtype)


def _inproj(x2d, g_pre, w_in_t, tm, w_bf=None, cast=()):
    t = x2d.shape[0]
    emit_w = w_bf is None
    tn = INPROJ_TN if emit_w else INPROJ_TN_BF16
    n_i, n_j = t // tm, N_MAIN // tn

    def w_rows(i, j):
        row0 = j * tn + jnp.where(j * tn >= OFF_MA, GATE_RANK, 0)
        return (pl.multiple_of(row0, GATE_RANK), 0)

    if emit_w:
        w_args = (w_in_t, w_in_t)
        w_specs = [pl.BlockSpec((pl.Element(tn), pl.Element(D_MODEL)), w_rows),
                   pl.BlockSpec((A_PAD, D_MODEL), lambda i, j: (W_IN_A // A_PAD, 0))]
        w_out_specs = [pl.BlockSpec((tn, D_MODEL), lambda i, j: (j, 0)),
                       pl.BlockSpec((A_PAD, D_MODEL), lambda i, j: (0, 0))]
        w_out_shapes = [jax.ShapeDtypeStruct((N_MAIN, D_MODEL), BF16),
                        jax.ShapeDtypeStruct((A_PAD, D_MODEL), BF16)]
        assert n_i == 1, "the bf16 weight copy is written once, by a single token tile"
    else:
        w_args = w_bf
        w_specs = [pl.BlockSpec((tn, D_MODEL), lambda i, j: (j, 0)),
                   pl.BlockSpec((A_PAD, D_MODEL), lambda i, j: (0, 0))]
        w_out_specs, w_out_shapes = [], []

    cast_specs = []
    for w in cast:
        per_i = w.shape[0] // n_i
        assert per_i * n_i == w.shape[0] and per_i % BF16_SUBLANES == 0
        n_sub = max(n for n in range(1, n_j + 1)
                    if per_i % n == 0 and (per_i // n) % BF16_SUBLANES == 0)
        cast_specs.append(pl.BlockSpec(
            (per_i // n_sub, w.shape[1]),
            lambda i, j, n_sub=n_sub: (i * n_sub + jnp.minimum(j, n_sub - 1), 0)))

    kern = functools.partial(_inproj_kernel, n_cast=len(cast), emit_w=emit_w)
    return pl.pallas_call(
        kern,
        grid=(n_i, n_j),
        in_specs=[
            pl.BlockSpec((tm, D_MODEL), lambda i, j: (i, 0)),
            pl.BlockSpec((1, D_MODEL), lambda i, j: (0, 0)),
        ] + w_specs + cast_specs,
        out_specs=[
            pl.BlockSpec((tm, tn), lambda i, j: (i, j)),
            pl.BlockSpec((tm, A_PAD), lambda i, j: (i, 0)),
        ] + cast_specs + w_out_specs,
        out_shape=[
            jax.ShapeDtypeStruct((t, N_MAIN), BF16),
            jax.ShapeDtypeStruct((t, A_PAD), BF16),
        ] + [jax.ShapeDtypeStruct(w.shape, BF16) for w in cast] + w_out_shapes,
        scratch_shapes=[pltpu.VMEM((tm, D_MODEL), BF16)],
        compiler_params=pltpu.CompilerParams(
            dimension_semantics=("arbitrary", "arbitrary"),
            vmem_limit_bytes=VMEM_LIMIT),
        name="inproj_f32w" if emit_w else "inproj",
    )(x2d, g_pre, *w_args, *cast)


def _gla_levels(c):
    levels = []
    blk = c
    while blk > GLA_SUB:
        per = blk // GLA_SUB
        levels.append((blk, [(j // per) * per + per // 2 - 1 for j in range(c // GLA_SUB)]))
        blk //= 2
    return levels


def _gla_masks(c):
    t = lax.broadcasted_iota(jnp.int32, (c, c), 0)
    s = lax.broadcasted_iota(jnp.int32, (c, c), 1)
    levels = _gla_levels(c)
    masks = []
    for blk, _ in levels:
        sh = blk.bit_length() - 1
        half = blk // 2
        same = (t >> sh) == (s >> sh)
        masks.append(same & ((t & (blk - 1)) >= half) & ((s & (blk - 1)) < half))
    sh = GLA_SUB.bit_length() - 1
    masks.append(((t >> sh) == (s >> sh)) & (s <= t))
    return masks


def _run_interleaved(*gens):
    results = [None] * len(gens)
    live = list(range(len(gens)))
    while live:
        for i in list(live):
            try:
                next(gens[i])
            except StopIteration as stop:
                results[i] = stop.value
                live.remove(i)
    return results


def _gla_chunk(q, k, vb, log_a, s_heads, tril_ones, masks):
    c = q.shape[0]
    hi, lo = _split_bf16(log_a)
    b = _dot(tril_ones, hi) + _dot(tril_ones, lo)
    n = c // GLA_SUB

    def rep(x):
        return _bcast_rows(x, list(range(n)), GLA_SUB)

    ends = jnp.concatenate([b[(j + 1) * GLA_SUB - 1:(j + 1) * GLA_SUB, :] for j in range(n)], axis=0)
    starts = jnp.concatenate([jnp.zeros((1, b.shape[1]), F32), ends[:n - 1]], axis=0)
    starts_r = rep(starts)
    q_e = q * jnp.exp(b - starts_r)
    k_e = k * jnp.exp(rep(ends) - b)

    qs, ks = [], []
    for _, ref_blocks in _gla_levels(c):
        ref = jnp.concatenate([ends[r:r + 1] for r in ref_blocks], axis=0)
        qs.append((q_e * rep(jnp.exp(jnp.minimum(starts - ref, 0.0)))).astype(BF16))
        ks.append((k_e * rep(jnp.exp(jnp.minimum(ref - ends, 0.0)))).astype(BF16))
    qs.append(q_e.astype(BF16))
    ks.append((k * jnp.exp(starts_r - b)).astype(BF16))

    q_in = (q_e * rep(jnp.exp(starts))).astype(BF16)
    k_out = k_e * rep(jnp.exp(ends[n - 1:n] - ends))
    yield

    outs, new_states = [], []
    for h in range(HB):
        kc = slice(h * DK, (h + 1) * DK)
        vc = slice(h * DV, (h + 1) * DV)
        att = jnp.zeros((c, c), F32)
        for ql, kl, m in zip(qs, ks, masks):
            att = jnp.where(m, _dot_nt(ql[:, kc], kl[:, kc]), att)
        v_h = vb[:, vc]
        s_h = s_heads[h]
        o_h = _dot(att.astype(BF16), v_h) + _dot(q_in[:, kc], s_h.astype(BF16))
        b_t = jnp.transpose(b[:, kc])
        decay = jnp.exp(jnp.broadcast_to(b_t[:, c - 1:c], (DK, DV)))
        k_t = jnp.transpose(k_out[:, kc]).astype(BF16)
        new_states.append(decay * s_h + _dot(k_t, v_h))
        outs.append(o_h)
        yield
    return jnp.concatenate(outs, axis=1), new_states


def _gla_sample_kernel(q_ref, k_ref, vb_ref, a_ref, w_a2_ref, b_a_ref, s0_ref,
                       o_ref, s1_ref, *, nb, seq):
    rows = nb * seq
    log_a = _log_decay(a_ref[...], w_a2_ref[...], b_a_ref[...])
    t = lax.broadcasted_iota(jnp.int32, (rows, rows), 0)
    s = lax.broadcasted_iota(jnp.int32, (rows, rows), 1)
    sh = seq.bit_length() - 1
    causal = ((t >> sh) == (s >> sh)) & (s <= t)
    tril_ones = jnp.where(causal, 1.0, 0.0).astype(BF16)
    hi, lo = _split_bf16(log_a)
    b = _dot(tril_ones, hi) + _dot(tril_ones, lo)

    q = q_ref[...].astype(F32) * (DK ** -0.5)
    k = k_ref[...].astype(F32)
    vb = vb_ref[...]
    q_t = q * jnp.exp(b)
    k_t = (k * jnp.exp(-b)).astype(BF16)
    b_last = _bcast_rows(b, [n * seq + seq - 1 for n in range(nb)], seq)
    k_out = k * jnp.exp(b_last - b)
    col = lax.broadcasted_iota(jnp.int32, (DK, rows), 1) >> sh

    outs = []
    for h in range(HB):
        kc = slice(h * DK, (h + 1) * DK)
        vc = slice(h * DV, (h + 1) * DV)
        v_h = vb[:, vc]
        att = jnp.where(causal, _dot_nt(q_t[:, kc].astype(BF16), k_t[:, kc]), 0.0)
        o_intra = _dot(att.astype(BF16), v_h)
        b_tr = jnp.transpose(b[:, kc])
        k_tr = jnp.transpose(k_out[:, kc])
        o_inter = []
        for n in range(nb):
            s_n = s0_ref[n, h]
            r0 = n * seq
            o_inter.append(_dot(q_t[r0:r0 + seq, kc], s_n))
            decay = jnp.exp(jnp.broadcast_to(b_tr[:, r0 + seq - 1:r0 + seq], (DK, DV)))
            k_n = jnp.where(col == n, k_tr, 0.0).astype(BF16)
            s1_ref[n, h] = decay * s_n + _dot(k_n, v_h)
        outs.append(o_intra + jnp.concatenate(o_inter, axis=0))
    o_ref[...] = jnp.concatenate(outs, axis=1)


def _gla_sample(proj, proj_a, w_a2_p, b_a, s0, nb, seq):
    n_batch = s0.shape[0]
    rows = nb * seq
    kern = functools.partial(_gla_sample_kernel, nb=nb, seq=seq)
    return pl.pallas_call(
        kern,
        grid=(n_batch // nb,),
        in_specs=[
            pl.BlockSpec((rows, K_B), lambda i: (i, OFF_Q // K_B)),
            pl.BlockSpec((rows, K_B), lambda i: (i, OFF_K // K_B)),
            pl.BlockSpec((rows, E_B), lambda i: (i, OFF_VB // E_B)),
            pl.BlockSpec((rows, A_PAD), lambda i: (i, 0)),
            pl.BlockSpec((A_PAD, K_B), lambda i: (0, 0)),
            pl.BlockSpec((1, K_B), lambda i: (0, 0)),
            pl.BlockSpec((nb, HB, DK, DV), lambda i: (i, 0, 0, 0)),
        ],
        out_specs=[
            pl.BlockSpec((rows, E_B), lambda i: (i, 0)),
            pl.BlockSpec((nb, HB, DK, DV), lambda i: (i, 0, 0, 0)),
        ],
        out_shape=[
            jax.ShapeDtypeStruct((n_batch * seq, E_B), F32),
            jax.ShapeDtypeStruct(s0.shape, F32),
        ],
        compiler_params=pltpu.CompilerParams(
            dimension_semantics=("arbitrary",),
            vmem_limit_bytes=VMEM_LIMIT),
        name="gla_sample",
    )(proj, proj, proj, proj_a, w_a2_p, b_a, s0)


def _branch_a(proj_ref, g_v_ref, w_s_ref, b_col_ref, tm, seq):
    vn = _rms(proj_ref[:, OFF_V:OFF_V + E_A].astype(F32), g_v_ref[...])
    vn_b = vn.astype(BF16)
    t = lax.broadcasted_iota(jnp.int32, (CHUNK, CHUNK), 0)
    s = lax.broadcasted_iota(jnp.int32, (CHUNK, CHUNK), 1)
    causal = s <= t
    if seq < CHUNK:
        sh = seq.bit_length() - 1
        causal = causal & ((t >> sh) == (s >> sh))
    w_sp = [jnp.where(causal, w_s_ref[g], 0.0).astype(BF16) for g in range(HA)]
    yield
    y_rows = []
    for c in range(tm // CHUNK):
        r = slice(c * CHUNK, (c + 1) * CHUNK)
        gate = jnp.concatenate(
            [_dot(w_sp[g], vn_b[r, g * DA:(g + 1) * DA]) + b_col_ref[:, g:g + 1]
             for g in range(HA)], axis=1)
        z = proj_ref[r, OFF_Z:OFF_Z + E_A].astype(F32)
        y_rows.append((proj_ref[r, OFF_U:OFF_U + E_A].astype(F32) * gate
                       * (z * _sigmoid(z))).astype(BF16))
        yield
    y_a = y_rows[0] if len(y_rows) == 1 else jnp.concatenate(y_rows, axis=0)
    return y_a, vn


def _prompt_gla(proj_ref, proj_a_ref, w_a2_ref, b_a_ref, g_o_ref, states, tm):
    log_a = _log_decay(proj_a_ref[...], w_a2_ref[...], b_a_ref[...])
    t = lax.broadcasted_iota(jnp.int32, (CHUNK, CHUNK), 0)
    s = lax.broadcasted_iota(jnp.int32, (CHUNK, CHUNK), 1)
    tril_ones = jnp.where(s <= t, 1.0, 0.0).astype(BF16)
    masks = _gla_masks(CHUNK)
    yield
    y_rows = []
    for c in range(tm // CHUNK):
        r = slice(c * CHUNK, (c + 1) * CHUNK)
        q = proj_ref[r, OFF_Q:OFF_Q + K_B].astype(F32) * (DK ** -0.5)
        k = proj_ref[r, OFF_K:OFF_K + K_B].astype(F32)
        vb = proj_ref[r, OFF_VB:OFF_VB + E_B]
        o_c, states = yield from _gla_chunk(q, k, vb, log_a[r], states, tril_ones, masks)
        y_rows.append(_branch_b_out(o_c, proj_ref[r, OFF_ZB:OFF_ZB + E_B], g_o_ref).astype(BF16))
        yield
    y_b = y_rows[0] if len(y_rows) == 1 else jnp.concatenate(y_rows, axis=0)
    return y_b, states


def _branch_b_out(o, zb, g_o_ref):
    g_o = g_o_ref[...]
    on = jnp.concatenate(
        [_rms(o[:, h * DV:(h + 1) * DV], g_o[:, h * DV:(h + 1) * DV]) for h in range(HB)], axis=1)
    zb = zb.astype(F32)
    return on * (zb * _sigmoid(zb))


BACK_BLOCKS = 4


def _back(y_a, y_b, m_a_ref, m_b_ref, x_ref, p_ref,
          w_pa_ref, w_pb_ref, w_o_ref, g_post_ref, w_pg_ref, w_pe_ref, col0=(0, 0)):
    bw = D_MODEL // BACK_BLOCKS
    blocks = [slice(j * bw, (j + 1) * bw) for j in range(BACK_BLOCKS)]
    m_parts = []
    for c in blocks:
        pa = _dot(y_a, w_pa_ref[:, c])
        yield
        pb = _dot(y_b, w_pb_ref[:, c])
        m_a = m_a_ref[:, col0[0] + c.start:col0[0] + c.stop].astype(F32)
        m_b = m_b_ref[:, col0[1] + c.start:col0[1] + c.stop].astype(F32)
        m_parts.append((_sigmoid(m_a) * pa + _sigmoid(m_b) * pb).astype(BF16))
        yield
    m = jnp.concatenate(m_parts, axis=1)
    mo_parts = []
    for c in blocks:
        mo_parts.append(_dot(m, w_o_ref[:, c]))
        yield
    h_res = x_ref[...] + _rms(jnp.concatenate(mo_parts, axis=1), g_post_ref[...])
    h_b = h_res.astype(BF16)
    p_b = p_ref[...].astype(BF16)
    yield
    y_parts = []
    for c in blocks:
        pg = _sigmoid(_dot(h_b, w_pg_ref[:, c]))
        y_parts.append(h_res[:, c] + pg * _dot(p_b, w_pe_ref[:, c]))
        yield
    return jnp.concatenate(y_parts, axis=1)


def _post_prompt_kernel(proj_ref, proj_a_ref, m_a_ref, m_b_ref, x_ref, p_ref,
                        g_v_ref, w_s_ref, b_col_ref, w_a2_ref, b_a_ref, g_o_ref,
                        w_pa_ref, w_pb_ref, w_o_ref, g_post_ref, w_pg_ref, w_pe_ref,
                        y_ref, s_out_ref, s_ref, ya_ref, yb_ref, *, tm, seq, n_tiles):
    step = pl.program_id(0)
    nt = seq // tm

    def front():
        keep = lax.rem(step, nt) != 0
        states = [jnp.where(keep, s_ref[h], 0.0) for h in range(HB)]
        y_a, _ = yield from _branch_a(proj_ref, g_v_ref, w_s_ref, b_col_ref, tm, seq)
        y_b, states = yield from _prompt_gla(proj_ref, proj_a_ref, w_a2_ref, b_a_ref, g_o_ref,
                                             states, tm)
        return y_a, y_b, states

    def back():
        return _back(ya_ref[...], yb_ref[...], m_a_ref, m_b_ref, x_ref, p_ref,
                     w_pa_ref, w_pb_ref, w_o_ref, g_post_ref, w_pg_ref, w_pe_ref)

    def store_front(y_a, y_b, states):
        ya_ref[...] = y_a
        yb_ref[...] = y_b
        for h in range(HB):
            s_ref[h] = states[h]
            s_out_ref[0, h] = states[h]

    @pl.when(step == 0)
    def _():
        store_front(*_run_interleaved(front())[0])

    @pl.when((step > 0) & (step < n_tiles))
    def _():
        y, new = _run_interleaved(back(), front())
        y_ref[...] = y
        store_front(*new)

    @pl.when(step == n_tiles)
    def _():
        y_ref[...] = _run_interleaved(back())[0]


def _post_sample_kernel(proj_ref, x_ref, p_ref, o_ref, g_v_ref, w_s_ref, b_col_ref, g_o_ref,
                        w_pa_ref, w_pb_ref, w_o_ref, g_post_ref, w_pg_ref, w_pe_ref,
                        y_ref, vn_ref, *, tm, seq):
    (y_a, vn), = _run_interleaved(_branch_a(proj_ref, g_v_ref, w_s_ref, b_col_ref, tm, seq))
    vn_ref[...] = vn
    y_b = _branch_b_out(o_ref[...], proj_ref[:, OFF_ZB:OFF_ZB + E_B], g_o_ref).astype(BF16)
    y_ref[...], = _run_interleaved(
        _back(y_a, y_b, proj_ref, proj_ref, x_ref, p_ref,
              w_pa_ref, w_pb_ref, w_o_ref, g_post_ref, w_pg_ref, w_pe_ref, col0=(OFF_MA, OFF_MB)))


def _const_spec(shape):
    nd = len(shape)
    return pl.BlockSpec(shape, lambda *_: (0,) * nd, pipeline_mode=pl.Buffered(1))


def _post_prompt(proj, proj_a, x2d, p2d, wts, n_batch, seq, tm):
    nt = seq // tm
    n_tiles = n_batch * nt
    kern = functools.partial(_post_prompt_kernel, tm=tm, seq=seq, n_tiles=n_tiles)
    cur = lambda s: (jnp.minimum(s, n_tiles - 1), 0)
    prev = lambda s: (jnp.maximum(s - 1, 0), 0)
    names = ("g_v", "w_s", "b_col", "w_a2", "b_a", "g_o", "w_pa", "w_pb", "w_o", "g_post", "w_pg", "w_pe")
    w_args = [wts[n] for n in names]
    return pl.pallas_call(
        kern,
        grid=(n_tiles + 1,),
        in_specs=[
            pl.BlockSpec((tm, OFF_MA), cur),
            pl.BlockSpec((tm, A_PAD), cur),
            pl.BlockSpec((tm, D_MODEL), lambda s: (jnp.maximum(s - 1, 0), OFF_MA // D_MODEL)),
            pl.BlockSpec((tm, D_MODEL), lambda s: (jnp.maximum(s - 1, 0), OFF_MB // D_MODEL)),
            pl.BlockSpec((tm, D_MODEL), prev),
            pl.BlockSpec((tm, PLE_DIM), prev),
        ] + [_const_spec(w.shape) for w in w_args],
        out_specs=[
            pl.BlockSpec((tm, D_MODEL), prev),
            pl.BlockSpec((1, HB, DK, DV), lambda s: (jnp.minimum(s, n_tiles - 1) // nt, 0, 0, 0)),
        ],
        out_shape=[
            jax.ShapeDtypeStruct((n_batch * seq, D_MODEL), F32),
            jax.ShapeDtypeStruct((n_batch, HB, DK, DV), F32),
        ],
        scratch_shapes=[
            pltpu.VMEM((HB, DK, DV), F32),
            pltpu.VMEM((tm, E_A), BF16),
            pltpu.VMEM((tm, E_B), BF16),
        ],
        compiler_params=pltpu.CompilerParams(
            dimension_semantics=("arbitrary",),
            vmem_limit_bytes=VMEM_LIMIT),
        name="post_prompt",
    )(proj, proj_a, proj, proj, x2d, p2d, *w_args)


def _post_sample(proj, x2d, p2d, o_gla, wts, seq, tm):
    t = x2d.shape[0]
    kern = functools.partial(_post_sample_kernel, tm=tm, seq=seq)
    row = lambda i: (i, 0)
    names = ("g_v", "w_s_bd", "b_col_bd", "g_o", "w_pa", "w_pb", "w_o", "g_post", "w_pg", "w_pe")
    w_args = [wts[n] for n in names]
    return pl.pallas_call(
        kern,
        grid=(t // tm,),
        in_specs=[
            pl.BlockSpec((tm, N_MAIN), row),
            pl.BlockSpec((tm, D_MODEL), row),
            pl.BlockSpec((tm, PLE_DIM), row),
            pl.BlockSpec((tm, E_B), row),
        ] + [_const_spec(w.shape) for w in w_args],
        out_specs=[
            pl.BlockSpec((tm, D_MODEL), row),
            pl.BlockSpec((tm, E_A), row),
        ],
        out_shape=[
            jax.ShapeDtypeStruct((t, D_MODEL), F32),
            jax.ShapeDtypeStruct((t, E_A), F32),
        ],
        compiler_params=pltpu.CompilerParams(
            dimension_semantics=("arbitrary",),
            vmem_limit_bytes=VMEM_LIMIT),
        name="post_sample",
    )(proj, x2d, p2d, o_gla, *w_args)


def _layer_weights(i, dec_seq, g_pre, w_in, g_v, w_s, b_s, w_a2, b_a, g_o,
                   w_pa, w_pb, w_o, g_post, w_pg, w_pe):
    w_a2_p = jnp.concatenate(
        [w_a2[i], jnp.zeros((A_PAD - GATE_RANK, K_B), w_a2.dtype)], axis=0).astype(BF16)
    reps = CHUNK // dec_seq
    return dict(
        g_pre=g_pre[i][None, :],
        w_in_t=w_in[i].T,
        g_v=g_v[i][None, :],
        w_s=w_s[i],
        b_col=b_s[i].T,
        w_s_bd=jnp.tile(w_s[i][:, :dec_seq, :dec_seq], (1, reps, reps)),
        b_col_bd=jnp.tile(b_s[i][:, :dec_seq].T, (reps, 1)),
        w_a2=w_a2_p,
        b_a=b_a[i][None, :],
        g_o=g_o[i].reshape(1, E_B),
        g_post=g_post[i][None, :],
    )


SIDE_WEIGHTS = ("w_pa", "w_pb", "w_o", "w_pg", "w_pe")


def kernel(x_prompt, x_sample, p_prompt, p_sample, state_gla, g_pre, w_in, g_v, w_s, b_s,
           w_a2, b_a, g_o, w_pa, w_pb, w_o, g_post, w_pg, w_pe):
    depth = w_in.shape[0]
    n_batch, seq, _ = x_prompt.shape
    dec_batch, dec_seq, _ = x_sample.shape
    hp = x_prompt.reshape(n_batch * seq, D_MODEL)
    hs = x_sample.reshape(dec_batch * dec_seq, D_MODEL)
    sp_list, ss_list, cv_list = [], [], []
    for i in range(depth):
        wts = _layer_weights(i, dec_seq, g_pre, w_in, g_v, w_s, b_s, w_a2, b_a, g_o,
                             w_pa, w_pb, w_o, g_post, w_pg, w_pe)
        side_f32 = dict(w_pa=w_pa[i], w_pb=w_pb[i], w_o=w_o[i], w_pg=w_pg[i], w_pe=w_pe[i])
        proj_s, proj_a_s, *w_bf = _inproj(hs, wts["g_pre"], wts["w_in_t"], tm=1024)
        proj_p, proj_a_p, *side = _inproj(hp, wts["g_pre"], wts["w_in_t"], tm=1024, w_bf=w_bf,
                                          cast=[side_f32[n] for n in SIDE_WEIGHTS])
        wts.update(zip(SIDE_WEIGHTS, side))
        o_s, s_s = _gla_sample(proj_s, proj_a_s, wts["w_a2"], wts["b_a"], state_gla[i],
                               nb=CHUNK // dec_seq, seq=dec_seq)
        hp, s_p = _post_prompt(proj_p, proj_a_p, hp, p_prompt[i].reshape(n_batch * seq, PLE_DIM),
                               wts, n_batch, seq, tm=256)
        hs, v_s = _post_sample(proj_s, hs, p_sample[i].reshape(dec_batch * dec_seq, PLE_DIM),
                               o_s, wts, dec_seq, tm=256)
        sp_list.append(s_p)
        ss_list.append(s_s)
        cv_list.append(v_s.reshape(dec_batch, dec_seq, E_A))
    stack = (lambda xs: xs[0][None]) if depth == 1 else jnp.stack
    return (hp.reshape(n_batch, seq, D_MODEL),
            hs.reshape(dec_batch, dec_seq, D_MODEL),
            stack(sp_list), stack(ss_list), stack(cv_list))
```

```python
import functools

import jax
import jax.numpy as jnp
from jax import lax
from jax.experimental import pallas as pl
from jax.experimental.pallas import tpu as pltpu

D_MODEL = 2048
E_A = D_MODEL // 2
HA = 4
DA = E_A // HA
CHUNK = 128
HB = 4
E_B = D_MODEL // 2
DV = E_B // HB
DK = DV // 2
K_B = HB * DK
GATE_RANK = 16
GATE_TAU = 16.0
GLA_SUB = 16
PLE_DIM = 256
EPS = 1e-6

LANES = 128
OFF_U = 0
OFF_V = OFF_U + E_A
OFF_Z = OFF_V + E_A
OFF_Q = OFF_Z + E_A
OFF_K = OFF_Q + K_B
OFF_VB = OFF_K + K_B
OFF_ZB = OFF_VB + E_B
OFF_MA = OFF_ZB + E_B
OFF_MB = OFF_MA + D_MODEL
N_MAIN = OFF_MB + D_MODEL
W_IN_A = OFF_MA
A_PAD = LANES
INPROJ_TN = 1024
INPROJ_TN_BF16 = 2048

F32 = jnp.float32
BF16 = jnp.bfloat16

VMEM_LIMIT = 56 * 1024 * 1024


def _dot(a, b):
    return jnp.dot(a, b, preferred_element_type=F32)


def _dot_nt(a, b):
    return lax.dot_general(a, b, (((1,), (1,)), ((), ())), preferred_element_type=F32)


def _sigmoid(x):
    return 1.0 / (1.0 + jnp.exp(-x))


def _rms(x, g):
    return x * lax.rsqrt(jnp.mean(x * x, axis=-1, keepdims=True) + EPS) * g


def _split_bf16(x):
    hi = x.astype(BF16)
    lo = (x - hi.astype(F32)).astype(BF16)
    return hi, lo


def _log_decay(a_lr, w_a2, b_a):
    pre = _dot(a_lr, w_a2) + b_a
    return (jnp.minimum(pre, 0.0) - jnp.log(1.0 + jnp.exp(-jnp.abs(pre)))) * (1.0 / GATE_TAU)


def _bcast_rows(b, idxs, seg):
    parts = []
    for i in idxs:
        if i < 0:
            parts.append(jnp.zeros((seg, b.shape[1]), b.dtype))
        else:
            parts.append(jnp.broadcast_to(b[i:i + 1, :], (seg, b.shape[1])))
    return parts[0] if len(parts) == 1 else jnp.concatenate(parts, axis=0)


BF16_SUBLANES = 16


def _inproj_kernel(*refs, n_cast, emit_w):
    x_ref, g_ref, w_ref, w_a_ref = refs[:4]
    cast_in = refs[4:4 + n_cast]
    o_ref, o_a_ref = refs[4 + n_cast:6 + n_cast]
    cast_out = refs[6 + n_cast:6 + 2 * n_cast]
    rest = refs[6 + 2 * n_cast:]
    xn_ref = rest[-1]

    @pl.when(pl.program_id(1) == 0)
    def _():
        xn = _rms(x_ref[...], g_ref[...]).astype(BF16)
        xn_ref[...] = xn
        w_a = w_a_ref[...].astype(BF16)
        if emit_w:
            rest[1][...] = w_a
        o_a_ref[...] = _dot_nt(xn, w_a).astype(o_a_ref.dtype)

    for src, dst in zip(cast_in, cast_out):
        dst[...] = src[...].astype(dst.dtype)
    w = w_ref[...].astype(BF16)
    if emit_w:
        rest[0][...] = w
    o_ref[...] = _dot_nt(xn_ref[...], w).astype(o_ref.dtype)


def _inproj(x2d, g_pre, w_in_t, tm, w_bf=None, cast=()):
    t = x2d.shape[0]
    emit_w = w_bf is None
    tn = INPROJ_TN if emit_w else INPROJ_TN_BF16
    n_i, n_j = t // tm, N_MAIN // tn

    def w_rows(i, j):
        row0 = j * tn + jnp.where(j * tn >= OFF_MA, GATE_RANK, 0)
        return (pl.multiple_of(row0, GATE_RANK), 0)

    if emit_w:
        w_args = (w_in_t, w_in_t)
        w_specs = [pl.BlockSpec((pl.Element(tn), pl.Element(D_MODEL)), w_rows),
                   pl.BlockSpec((A_PAD, D_MODEL), lambda i, j: (W_IN_A // A_PAD, 0))]
        w_out_specs = [pl.BlockSpec((tn, D_MODEL), lambda i, j: (j, 0)),
                       pl.BlockSpec((A_PAD, D_MODEL), lambda i, j: (0, 0))]
        w_out_shapes = [jax.ShapeDtypeStruct((N_MAIN, D_MODEL), BF16),
                        jax.ShapeDtypeStruct((A_PAD, D_MODEL), BF16)]
        assert n_i == 1, "the bf16 weight copy is written once, by a single token tile"
    else:
        w_args = w_bf
        w_specs = [pl.BlockSpec((tn, D_MODEL), lambda i, j: (j, 0)),
                   pl.BlockSpec((A_PAD, D_MODEL), lambda i, j: (0, 0))]
        w_out_specs, w_out_shapes = [], []

    cast_specs = []
    for w in cast:
        per_i = w.shape[0] // n_i
        assert per_i * n_i == w.shape[0] and per_i % BF16_SUBLANES == 0
        n_sub = max(n for n in range(1, n_j + 1)
                    if per_i % n == 0 and (per_i // n) % BF16_SUBLANES == 0)
        cast_specs.append(pl.BlockSpec(
            (per_i // n_sub, w.shape[1]),
            lambda i, j, n_sub=n_sub: (i * n_sub + jnp.minimum(j, n_sub - 1), 0)))

    kern = functools.partial(_inproj_kernel, n_cast=len(cast), emit_w=emit_w)
    return pl.pallas_call(
        kern,
        grid=(n_i, n_j),
        in_specs=[
            pl.BlockSpec((tm, D_MODEL), lambda i, j: (i, 0)),
            pl.BlockSpec((1, D_MODEL), lambda i, j: (0, 0)),
        ] + w_specs + cast_specs,
        out_specs=[
            pl.BlockSpec((tm, tn), lambda i, j: (i, j)),
            pl.BlockSpec((tm, A_PAD), lambda i, j: (i, 0)),
        ] + cast_specs + w_out_specs,
        out_shape=[
            jax.ShapeDtypeStruct((t, N_MAIN), BF16),
            jax.ShapeDtypeStruct((t, A_PAD), BF16),
        ] + [jax.ShapeDtypeStruct(w.shape, BF16) for w in cast] + w_out_shapes,
        scratch_shapes=[pltpu.VMEM((tm, D_MODEL), BF16)],
        compiler_params=pltpu.CompilerParams(
            dimension_semantics=("arbitrary", "arbitrary"),
            vmem_limit_bytes=VMEM_LIMIT),
        name="inproj_f32w" if emit_w else "inproj",
    )(x2d, g_pre, *w_args, *cast)


def _gla_levels(c):
    levels = []
    blk = c
    while blk > GLA_SUB:
        per = blk // GLA_SUB
        levels.append((blk, [(j // per) * per + per // 2 - 1 for j in range(c // GLA_SUB)]))
        blk //= 2
    return levels


def _gla_masks(c):
    t = lax.broadcasted_iota(jnp.int32, (c, c), 0)
    s = lax.broadcasted_iota(jnp.int32, (c, c), 1)
    levels = _gla_levels(c)
    masks = []
    for blk, _ in levels:
        sh = blk.bit_length() - 1
        half = blk // 2
        same = (t >> sh) == (s >> sh)
        masks.append(same & ((t & (blk - 1)) >= half) & ((s & (blk - 1)) < half))
    sh = GLA_SUB.bit_length() - 1
    masks.append(((t >> sh) == (s >> sh)) & (s <= t))
    return masks


def _run_interleaved(*gens):
    results = [None] * len(gens)
    live = list(range(len(gens)))
    while live:
        for i in list(live):
            try:
                next(gens[i])
            except StopIteration as stop:
                results[i] = stop.value
                live.remove(i)
    return results


def _gla_chunk(q, k, vb, log_a, s_heads, tril_ones, masks):
    c = q.shape[0]
    hi, lo = _split_bf16(log_a)
    b = _dot(tril_ones, hi) + _dot(tril_ones, lo)
    n = c // GLA_SUB

    def rep(x):
        return _bcast_rows(x, list(range(n)), GLA_SUB)

    ends = jnp.concatenate([b[(j + 1) * GLA_SUB - 1:(j + 1) * GLA_SUB, :] for j in range(n)], axis=0)
    starts = jnp.concatenate([jnp.zeros((1, b.shape[1]), F32), ends[:n - 1]], axis=0)
    starts_r = rep(starts)
    q_e = q * jnp.exp(b - starts_r)
    k_e = k * jnp.exp(rep(ends) - b)

    qs, ks = [], []
    for _, ref_blocks in _gla_levels(c):
        ref = jnp.concatenate([ends[r:r + 1] for r in ref_blocks], axis=0)
        qs.append((q_e * rep(jnp.exp(jnp.minimum(starts - ref, 0.0)))).astype(BF16))
        ks.append((k_e * rep(jnp.exp(jnp.minimum(ref - ends, 0.0)))).astype(BF16))
    qs.append(q_e.astype(BF16))
    ks.append((k * jnp.exp(starts_r - b)).astype(BF16))

    q_in = (q_e * rep(jnp.exp(starts))).astype(BF16)
    k_out = k_e * rep(jnp.exp(ends[n - 1:n] - ends))
    yield

    outs, new_states = [], []
    for h in range(HB):
        kc = slice(h * DK, (h + 1) * DK)
        vc = slice(h * DV, (h + 1) * DV)
        att = jnp.zeros((c, c), F32)
        for ql, kl, m in zip(qs, ks, masks):
            att = jnp.where(m, _dot_nt(ql[:, kc], kl[:, kc]), att)
        v_h = vb[:, vc]
        s_h = s_heads[h]
        o_h = _dot(att.astype(BF16), v_h) + _dot(q_in[:, kc], s_h.astype(BF16))
        b_t = jnp.transpose(b[:, kc])
        decay = jnp.exp(jnp.broadcast_to(b_t[:, c - 1:c], (DK, DV)))
        k_t = jnp.transpose(k_out[:, kc]).astype(BF16)
        new_states.append(decay * s_h + _dot(k_t, v_h))
        outs.append(o_h)
        yield
    return jnp.concatenate(outs, axis=1), new_states


def _gla_sample_kernel(q_ref, k_ref, vb_ref, a_ref, w_a2_ref, b_a_ref, s0_ref,
                       o_ref, s1_ref, *, nb, seq):
    rows = nb * seq
    log_a = _log_decay(a_ref[...], w_a2_ref[...], b_a_ref[...])
    t = lax.broadcasted_iota(jnp.int32, (rows, rows), 0)
    s = lax.broadcasted_iota(jnp.int32, (rows, rows), 1)
    sh = seq.bit_length() - 1
    causal = ((t >> sh) == (s >> sh)) & (s <= t)
    tril_ones = jnp.where(causal, 1.0, 0.0).astype(BF16)
    hi, lo = _split_bf16(log_a)
    b = _dot(tril_ones, hi) + _dot(tril_ones, lo)

    q = q_ref[...].astype(F32) * (DK ** -0.5)
    k = k_ref[...].astype(F32)
    vb = vb_ref[...]
    q_t = q * jnp.exp(b)
    k_t = (k * jnp.exp(-b)).astype(BF16)
    b_last = _bcast_rows(b, [n * seq + seq - 1 for n in range(nb)], seq)
    k_out = k * jnp.exp(b_last - b)
    col = lax.broadcasted_iota(jnp.int32, (DK, rows), 1) >> sh

    outs = []
    for h in range(HB):
        kc = slice(h * DK, (h + 1) * DK)
        vc = slice(h * DV, (h + 1) * DV)
        v_h = vb[:, vc]
        att = jnp.where(causal, _dot_nt(q_t[:, kc].astype(BF16), k_t[:, kc]), 0.0)
        o_intra = _dot(att.astype(BF16), v_h)
        b_tr = jnp.transpose(b[:, kc])
        k_tr = jnp.transpose(k_out[:, kc])
        o_inter = []
        for n in range(nb):
            s_n = s0_ref[n, h]
            r0 = n * seq
            o_inter.append(_dot(q_t[r0:r0 + seq, kc], s_n))
            decay = jnp.exp(jnp.broadcast_to(b_tr[:, r0 + seq - 1:r0 + seq], (DK, DV)))
            k_n = jnp.where(col == n, k_tr, 0.0).astype(BF16)
            s1_ref[n, h] = decay * s_n + _dot(k_n, v_h)
        outs.append(o_intra + jnp.concatenate(o_inter, axis=0))
    o_ref[...] = jnp.concatenate(outs, axis=1)


def _gla_sample(proj, proj_a, w_a2_p, b_a, s0, nb, seq):
    n_batch = s0.shape[0]
    rows = nb * seq
    kern = functools.partial(_gla_sample_kernel, nb=nb, seq=seq)
    return pl.pallas_call(
        kern,
        grid=(n_batch // nb,),
        in_specs=[
            pl.BlockSpec((rows, K_B), lambda i: (i, OFF_Q // K_B)),
            pl.BlockSpec((rows, K_B), lambda i: (i, OFF_K // K_B)),
            pl.BlockSpec((rows, E_B), lambda i: (i, OFF_VB // E_B)),
            pl.BlockSpec((rows, A_PAD), lambda i: (i, 0)),
            pl.BlockSpec((A_PAD, K_B), lambda i: (0, 0)),
            pl.BlockSpec((1, K_B), lambda i: (0, 0)),
            pl.BlockSpec((nb, HB, DK, DV), lambda i: (i, 0, 0, 0)),
        ],
        out_specs=[
            pl.BlockSpec((rows, E_B), lambda i: (i, 0)),
            pl.BlockSpec((nb, HB, DK, DV), lambda i: (i, 0, 0, 0)),
        ],
        out_shape=[
            jax.ShapeDtypeStruct((n_batch * seq, E_B), F32),
            jax.ShapeDtypeStruct(s0.shape, F32),
        ],
        compiler_params=pltpu.CompilerParams(
            dimension_semantics=("arbitrary",),
            vmem_limit_bytes=VMEM_LIMIT),
        name="gla_sample",
    )(proj, proj, proj, proj_a, w_a2_p, b_a, s0)


def _branch_a(proj_ref, g_v_ref, w_s_ref, b_col_ref, tm, seq):
    vn = _rms(proj_ref[:, OFF_V:OFF_V + E_A].astype(F32), g_v_ref[...])
    vn_b = vn.astype(BF16)
    t = lax.broadcasted_iota(jnp.int32, (CHUNK, CHUNK), 0)
    s = lax.broadcasted_iota(jnp.int32, (CHUNK, CHUNK), 1)
    causal = s <= t
    if seq < CHUNK:
        sh = seq.bit_length() - 1
        causal = causal & ((t >> sh) == (s >> sh))
    w_sp = [jnp.where(causal, w_s_ref[g], 0.0).astype(BF16) for g in range(HA)]
    yield
    y_rows = []
    for c in range(tm // CHUNK):
        r = slice(c * CHUNK, (c + 1) * CHUNK)
        gate = jnp.concatenate(
            [_dot(w_sp[g], vn_b[r, g * DA:(g + 1) * DA]) + b_col_ref[:, g:g + 1]
             for g in range(HA)], axis=1)
        z = proj_ref[r, OFF_Z:OFF_Z + E_A].astype(F32)
        y_rows.append((proj_ref[r, OFF_U:OFF_U + E_A].astype(F32) * gate
                       * (z * _sigmoid(z))).astype(BF16))
        yield
    y_a = y_rows[0] if len(y_rows) == 1 else jnp.concatenate(y_rows, axis=0)
    return y_a, vn


def _prompt_gla(proj_ref, proj_a_ref, w_a2_ref, b_a_ref, g_o_ref, states, tm):
    log_a = _log_decay(proj_a_ref[...], w_a2_ref[...], b_a_ref[...])
    t = lax.broadcasted_iota(jnp.int32, (CHUNK, CHUNK), 0)
    s = lax.broadcasted_iota(jnp.int32, (CHUNK, CHUNK), 1)
    tril_ones = jnp.where(s <= t, 1.0, 0.0).astype(BF16)
    masks = _gla_masks(CHUNK)
    yield
    y_rows = []
    for c in range(tm // CHUNK):
        r = slice(c * CHUNK, (c + 1) * CHUNK)
        q = proj_ref[r, OFF_Q:OFF_Q + K_B].astype(F32) * (DK ** -0.5)
        k = proj_ref[r, OFF_K:OFF_K + K_B].astype(F32)
        vb = proj_ref[r, OFF_VB:OFF_VB + E_B]
        o_c, states = yield from _gla_chunk(q, k, vb, log_a[r], states, tril_ones, masks)
        y_rows.append(_branch_b_out(o_c, proj_ref[r, OFF_ZB:OFF_ZB + E_B], g_o_ref).astype(BF16))
        yield
    y_b = y_rows[0] if len(y_rows) == 1 else jnp.concatenate(y_rows, axis=0)
    return y_b, states


def _branch_b_out(o, zb, g_o_ref):
    g_o = g_o_ref[...]
    on = jnp.concatenate(
        [_rms(o[:, h * DV:(h + 1) * DV], g_o[:, h * DV:(h + 1) * DV]) for h in range(HB)], axis=1)
    zb = zb.astype(F32)
    return on * (zb * _sigmoid(zb))


COL_BLOCKS = 4
COL_BLOCK = D_MODEL // COL_BLOCKS


def _merge(y_a, y_b, proj_ref, w_pa_ref, w_pb_ref, m_ref):
    for j in range(COL_BLOCKS):
        c = slice(j * COL_BLOCK, (j + 1) * COL_BLOCK)
        m_a = proj_ref[:, OFF_MA + c.start:OFF_MA + c.stop].astype(F32)
        m_b = proj_ref[:, OFF_MB + c.start:OFF_MB + c.stop].astype(F32)
        m_ref[:, c] = (_sigmoid(m_a) * _dot(y_a, w_pa_ref[:, c])
                       + _sigmoid(m_b) * _dot(y_b, w_pb_ref[:, c])).astype(m_ref.dtype)


def _mix_prompt_kernel(proj_ref, proj_a_ref, g_v_ref, w_s_ref, b_col_ref, w_a2_ref, b_a_ref,
                       g_o_ref, w_pa_ref, w_pb_ref, m_ref, s_out_ref, s_ref, *, tm, seq):
    keep = pl.program_id(1) != 0
    states = [jnp.where(keep, s_ref[h], 0.0) for h in range(HB)]
    (y_a, _), = _run_interleaved(_branch_a(proj_ref, g_v_ref, w_s_ref, b_col_ref, tm, seq))
    (y_b, states), = _run_interleaved(
        _prompt_gla(proj_ref, proj_a_ref, w_a2_ref, b_a_ref, g_o_ref, states, tm))
    for h in range(HB):
        s_ref[h] = states[h]
        s_out_ref[0, h] = states[h]
    _merge(y_a, y_b, proj_ref, w_pa_ref, w_pb_ref, m_ref)


def _mix_sample_kernel(proj_ref, o_ref, g_v_ref, w_s_ref, b_col_ref, g_o_ref, w_pa_ref, w_pb_ref,
                       m_ref, vn_ref, *, tm, seq):
    (y_a, vn), = _run_interleaved(_branch_a(proj_ref, g_v_ref, w_s_ref, b_col_ref, tm, seq))
    vn_ref[...] = vn
    y_b = _branch_b_out(o_ref[...], proj_ref[:, OFF_ZB:OFF_ZB + E_B], g_o_ref).astype(BF16)
    _merge(y_a, y_b, proj_ref, w_pa_ref, w_pb_ref, m_ref)


def _out_kernel(m_ref, x_ref, p_ref, w_o_ref, g_post_ref, w_pg_ref, w_pe_ref, y_ref):
    m = m_ref[...]
    blocks = [slice(j * COL_BLOCK, (j + 1) * COL_BLOCK) for j in range(COL_BLOCKS)]
    mo = jnp.concatenate([_dot(m, w_o_ref[:, c]) for c in blocks], axis=1)
    h_res = x_ref[...] + _rms(mo, g_post_ref[...])
    h_b = h_res.astype(BF16)
    p_b = p_ref[...].astype(BF16)
    for c in blocks:
        pg = _sigmoid(_dot(h_b, w_pg_ref[:, c]))
        y_ref[:, c] = h_res[:, c] + pg * _dot(p_b, w_pe_ref[:, c])


def _const_spec(shape):
    nd = len(shape)
    return pl.BlockSpec(shape, lambda *_: (0,) * nd, pipeline_mode=pl.Buffered(1))


def _mix_prompt(proj, proj_a, wts, n_batch, seq, tm):
    nt = seq // tm
    kern = functools.partial(_mix_prompt_kernel, tm=tm, seq=seq)
    row = lambda b, i: (b * nt + i, 0)
    names = ("g_v", "w_s", "b_col", "w_a2", "b_a", "g_o", "w_pa", "w_pb")
    w_args = [wts[n] for n in names]
    return pl.pallas_call(
        kern,
        grid=(n_batch, nt),
        in_specs=[
            pl.BlockSpec((tm, N_MAIN), row),
            pl.BlockSpec((tm, A_PAD), row),
        ] + [_const_spec(w.shape) for w in w_args],
        out_specs=[
            pl.BlockSpec((tm, D_MODEL), row),
            pl.BlockSpec((1, HB, DK, DV), lambda b, i: (b, 0, 0, 0)),
        ],
        out_shape=[
            jax.ShapeDtypeStruct((n_batch * seq, D_MODEL), BF16),
            jax.ShapeDtypeStruct((n_batch, HB, DK, DV), F32),
        ],
        scratch_shapes=[pltpu.VMEM((HB, DK, DV), F32)],
        compiler_params=pltpu.CompilerParams(
            dimension_semantics=("arbitrary", "arbitrary"),
            vmem_limit_bytes=VMEM_LIMIT),
        name="mix_prompt",
    )(proj, proj_a, *w_args)


def _mix_sample(proj, o_gla, wts, seq, tm):
    t = proj.shape[0]
    kern = functools.partial(_mix_sample_kernel, tm=tm, seq=seq)
    row = lambda i: (i, 0)
    names = ("g_v", "w_s_bd", "b_col_bd", "g_o", "w_pa", "w_pb")
    w_args = [wts[n] for n in names]
    return pl.pallas_call(
        kern,
        grid=(t // tm,),
        in_specs=[
            pl.BlockSpec((tm, N_MAIN), row),
            pl.BlockSpec((tm, E_B), row),
        ] + [_const_spec(w.shape) for w in w_args],
        out_specs=[
            pl.BlockSpec((tm, D_MODEL), row),
            pl.BlockSpec((tm, E_A), row),
        ],
        out_shape=[
            jax.ShapeDtypeStruct((t, D_MODEL), BF16),
            jax.ShapeDtypeStruct((t, E_A), F32),
        ],
        compiler_params=pltpu.CompilerParams(
            dimension_semantics=("arbitrary",),
            vmem_limit_bytes=VMEM_LIMIT),
        name="mix_sample",
    )(proj, o_gla, *w_args)


def _out_proj(m, x2d, p2d, wts, tm):
    t = x2d.shape[0]
    row = lambda i: (i, 0)
    names = ("w_o", "g_post", "w_pg", "w_pe")
    w_args = [wts[n] for n in names]
    return pl.pallas_call(
        _out_kernel,
        grid=(t // tm,),
        in_specs=[
            pl.BlockSpec((tm, D_MODEL), row),
            pl.BlockSpec((tm, D_MODEL), row),
            pl.BlockSpec((tm, PLE_DIM), row),
        ] + [_const_spec(w.shape) for w in w_args],
        out_specs=pl.BlockSpec((tm, D_MODEL), row),
        out_shape=jax.ShapeDtypeStruct((t, D_MODEL), F32),
        compiler_params=pltpu.CompilerParams(
            dimension_semantics=("arbitrary",),
            vmem_limit_bytes=VMEM_LIMIT),
        name="out_proj",
    )(m, x2d, p2d, *w_args)


def _layer_weights(i, dec_seq, g_pre, w_in, g_v, w_s, b_s, w_a2, b_a, g_o,
                   w_pa, w_pb, w_o, g_post, w_pg, w_pe):
    w_a2_p = jnp.concatenate(
        [w_a2[i], jnp.zeros((A_PAD - GATE_RANK, K_B), w_a2.dtype)], axis=0).astype(BF16)
    reps = CHUNK // dec_seq
    return dict(
        g_pre=g_pre[i][None, :],
        w_in_t=w_in[i].T,
        g_v=g_v[i][None, :],
        w_s=w_s[i],
        b_col=b_s[i].T,
        w_s_bd=jnp.tile(w_s[i][:, :dec_seq, :dec_seq], (1, reps, reps)),
        b_col_bd=jnp.tile(b_s[i][:, :dec_seq].T, (reps, 1)),
        w_a2=w_a2_p,
        b_a=b_a[i][None, :],
        g_o=g_o[i].reshape(1, E_B),
        g_post=g_post[i][None, :],
    )


SIDE_WEIGHTS = ("w_pa", "w_pb", "w_o", "w_pg", "w_pe")


def kernel(x_prompt, x_sample, p_prompt, p_sample, state_gla, g_pre, w_in, g_v, w_s, b_s,
           w_a2, b_a, g_o, w_pa, w_pb, w_o, g_post, w_pg, w_pe):
    depth = w_in.shape[0]
    n_batch, seq, _ = x_prompt.shape
    dec_batch, dec_seq, _ = x_sample.shape
    hp = x_prompt.reshape(n_batch * seq, D_MODEL)
    hs = x_sample.reshape(dec_batch * dec_seq, D_MODEL)
    sp_list, ss_list, cv_list = [], [], []
    for i in range(depth):
        wts = _layer_weights(i, dec_seq, g_pre, w_in, g_v, w_s, b_s, w_a2, b_a, g_o,
                             w_pa, w_pb, w_o, g_post, w_pg, w_pe)
        side_f32 = dict(w_pa=w_pa[i], w_pb=w_pb[i], w_o=w_o[i], w_pg=w_pg[i], w_pe=w_pe[i])
        proj_s, proj_a_s, *w_bf = _inproj(hs, wts["g_pre"], wts["w_in_t"], tm=1024)
        proj_p, proj_a_p, *side = _inproj(hp, wts["g_pre"], wts["w_in_t"], tm=1024, w_bf=w_bf,
                                          cast=[side_f32[n] for n in SIDE_WEIGHTS])
        wts.update(zip(SIDE_WEIGHTS, side))
        o_s, s_s = _gla_sample(proj_s, proj_a_s, wts["w_a2"], wts["b_a"], state_gla[i],
                               nb=CHUNK // dec_seq, seq=dec_seq)
        m_p, s_p = _mix_prompt(proj_p, proj_a_p, wts, n_batch, seq, tm=512)
        m_s, v_s = _mix_sample(proj_s, o_s, wts, dec_seq, tm=512)
        hp = _out_proj(m_p, hp, p_prompt[i].reshape(n_batch * seq, PLE_DIM), wts, tm=512)
        hs = _out_proj(m_s, hs, p_sample[i].reshape(dec_batch * dec_seq, PLE_DIM), wts, tm=512)
        sp_list.append(s_p)
        ss_list.append(s_s)
        cv_list.append(v_s.reshape(dec_batch, dec_seq, E_A))
    stack = (lambda xs: xs[0][None]) if depth == 1 else jnp.stack
    return (hp.reshape(n_batch, seq, D_MODEL),
            hs.reshape(dec_batch, dec_seq, D_MODEL),
            stack(sp_list), stack(ss_list), stack(cv_list))
```

```python
import functools

import jax
import jax.numpy as jnp
from jax import lax
from jax.experimental import pallas as pl
from jax.experimental.pallas import tpu as pltpu

D_MODEL = 2048
E_A = D_MODEL // 2
HA = 4
DA = E_A // HA
CHUNK = 128
HB = 4
E_B = D_MODEL // 2
DV = E_B // HB
DK = DV // 2
K_B = HB * DK
GATE_RANK = 16
GATE_TAU = 16.0
GLA_SUB = 16
PLE_DIM = 256
EPS = 1e-6

LANES = 128
OFF_U = 0
OFF_V = OFF_U + E_A
OFF_Z = OFF_V + E_A
OFF_Q = OFF_Z + E_A
OFF_K = OFF_Q + K_B
OFF_VB = OFF_K + K_B
OFF_ZB = OFF_VB + E_B
OFF_MA = OFF_ZB + E_B
OFF_MB = OFF_MA + D_MODEL
N_MAIN = OFF_MB + D_MODEL
W_IN_A = OFF_MA
A_PAD = LANES
INPROJ_TN = 1024
INPROJ_TN_BF16 = 2048

F32 = jnp.float32
BF16 = jnp.bfloat16

VMEM_LIMIT = 60 * 1024 * 1024


def _dot(a, b):
    return jnp.dot(a, b, preferred_element_type=F32)


def _dot_nt(a, b):
    return lax.dot_general(a, b, (((1,), (1,)), ((), ())), preferred_element_type=F32)


def _sigmoid(x):
    return 1.0 / (1.0 + jnp.exp(-x))


def _rms(x, g):
    return x * lax.rsqrt(jnp.mean(x * x, axis=-1, keepdims=True) + EPS) * g


def _split_bf16(x):
    hi = x.astype(BF16)
    lo = (x - hi.astype(F32)).astype(BF16)
    return hi, lo


def _log_decay(a_lr, w_a2, b_a):
    pre = _dot(a_lr, w_a2) + b_a
    return (jnp.minimum(pre, 0.0) - jnp.log(1.0 + jnp.exp(-jnp.abs(pre)))) * (1.0 / GATE_TAU)


def _bcast_rows(b, idxs, seg):
    parts = []
    for i in idxs:
        if i < 0:
            parts.append(jnp.zeros((seg, b.shape[1]), b.dtype))
        else:
            parts.append(jnp.broadcast_to(b[i:i + 1, :], (seg, b.shape[1])))
    return parts[0] if len(parts) == 1 else jnp.concatenate(parts, axis=0)


BF16_SUBLANES = 16


_GROUPS = ((OFF_U, E_A, None), (OFF_V, E_A, "rms"), (OFF_Z, E_A, "silu"), (OFF_Q, K_B, "qscale"),
           (OFF_K, K_B, None), (OFF_VB, E_B, None), (OFF_ZB, E_B, "silu"),
           (OFF_MA, D_MODEL, "sigmoid"), (OFF_MB, D_MODEL, "sigmoid"))


def _activate(acc, col0, g_v_ref, vn_ref):
    tn = acc.shape[1]
    parts = []
    for off, width, kind in _GROUPS:
        lo, hi = max(off, col0), min(off + width, col0 + tn)
        if lo >= hi:
            continue
        a = acc[:, lo - col0:hi - col0]
        if kind == "rms":
            assert (lo, hi) == (off, off + width), "a norm group must sit inside one tile"
            a = _rms(a, g_v_ref[...])
            if vn_ref is not None:
                vn_ref[...] = a
        elif kind == "silu":
            a = a * _sigmoid(a)
        elif kind == "sigmoid":
            a = _sigmoid(a)
        elif kind == "qscale":
            a = a * (DK ** -0.5)
        parts.append(a.astype(BF16))
    return parts[0] if len(parts) == 1 else jnp.concatenate(parts, axis=1)


def _inproj_kernel(*refs, n_cast, emit_w, n_j):
    x_ref, g_ref, g_v_ref, w_ref, w_a_ref = refs[:5]
    cast_in = refs[5:5 + n_cast]
    o_ref, o_a_ref = refs[5 + n_cast:7 + n_cast]
    cast_out = refs[7 + n_cast:7 + 2 * n_cast]
    rest = refs[7 + 2 * n_cast:]
    xn_ref = rest[-1]
    w_out_ref, w_a_out_ref, vn_ref = rest[:3] if emit_w else (None, None, None)
    j = pl.program_id(1)

    @pl.when(j == 0)
    def _():
        xn = _rms(x_ref[...], g_ref[...]).astype(BF16)
        xn_ref[...] = xn
        w_a = w_a_ref[...].astype(BF16)
        if emit_w:
            w_a_out_ref[...] = w_a
        o_a_ref[...] = _dot_nt(xn, w_a).astype(o_a_ref.dtype)

    for src, dst in zip(cast_in, cast_out):
        dst[...] = src[...].astype(dst.dtype)

    tn = o_ref.shape[1]
    for k in range(n_j):
        @pl.when(j == k)
        def _(k=k):
            w = w_ref[...].astype(BF16)
            if emit_w:
                w_out_ref[...] = w
            o_ref[...] = _activate(_dot_nt(xn_ref[...], w), k * tn, g_v_ref, vn_ref)


def _inproj(x2d, g_pre, g_v, w_in_t, tm, w_bf=None, cast=()):
    t = x2d.shape[0]
    emit_w = w_bf is None
    tn = INPROJ_TN if emit_w else INPROJ_TN_BF16
    n_i, n_j = t // tm, N_MAIN // tn

    def w_rows(i, j):
        row0 = j * tn + jnp.where(j * tn >= OFF_MA, GATE_RANK, 0)
        return (pl.multiple_of(row0, GATE_RANK), 0)

    if emit_w:
        w_args = (w_in_t, w_in_t)
        w_specs = [pl.BlockSpec((pl.Element(tn), pl.Element(D_MODEL)), w_rows),
                   pl.BlockSpec((A_PAD, D_MODEL), lambda i, j: (W_IN_A // A_PAD, 0))]
        w_out_specs = [pl.BlockSpec((tn, D_MODEL), lambda i, j: (j, 0)),
                       pl.BlockSpec((A_PAD, D_MODEL), lambda i, j: (0, 0)),
                       pl.BlockSpec((tm, E_A), lambda i, j: (i, 0))]
        w_out_shapes = [jax.ShapeDtypeStruct((N_MAIN, D_MODEL), BF16),
                        jax.ShapeDtypeStruct((A_PAD, D_MODEL), BF16),
                        jax.ShapeDtypeStruct((t, E_A), F32)]
        assert n_i == 1, "the bf16 weight copy is written once, by a single token tile"
    else:
        w_args = w_bf
        w_specs = [pl.BlockSpec((tn, D_MODEL), lambda i, j: (j, 0)),
                   pl.BlockSpec((A_PAD, D_MODEL), lambda i, j: (0, 0))]
        w_out_specs, w_out_shapes = [], []

    cast_specs = []
    for w in cast:
        per_i = w.shape[0] // n_i
        assert per_i * n_i == w.shape[0] and per_i % BF16_SUBLANES == 0
        n_sub = max(n for n in range(1, n_j + 1)
                    if per_i % n == 0 and (per_i // n) % BF16_SUBLANES == 0)
        cast_specs.append(pl.BlockSpec(
            (per_i // n_sub, w.shape[1]),
            lambda i, j, n_sub=n_sub: (i * n_sub + jnp.minimum(j, n_sub - 1), 0)))

    kern = functools.partial(_inproj_kernel, n_cast=len(cast), emit_w=emit_w, n_j=n_j)
    return pl.pallas_call(
        kern,
        grid=(n_i, n_j),
        in_specs=[
            pl.BlockSpec((tm, D_MODEL), lambda i, j: (i, 0)),
            pl.BlockSpec((1, D_MODEL), lambda i, j: (0, 0)),
            pl.BlockSpec((1, E_A), lambda i, j: (0, 0)),
        ] + w_specs + cast_specs,
        out_specs=[
            pl.BlockSpec((tm, tn), lambda i, j: (i, j)),
            pl.BlockSpec((tm, A_PAD), lambda i, j: (i, 0)),
        ] + cast_specs + w_out_specs,
        out_shape=[
            jax.ShapeDtypeStruct((t, N_MAIN), BF16),
            jax.ShapeDtypeStruct((t, A_PAD), BF16),
        ] + [jax.ShapeDtypeStruct(w.shape, BF16) for w in cast] + w_out_shapes,
        scratch_shapes=[pltpu.VMEM((tm, D_MODEL), BF16)],
        compiler_params=pltpu.CompilerParams(
            dimension_semantics=("arbitrary", "arbitrary"),
            vmem_limit_bytes=VMEM_LIMIT),
        name="inproj_f32w" if emit_w else "inproj",
    )(x2d, g_pre, g_v, *w_args, *cast)


def _gla_levels(c):
    levels = []
    blk = c
    while blk > GLA_SUB:
        per = blk // GLA_SUB
        levels.append((blk, [(j // per) * per + per // 2 - 1 for j in range(c // GLA_SUB)]))
        blk //= 2
    return levels


def _gla_masks(c):
    t = lax.broadcasted_iota(jnp.int32, (c, c), 0)
    s = lax.broadcasted_iota(jnp.int32, (c, c), 1)
    levels = _gla_levels(c)
    masks = []
    for blk, _ in levels:
        sh = blk.bit_length() - 1
        half = blk // 2
        same = (t >> sh) == (s >> sh)
        masks.append(same & ((t & (blk - 1)) >= half) & ((s & (blk - 1)) < half))
    sh = GLA_SUB.bit_length() - 1
    masks.append(((t >> sh) == (s >> sh)) & (s <= t))
    return masks


def _run_interleaved(*gens, stride=None):
    stride = stride or (1,) * len(gens)
    results = [None] * len(gens)
    live = list(range(len(gens)))
    while live:
        for i in list(live):
            try:
                for _ in range(stride[i]):
                    next(gens[i])
            except StopIteration as stop:
                results[i] = stop.value
                live.remove(i)
    return results


def _gla_chunk(q, k, vb, log_a, s_heads, tril_ones, masks):
    c = q.shape[0]
    hi, lo = _split_bf16(log_a)
    b = _dot(tril_ones, hi) + _dot(tril_ones, lo)
    n = c // GLA_SUB

    def rep(x):
        return _bcast_rows(x, list(range(n)), GLA_SUB)

    ends = jnp.concatenate([b[(j + 1) * GLA_SUB - 1:(j + 1) * GLA_SUB, :] for j in range(n)], axis=0)
    starts = jnp.concatenate([jnp.zeros((1, b.shape[1]), F32), ends[:n - 1]], axis=0)
    starts_r = rep(starts)
    q_e = q * jnp.exp(b - starts_r)
    k_e = k * jnp.exp(rep(ends) - b)

    qs, ks = [], []
    for _, ref_blocks in _gla_levels(c):
        ref = jnp.concatenate([ends[r:r + 1] for r in ref_blocks], axis=0)
        qs.append((q_e * rep(jnp.exp(jnp.minimum(starts - ref, 0.0)))).astype(BF16))
        ks.append((k_e * rep(jnp.exp(jnp.minimum(ref - ends, 0.0)))).astype(BF16))
    qs.append(q_e.astype(BF16))
    ks.append((k * jnp.exp(starts_r - b)).astype(BF16))

    q_in = (q_e * rep(jnp.exp(starts))).astype(BF16)
    k_out = k_e * rep(jnp.exp(ends[n - 1:n] - ends))
    yield

    outs, new_states = [], []
    for h in range(HB):
        kc = slice(h * DK, (h + 1) * DK)
        vc = slice(h * DV, (h + 1) * DV)
        att = jnp.zeros((c, c), F32)
        for ql, kl, m in zip(qs, ks, masks):
            att = jnp.where(m, _dot_nt(ql[:, kc], kl[:, kc]), att)
        v_h = vb[:, vc]
        s_h = s_heads[h]
        o_h = _dot(att.astype(BF16), v_h) + _dot(q_in[:, kc], s_h.astype(BF16))
        b_t = jnp.transpose(b[:, kc])
        decay = jnp.exp(jnp.broadcast_to(b_t[:, c - 1:c], (DK, DV)))
        k_t = jnp.transpose(k_out[:, kc]).astype(BF16)
        new_states.append(decay * s_h + _dot(k_t, v_h))
        outs.append(o_h)
        yield
    return jnp.concatenate(outs, axis=1), new_states


def _gla_sample_kernel(q_ref, k_ref, vb_ref, a_ref, w_a2_ref, b_a_ref, s0_ref,
                       o_ref, s1_ref, *, nb, seq):
    rows = nb * seq
    log_a = _log_decay(a_ref[...], w_a2_ref[...], b_a_ref[...])
    t = lax.broadcasted_iota(jnp.int32, (rows, rows), 0)
    s = lax.broadcasted_iota(jnp.int32, (rows, rows), 1)
    sh = seq.bit_length() - 1
    causal = ((t >> sh) == (s >> sh)) & (s <= t)
    tril_ones = jnp.where(causal, 1.0, 0.0).astype(BF16)
    hi, lo = _split_bf16(log_a)
    b = _dot(tril_ones, hi) + _dot(tril_ones, lo)

    q = q_ref[...].astype(F32)
    k = k_ref[...].astype(F32)
    vb = vb_ref[...]
    q_t = q * jnp.exp(b)
    k_t = (k * jnp.exp(-b)).astype(BF16)
    b_last = _bcast_rows(b, [n * seq + seq - 1 for n in range(nb)], seq)
    k_out = k * jnp.exp(b_last - b)
    col = lax.broadcasted_iota(jnp.int32, (DK, rows), 1) >> sh

    outs = []
    for h in range(HB):
        kc = slice(h * DK, (h + 1) * DK)
        vc = slice(h * DV, (h + 1) * DV)
        v_h = vb[:, vc]
        att = jnp.where(causal, _dot_nt(q_t[:, kc].astype(BF16), k_t[:, kc]), 0.0)
        o_intra = _dot(att.astype(BF16), v_h)
        b_tr = jnp.transpose(b[:, kc])
        k_tr = jnp.transpose(k_out[:, kc])
        o_inter = []
        for n in range(nb):
            s_n = s0_ref[n, h]
            r0 = n * seq
            o_inter.append(_dot(q_t[r0:r0 + seq, kc], s_n))
            decay = jnp.exp(jnp.broadcast_to(b_tr[:, r0 + seq - 1:r0 + seq], (DK, DV)))
            k_n = jnp.where(col == n, k_tr, 0.0).astype(BF16)
            s1_ref[n, h] = decay * s_n + _dot(k_n, v_h)
        outs.append(o_intra + jnp.concatenate(o_inter, axis=0))
    o_ref[...] = jnp.concatenate(outs, axis=1)


def _gla_sample(proj, proj_a, w_a2_p, b_a, s0, nb, seq):
    n_batch = s0.shape[0]
    rows = nb * seq
    kern = functools.partial(_gla_sample_kernel, nb=nb, seq=seq)
    return pl.pallas_call(
        kern,
        grid=(n_batch // nb,),
        in_specs=[
            pl.BlockSpec((rows, K_B), lambda i: (i, OFF_Q // K_B)),
            pl.BlockSpec((rows, K_B), lambda i: (i, OFF_K // K_B)),
            pl.BlockSpec((rows, E_B), lambda i: (i, OFF_VB // E_B)),
            pl.BlockSpec((rows, A_PAD), lambda i: (i, 0)),
            pl.BlockSpec((A_PAD, K_B), lambda i: (0, 0)),
            pl.BlockSpec((1, K_B), lambda i: (0, 0)),
            pl.BlockSpec((nb, HB, DK, DV), lambda i: (i, 0, 0, 0)),
        ],
        out_specs=[
            pl.BlockSpec((rows, E_B), lambda i: (i, 0)),
            pl.BlockSpec((nb, HB, DK, DV), lambda i: (i, 0, 0, 0)),
        ],
        out_shape=[
            jax.ShapeDtypeStruct((n_batch * seq, E_B), F32),
            jax.ShapeDtypeStruct(s0.shape, F32),
        ],
        compiler_params=pltpu.CompilerParams(
            dimension_semantics=("arbitrary",),
            vmem_limit_bytes=VMEM_LIMIT),
        name="gla_sample",
    )(proj, proj, proj, proj_a, w_a2_p, b_a, s0)


def _branch_a(proj_ref, w_s_ref, b_col_ref, tm, seq):
    vn_b = proj_ref[:, OFF_V:OFF_V + E_A]
    t = lax.broadcasted_iota(jnp.int32, (CHUNK, CHUNK), 0)
    s = lax.broadcasted_iota(jnp.int32, (CHUNK, CHUNK), 1)
    causal = s <= t
    if seq < CHUNK:
        sh = seq.bit_length() - 1
        causal = causal & ((t >> sh) == (s >> sh))
    w_sp = [jnp.where(causal, w_s_ref[g], 0.0).astype(BF16) for g in range(HA)]
    yield
    y_rows = []
    for c in range(tm // CHUNK):
        r = slice(c * CHUNK, (c + 1) * CHUNK)
        gate = jnp.concatenate(
            [_dot(w_sp[g], vn_b[r, g * DA:(g + 1) * DA]) + b_col_ref[:, g:g + 1]
             for g in range(HA)], axis=1)
        silu_z = proj_ref[r, OFF_Z:OFF_Z + E_A].astype(F32)
        y_rows.append((proj_ref[r, OFF_U:OFF_U + E_A].astype(F32) * gate * silu_z).astype(BF16))
        yield
    return y_rows[0] if len(y_rows) == 1 else jnp.concatenate(y_rows, axis=0)


def _prompt_gla(proj_ref, proj_a_ref, w_a2_ref, b_a_ref, g_o_ref, states, tm):
    log_a = _log_decay(proj_a_ref[...], w_a2_ref[...], b_a_ref[...])
    t = lax.broadcasted_iota(jnp.int32, (CHUNK, CHUNK), 0)
    s = lax.broadcasted_iota(jnp.int32, (CHUNK, CHUNK), 1)
    tril_ones = jnp.where(s <= t, 1.0, 0.0).astype(BF16)
    masks = _gla_masks(CHUNK)
    yield
    y_rows = []
    for c in range(tm // CHUNK):
        r = slice(c * CHUNK, (c + 1) * CHUNK)
        q = proj_ref[r, OFF_Q:OFF_Q + K_B].astype(F32)
        k = proj_ref[r, OFF_K:OFF_K + K_B].astype(F32)
        vb = proj_ref[r, OFF_VB:OFF_VB + E_B]
        o_c, states = yield from _gla_chunk(q, k, vb, log_a[r], states, tril_ones, masks)
        y_rows.append(_branch_b_out(o_c, proj_ref[r, OFF_ZB:OFF_ZB + E_B], g_o_ref).astype(BF16))
        yield
    y_b = y_rows[0] if len(y_rows) == 1 else jnp.concatenate(y_rows, axis=0)
    return y_b, states


def _branch_b_out(o, silu_zb, g_o_ref):
    g_o = g_o_ref[...]
    on = jnp.concatenate(
        [_rms(o[:, h * DV:(h + 1) * DV], g_o[:, h * DV:(h + 1) * DV]) for h in range(HB)], axis=1)
    return on * silu_zb.astype(F32)


COL_BLOCKS = 4
COL_BLOCK = D_MODEL // COL_BLOCKS


def _merge(y_a, y_b, proj_ref, w_pa_ref, w_pb_ref, m_ref):
    for j in range(COL_BLOCKS):
        c = slice(j * COL_BLOCK, (j + 1) * COL_BLOCK)
        gate_a = proj_ref[:, OFF_MA + c.start:OFF_MA + c.stop].astype(F32)
        gate_b = proj_ref[:, OFF_MB + c.start:OFF_MB + c.stop].astype(F32)
        m_ref[:, c] = (gate_a * _dot(y_a, w_pa_ref[:, c])
                       + gate_b * _dot(y_b, w_pb_ref[:, c])).astype(m_ref.dtype)


def _mix_prompt_kernel(proj_ref, proj_a_ref, w_s_ref, b_col_ref, w_a2_ref, b_a_ref,
                       g_o_ref, w_pa_ref, w_pb_ref, m_ref, s_out_ref, s_ref, *, tm, seq):
    keep = pl.program_id(1) != 0
    states = [jnp.where(keep, s_ref[h], 0.0) for h in range(HB)]
    y_a, = _run_interleaved(_branch_a(proj_ref, w_s_ref, b_col_ref, tm, seq))
    (y_b, states), = _run_interleaved(
        _prompt_gla(proj_ref, proj_a_ref, w_a2_ref, b_a_ref, g_o_ref, states, tm))
    for h in range(HB):
        s_ref[h] = states[h]
        s_out_ref[0, h] = states[h]
    _merge(y_a, y_b, proj_ref, w_pa_ref, w_pb_ref, m_ref)


def _mix_sample_kernel(proj_ref, o_ref, w_s_ref, b_col_ref, g_o_ref, w_pa_ref, w_pb_ref,
                       m_ref, *, tm, seq):
    y_a, = _run_interleaved(_branch_a(proj_ref, w_s_ref, b_col_ref, tm, seq))
    y_b = _branch_b_out(o_ref[...], proj_ref[:, OFF_ZB:OFF_ZB + E_B], g_o_ref).astype(BF16)
    _merge(y_a, y_b, proj_ref, w_pa_ref, w_pb_ref, m_ref)


def _out_kernel(m_ref, x_ref, p_ref, w_o_ref, g_post_ref, w_pg_ref, w_pe_ref, y_ref):
    m = m_ref[...]
    blocks = [slice(j * COL_BLOCK, (j + 1) * COL_BLOCK) for j in range(COL_BLOCKS)]
    mo = jnp.concatenate([_dot(m, w_o_ref[:, c]) for c in blocks], axis=1)
    h_res = x_ref[...] + _rms(mo, g_post_ref[...])
    h_b = h_res.astype(BF16)
    p_b = p_ref[...].astype(BF16)
    for c in blocks:
        pg = _sigmoid(_dot(h_b, w_pg_ref[:, c]))
        y_ref[:, c] = h_res[:, c] + pg * _dot(p_b, w_pe_ref[:, c])


def _const_spec(shape):
    nd = len(shape)
    return pl.BlockSpec(shape, lambda *_: (0,) * nd, pipeline_mode=pl.Buffered(1))


def _mix_prompt(proj, proj_a, wts, n_batch, seq, tm):
    nt = seq // tm
    kern = functools.partial(_mix_prompt_kernel, tm=tm, seq=seq)
    row = lambda b, i: (b * nt + i, 0)
    names = ("w_s", "b_col", "w_a2", "b_a", "g_o", "w_pa", "w_pb")
    w_args = [wts[n] for n in names]
    return pl.pallas_call(
        kern,
        grid=(n_batch, nt),
        in_specs=[
            pl.BlockSpec((tm, N_MAIN), row),
            pl.BlockSpec((tm, A_PAD), row),
        ] + [_const_spec(w.shape) for w in w_args],
        out_specs=[
            pl.BlockSpec((tm, D_MODEL), row),
            pl.BlockSpec((1, HB, DK, DV), lambda b, i: (b, 0, 0, 0)),
        ],
        out_shape=[
            jax.ShapeDtypeStruct((n_batch * seq, D_MODEL), BF16),
            jax.ShapeDtypeStruct((n_batch, HB, DK, DV), F32),
        ],
        scratch_shapes=[pltpu.VMEM((HB, DK, DV), F32)],
        compiler_params=pltpu.CompilerParams(
            dimension_semantics=("arbitrary", "arbitrary"),
            vmem_limit_bytes=VMEM_LIMIT),
        name="mix_prompt",
    )(proj, proj_a, *w_args)


def _mix_sample(proj, o_gla, wts, seq, tm):
    t = proj.shape[0]
    kern = functools.partial(_mix_sample_kernel, tm=tm, seq=seq)
    row = lambda i: (i, 0)
    names = ("w_s_bd", "b_col_bd", "g_o", "w_pa", "w_pb")
    w_args = [wts[n] for n in names]
    return pl.pallas_call(
        kern,
        grid=(t // tm,),
        in_specs=[
            pl.BlockSpec((tm, N_MAIN), row),
            pl.BlockSpec((tm, E_B), row),
        ] + [_const_spec(w.shape) for w in w_args],
        out_specs=pl.BlockSpec((tm, D_MODEL), row),
        out_shape=jax.ShapeDtypeStruct((t, D_MODEL), BF16),
        compiler_params=pltpu.CompilerParams(
            dimension_semantics=("arbitrary",),
            vmem_limit_bytes=VMEM_LIMIT),
        name="mix_sample",
    )(proj, o_gla, *w_args)


def _out_proj(m, x2d, p2d, wts, tm):
    t = x2d.shape[0]
    row = lambda i: (i, 0)
    names = ("w_o", "g_post", "w_pg", "w_pe")
    w_args = [wts[n] for n in names]
    return pl.pallas_call(
        _out_kernel,
        grid=(t // tm,),
        in_specs=[
            pl.BlockSpec((tm, D_MODEL), row),
            pl.BlockSpec((tm, D_MODEL), row),
            pl.BlockSpec((tm, PLE_DIM), row),
        ] + [_const_spec(w.shape) for w in w_args],
        out_specs=pl.BlockSpec((tm, D_MODEL), row),
        out_shape=jax.ShapeDtypeStruct((t, D_MODEL), F32),
        compiler_params=pltpu.CompilerParams(
            dimension_semantics=("arbitrary",),
            vmem_limit_bytes=VMEM_LIMIT),
        name="out_proj",
    )(m, x2d, p2d, *w_args)


def _layer_weights(i, dec_seq, g_pre, w_in, g_v, w_s, b_s, w_a2, b_a, g_o,
                   w_pa, w_pb, w_o, g_post, w_pg, w_pe):
    w_a2_p = jnp.concatenate(
        [w_a2[i], jnp.zeros((A_PAD - GATE_RANK, K_B), w_a2.dtype)], axis=0).astype(BF16)
    reps = CHUNK // dec_seq
    return dict(
        g_pre=g_pre[i][None, :],
        w_in_t=w_in[i].T,
        g_v=g_v[i][None, :],
        w_s=w_s[i],
        b_col=b_s[i].T,
        w_s_bd=jnp.tile(w_s[i][:, :dec_seq, :dec_seq], (1, reps, reps)),
        b_col_bd=jnp.tile(b_s[i][:, :dec_seq].T, (reps, 1)),
        w_a2=w_a2_p,
        b_a=b_a[i][None, :],
        g_o=g_o[i].reshape(1, E_B),
        g_post=g_post[i][None, :],
    )


SIDE_WEIGHTS = ("w_pa", "w_pb", "w_o", "w_pg", "w_pe")


def kernel(x_prompt, x_sample, p_prompt, p_sample, state_gla, g_pre, w_in, g_v, w_s, b_s,
           w_a2, b_a, g_o, w_pa, w_pb, w_o, g_post, w_pg, w_pe):
    depth = w_in.shape[0]
    n_batch, seq, _ = x_prompt.shape
    dec_batch, dec_seq, _ = x_sample.shape
    hp = x_prompt.reshape(n_batch * seq, D_MODEL)
    hs = x_sample.reshape(dec_batch * dec_seq, D_MODEL)
    sp_list, ss_list, cv_list = [], [], []
    for i in range(depth):
        wts = _layer_weights(i, dec_seq, g_pre, w_in, g_v, w_s, b_s, w_a2, b_a, g_o,
                             w_pa, w_pb, w_o, g_post, w_pg, w_pe)
        side_f32 = dict(w_pa=w_pa[i], w_pb=w_pb[i], w_o=w_o[i], w_pg=w_pg[i], w_pe=w_pe[i])
        proj_s, proj_a_s, w_bf, w_a_bf, v_s = _inproj(hs, wts["g_pre"], wts["g_v"], wts["w_in_t"],
                                                      tm=1024)
        proj_p, proj_a_p, *side = _inproj(hp, wts["g_pre"], wts["g_v"], wts["w_in_t"], tm=1024,
                                          w_bf=(w_bf, w_a_bf),
                                          cast=[side_f32[n] for n in SIDE_WEIGHTS])
        wts.update(zip(SIDE_WEIGHTS, side))
        o_s, s_s = _gla_sample(proj_s, proj_a_s, wts["w_a2"], wts["b_a"], state_gla[i],
                               nb=CHUNK // dec_seq, seq=dec_seq)
        m_p, s_p = _mix_prompt(proj_p, proj_a_p, wts, n_batch, seq, tm=512)
        m_s = _mix_sample(proj_s, o_s, wts, dec_seq, tm=512)
        hp = _out_proj(m_p, hp, p_prompt[i].reshape(n_batch * seq, PLE_DIM), wts, tm=512)
        hs = _out_proj(m_s, hs, p_sample[i].reshape(dec_batch * dec_seq, PLE_DIM), wts, tm=512)
        sp_list.append(s_p)
        ss_list.append(s_s)
        cv_list.append(v_s.reshape(dec_batch, dec_seq, E_A))
    stack = (lambda xs: xs[0][None]) if depth == 1 else jnp.stack
    return (hp.reshape(n_batch, seq, D_MODEL),
            hs.reshape(dec_batch, dec_seq, D_MODEL),
            stack(sp_list), stack(ss_list), stack(cv_list))
```

```python
import functools

import jax
import jax.numpy as jnp
from jax import lax
from jax.experimental import pallas as pl
from jax.experimental.pallas import tpu as pltpu

D_MODEL = 2048
E_A = D_MODEL // 2
HA = 4
DA = E_A // HA
CHUNK = 128
HB = 4
E_B = D_MODEL // 2
DV = E_B // HB
DK = DV // 2
K_B = HB * DK
GATE_RANK = 16
GATE_TAU = 16.0
GLA_SUB = 16
PLE_DIM = 256
EPS = 1e-6

LANES = 128
OFF_U = 0
OFF_V = OFF_U + E_A
OFF_Z = OFF_V + E_A
OFF_Q = OFF_Z + E_A
OFF_K = OFF_Q + K_B
OFF_VB = OFF_K + K_B
OFF_ZB = OFF_VB + E_B
OFF_MA = OFF_ZB + E_B
OFF_MB = OFF_MA + D_MODEL
N_MAIN = OFF_MB + D_MODEL
W_IN_A = OFF_MA
A_PAD = LANES
INPROJ_TN = 1024
INPROJ_TN_BF16 = 2048

F32 = jnp.float32
BF16 = jnp.bfloat16

VMEM_LIMIT = 56 * 1024 * 1024


def _dot(a, b):
    return jnp.dot(a, b, preferred_element_type=F32)


def _dot_nt(a, b):
    return lax.dot_general(a, b, (((1,), (1,)), ((), ())), preferred_element_type=F32)


def _sigmoid(x):
    return 1.0 / (1.0 + jnp.exp(-x))


def _rms(x, g):
    return x * lax.rsqrt(jnp.mean(x * x, axis=-1, keepdims=True) + EPS) * g


def _split_bf16(x):
    hi = x.astype(BF16)
    lo = (x - hi.astype(F32)).astype(BF16)
    return hi, lo


def _log_decay(a_lr, w_a2, b_a):
    pre = _dot(a_lr, w_a2) + b_a
    return (jnp.minimum(pre, 0.0) - jnp.log(1.0 + jnp.exp(-jnp.abs(pre)))) * (1.0 / GATE_TAU)


def _bcast_rows(b, idxs, seg):
    parts = []
    for i in idxs:
        if i < 0:
            parts.append(jnp.zeros((seg, b.shape[1]), b.dtype))
        else:
            parts.append(jnp.broadcast_to(b[i:i + 1, :], (seg, b.shape[1])))
    return parts[0] if len(parts) == 1 else jnp.concatenate(parts, axis=0)


BF16_SUBLANES = 16


def _inproj_kernel(*refs, n_cast, emit_w):
    x_ref, g_ref, w_ref, w_a_ref = refs[:4]
    cast_in = refs[4:4 + n_cast]
    o_ref, o_a_ref = refs[4 + n_cast:6 + n_cast]
    cast_out = refs[6 + n_cast:6 + 2 * n_cast]
    rest = refs[6 + 2 * n_cast:]
    xn_ref = rest[-1]

    @pl.when(pl.program_id(1) == 0)
    def _():
        xn = _rms(x_ref[...], g_ref[...]).astype(BF16)
        xn_ref[...] = xn
        w_a = w_a_ref[...].astype(BF16)
        if emit_w:
            rest[1][...] = w_a
        o_a_ref[...] = _dot_nt(xn, w_a).astype(o_a_ref.dtype)

    for src, dst in zip(cast_in, cast_out):
        dst[...] = src[...].astype(dst.dtype)
    w = w_ref[...].astype(BF16)
    if emit_w:
        rest[0][...] = w
    o_ref[...] = _dot_nt(xn_ref[...], w).astype(o_ref.dtype)


def _inproj(x2d, g_pre, w_in_t, tm, w_bf=None, cast=()):
    t = x2d.shape[0]
    emit_w = w_bf is None
    tn = INPROJ_TN if emit_w else INPROJ_TN_BF16
    n_i, n_j = t // tm, N_MAIN // tn

    def w_rows(i, j):
        row0 = j * tn + jnp.where(j * tn >= OFF_MA, GATE_RANK, 0)
        return (pl.multiple_of(row0, GATE_RANK), 0)

    if emit_w:
        w_args = (w_in_t, w_in_t)
        w_specs = [pl.BlockSpec((pl.Element(tn), pl.Element(D_MODEL)), w_rows),
                   pl.BlockSpec((A_PAD, D_MODEL), lambda i, j: (W_IN_A // A_PAD, 0))]
        w_out_specs = [pl.BlockSpec((tn, D_MODEL), lambda i, j: (j, 0)),
                       pl.BlockSpec((A_PAD, D_MODEL), lambda i, j: (0, 0))]
        w_out_shapes = [jax.ShapeDtypeStruct((N_MAIN, D_MODEL), BF16),
                        jax.ShapeDtypeStruct((A_PAD, D_MODEL), BF16)]
        assert n_i == 1, "the bf16 weight copy is written once, by a single token tile"
    else:
        w_args = w_bf
        w_specs = [pl.BlockSpec((tn, D_MODEL), lambda i, j: (j, 0)),
                   pl.BlockSpec((A_PAD, D_MODEL), lambda i, j: (0, 0))]
        w_out_specs, w_out_shapes = [], []

    cast_specs = []
    for w in cast:
        per_i = w.shape[0] // n_i
        assert per_i * n_i == w.shape[0] and per_i % BF16_SUBLANES == 0
        n_sub = max(n for n in range(1, n_j + 1)
                    if per_i % n == 0 and (per_i // n) % BF16_SUBLANES == 0)
        cast_specs.append(pl.BlockSpec(
            (per_i // n_sub, w.shape[1]),
            lambda i, j, n_sub=n_sub: (i * n_sub + jnp.minimum(j, n_sub - 1), 0)))

    kern = functools.partial(_inproj_kernel, n_cast=len(cast), emit_w=emit_w)
    return pl.pallas_call(
        kern,
        grid=(n_i, n_j),
        in_specs=[
            pl.BlockSpec((tm, D_MODEL), lambda i, j: (i, 0)),
            pl.BlockSpec((1, D_MODEL), lambda i, j: (0, 0)),
        ] + w_specs + cast_specs,
        out_specs=[
            pl.BlockSpec((tm, tn), lambda i, j: (i, j)),
            pl.BlockSpec((tm, A_PAD), lambda i, j: (i, 0)),
        ] + cast_specs + w_out_specs,
        out_shape=[
            jax.ShapeDtypeStruct((t, N_MAIN), BF16),
            jax.ShapeDtypeStruct((t, A_PAD), BF16),
        ] + [jax.ShapeDtypeStruct(w.shape, BF16) for w in cast] + w_out_shapes,
        scratch_shapes=[pltpu.VMEM((tm, D_MODEL), BF16)],
        compiler_params=pltpu.CompilerParams(
            dimension_semantics=("arbitrary", "arbitrary"),
            vmem_limit_bytes=VMEM_LIMIT),
        name="inproj_f32w" if emit_w else "inproj",
    )(x2d, g_pre, *w_args, *cast)


def _gla_levels(c):
    levels = []
    blk = c
    while blk > GLA_SUB:
        per = blk // GLA_SUB
        levels.append((blk, [(j // per) * per + per // 2 - 1 for j in range(c // GLA_SUB)]))
        blk //= 2
    return levels


def _gla_masks(c):
    t = lax.broadcasted_iota(jnp.int32, (c, c), 0)
    s = lax.broadcasted_iota(jnp.int32, (c, c), 1)
    levels = _gla_levels(c)
    masks = []
    for blk, _ in levels:
        sh = blk.bit_length() - 1
        half = blk // 2
        same = (t >> sh) == (s >> sh)
        masks.append(same & ((t & (blk - 1)) >= half) & ((s & (blk - 1)) < half))
    sh = GLA_SUB.bit_length() - 1
    masks.append(((t >> sh) == (s >> sh)) & (s <= t))
    return masks


def _gla_chunk(q, k, vb, log_a, s_heads, tril_ones, masks):
    c = q.shape[0]
    hi, lo = _split_bf16(log_a)
    b = _dot(tril_ones, hi) + _dot(tril_ones, lo)
    n = c // GLA_SUB

    def rep(x):
        return _bcast_rows(x, list(range(n)), GLA_SUB)

    ends = jnp.concatenate([b[(j + 1) * GLA_SUB - 1:(j + 1) * GLA_SUB, :] for j in range(n)], axis=0)
    starts = jnp.concatenate([jnp.zeros((1, b.shape[1]), F32), ends[:n - 1]], axis=0)
    starts_r = rep(starts)
    q_e = q * jnp.exp(b - starts_r)
    k_e = k * jnp.exp(rep(ends) - b)

    qs, ks = [], []
    for _, ref_blocks in _gla_levels(c):
        ref = jnp.concatenate([ends[r:r + 1] for r in ref_blocks], axis=0)
        qs.append((q_e * rep(jnp.exp(jnp.minimum(starts - ref, 0.0)))).astype(BF16))
        ks.append((k_e * rep(jnp.exp(jnp.minimum(ref - ends, 0.0)))).astype(BF16))
    qs.append(q_e.astype(BF16))
    ks.append((k * jnp.exp(starts_r - b)).astype(BF16))

    q_in = (q_e * rep(jnp.exp(starts))).astype(BF16)
    k_out = k_e * rep(jnp.exp(ends[n - 1:n] - ends))

    outs, new_states = [], []
    for h in range(HB):
        kc = slice(h * DK, (h + 1) * DK)
        vc = slice(h * DV, (h + 1) * DV)
        att = jnp.zeros((c, c), F32)
        for ql, kl, m in zip(qs, ks, masks):
            att = jnp.where(m, _dot_nt(ql[:, kc], kl[:, kc]), att)
        v_h = vb[:, vc]
        s_h = s_heads[h]
        o_h = _dot(att.astype(BF16), v_h) + _dot(q_in[:, kc], s_h.astype(BF16))
        b_t = jnp.transpose(b[:, kc])
        decay = jnp.exp(jnp.broadcast_to(b_t[:, c - 1:c], (DK, DV)))
        k_t = jnp.transpose(k_out[:, kc]).astype(BF16)
        new_states.append(decay * s_h + _dot(k_t, v_h))
        outs.append(o_h)
    return jnp.concatenate(outs, axis=1), new_states


def _gla_sample_kernel(q_ref, k_ref, vb_ref, a_ref, w_a2_ref, b_a_ref, s0_ref,
                       o_ref, s1_ref, *, nb, seq):
    rows = nb * seq
    log_a = _log_decay(a_ref[...], w_a2_ref[...], b_a_ref[...])
    t = lax.broadcasted_iota(jnp.int32, (rows, rows), 0)
    s = lax.broadcasted_iota(jnp.int32, (rows, rows), 1)
    sh = seq.bit_length() - 1
    causal = ((t >> sh) == (s >> sh)) & (s <= t)
    tril_ones = jnp.where(causal, 1.0, 0.0).astype(BF16)
    hi, lo = _split_bf16(log_a)
    b = _dot(tril_ones, hi) + _dot(tril_ones, lo)

    q = q_ref[...].astype(F32) * (DK ** -0.5)
    k = k_ref[...].astype(F32)
    vb = vb_ref[...]
    q_t = q * jnp.exp(b)
    k_t = (k * jnp.exp(-b)).astype(BF16)
    b_last = _bcast_rows(b, [n * seq + seq - 1 for n in range(nb)], seq)
    k_out = k * jnp.exp(b_last - b)
    col = lax.broadcasted_iota(jnp.int32, (DK, rows), 1) >> sh

    outs = []
    for h in range(HB):
        kc = slice(h * DK, (h + 1) * DK)
        vc = slice(h * DV, (h + 1) * DV)
        v_h = vb[:, vc]
        att = jnp.where(causal, _dot_nt(q_t[:, kc].astype(BF16), k_t[:, kc]), 0.0)
        o_intra = _dot(att.astype(BF16), v_h)
        b_tr = jnp.transpose(b[:, kc])
        k_tr = jnp.transpose(k_out[:, kc])
        o_inter = []
        for n in range(nb):
            s_n = s0_ref[n, h]
            r0 = n * seq
            o_inter.append(_dot(q_t[r0:r0 + seq, kc], s_n))
            decay = jnp.exp(jnp.broadcast_to(b_tr[:, r0 + seq - 1:r0 + seq], (DK, DV)))
            k_n = jnp.where(col == n, k_tr, 0.0).astype(BF16)
            s1_ref[n, h] = decay * s_n + _dot(k_n, v_h)
        outs.append(o_intra + jnp.concatenate(o_inter, axis=0))
    o_ref[...] = jnp.concatenate(outs, axis=1)


def _gla_sample(proj, proj_a, w_a2_p, b_a, s0, nb, seq):
    n_batch = s0.shape[0]
    rows = nb * seq
    kern = functools.partial(_gla_sample_kernel, nb=nb, seq=seq)
    return pl.pallas_call(
        kern,
        grid=(n_batch // nb,),
        in_specs=[
            pl.BlockSpec((rows, K_B), lambda i: (i, OFF_Q // K_B)),
            pl.BlockSpec((rows, K_B), lambda i: (i, OFF_K // K_B)),
            pl.BlockSpec((rows, E_B), lambda i: (i, OFF_VB // E_B)),
            pl.BlockSpec((rows, A_PAD), lambda i: (i, 0)),
            pl.BlockSpec((A_PAD, K_B), lambda i: (0, 0)),
            pl.BlockSpec((1, K_B), lambda i: (0, 0)),
            pl.BlockSpec((nb, HB, DK, DV), lambda i: (i, 0, 0, 0)),
        ],
        out_specs=[
            pl.BlockSpec((rows, E_B), lambda i: (i, 0)),
            pl.BlockSpec((nb, HB, DK, DV), lambda i: (i, 0, 0, 0)),
        ],
        out_shape=[
            jax.ShapeDtypeStruct((n_batch * seq, E_B), F32),
            jax.ShapeDtypeStruct(s0.shape, F32),
        ],
        compiler_params=pltpu.CompilerParams(
            dimension_semantics=("arbitrary",),
            vmem_limit_bytes=VMEM_LIMIT),
        name="gla_sample",
    )(proj, proj, proj, proj_a, w_a2_p, b_a, s0)


def _spatial_masked_weights(w_s_ref, seq):
    t = lax.broadcasted_iota(jnp.int32, (CHUNK, CHUNK), 0)
    s = lax.broadcasted_iota(jnp.int32, (CHUNK, CHUNK), 1)
    causal = s <= t
    if seq < CHUNK:
        sh = seq.bit_length() - 1
        causal = causal & ((t >> sh) == (s >> sh))
    return [jnp.where(causal, w_s_ref[g], 0.0).astype(BF16) for g in range(HA)]


def _branch_a_chunk(proj_ref, r, g_v_ref, w_sp, b_col_ref):
    vn = _rms(proj_ref[r, OFF_V:OFF_V + E_A].astype(F32), g_v_ref[...])
    vn_b = vn.astype(BF16)
    gate = jnp.concatenate(
        [_dot(w_sp[g], vn_b[:, g * DA:(g + 1) * DA]) + b_col_ref[:, g:g + 1]
         for g in range(HA)], axis=1)
    z = proj_ref[r, OFF_Z:OFF_Z + E_A].astype(F32)
    return proj_ref[r, OFF_U:OFF_U + E_A].astype(F32) * gate * (z * _sigmoid(z)), vn


def _branch_b_out(o, zb, g_o_ref):
    g_o = g_o_ref[...]
    on = jnp.concatenate(
        [_rms(o[:, h * DV:(h + 1) * DV], g_o[:, h * DV:(h + 1) * DV]) for h in range(HB)], axis=1)
    zb = zb.astype(F32)
    return on * (zb * _sigmoid(zb))


COL_BLOCKS = 4
COL_BLOCK = D_MODEL // COL_BLOCKS


def _merge(y_a, y_b, proj_ref, w_pa_ref, w_pb_ref, m_ref):
    for j in range(COL_BLOCKS):
        c = slice(j * COL_BLOCK, (j + 1) * COL_BLOCK)
        m_a = proj_ref[:, OFF_MA + c.start:OFF_MA + c.stop].astype(F32)
        m_b = proj_ref[:, OFF_MB + c.start:OFF_MB + c.stop].astype(F32)
        m_ref[:, c] = (_sigmoid(m_a) * _dot(y_a, w_pa_ref[:, c])
                       + _sigmoid(m_b) * _dot(y_b, w_pb_ref[:, c])).astype(m_ref.dtype)


def _chunks(tm):
    return [slice(c * CHUNK, (c + 1) * CHUNK) for c in range(tm // CHUNK)]


def _mix_prompt_kernel(proj_ref, proj_a_ref, g_v_ref, w_s_ref, b_col_ref, w_a2_ref, b_a_ref,
                       g_o_ref, w_pa_ref, w_pb_ref, m_ref, s_out_ref, s_ref, *, tm, seq):
    keep = pl.program_id(1) != 0
    states = [jnp.where(keep, s_ref[h], 0.0) for h in range(HB)]
    w_sp = _spatial_masked_weights(w_s_ref, seq)
    t = lax.broadcasted_iota(jnp.int32, (CHUNK, CHUNK), 0)
    s = lax.broadcasted_iota(jnp.int32, (CHUNK, CHUNK), 1)
    tril_ones = jnp.where(s <= t, 1.0, 0.0).astype(BF16)
    masks = _gla_masks(CHUNK)
    log_a = _log_decay(proj_a_ref[...], w_a2_ref[...], b_a_ref[...])
    ya_rows, yb_rows = [], []
    for r in _chunks(tm):
        ya_rows.append(_branch_a_chunk(proj_ref, r, g_v_ref, w_sp, b_col_ref)[0].astype(BF16))
        q = proj_ref[r, OFF_Q:OFF_Q + K_B].astype(F32) * (DK ** -0.5)
        k = proj_ref[r, OFF_K:OFF_K + K_B].astype(F32)
        vb = proj_ref[r, OFF_VB:OFF_VB + E_B]
        o_c, states = _gla_chunk(q, k, vb, log_a[r], states, tril_ones, masks)
        yb_rows.append(_branch_b_out(o_c, proj_ref[r, OFF_ZB:OFF_ZB + E_B], g_o_ref).astype(BF16))
    for h in range(HB):
        s_ref[h] = states[h]
        s_out_ref[0, h] = states[h]
    _merge(jnp.concatenate(ya_rows, axis=0), jnp.concatenate(yb_rows, axis=0),
           proj_ref, w_pa_ref, w_pb_ref, m_ref)


def _mix_sample_kernel(proj_ref, o_ref, g_v_ref, w_s_ref, b_col_ref, g_o_ref, w_pa_ref, w_pb_ref,
                       m_ref, vn_ref, *, tm, seq):
    w_sp = _spatial_masked_weights(w_s_ref, seq)
    ya_rows, yb_rows = [], []
    for r in _chunks(tm):
        y_a, vn = _branch_a_chunk(proj_ref, r, g_v_ref, w_sp, b_col_ref)
        vn_ref[r, :] = vn
        ya_rows.append(y_a.astype(BF16))
        yb_rows.append(_branch_b_out(o_ref[r, :], proj_ref[r, OFF_ZB:OFF_ZB + E_B],
                                     g_o_ref).astype(BF16))
    _merge(jnp.concatenate(ya_rows, axis=0), jnp.concatenate(yb_rows, axis=0),
           proj_ref, w_pa_ref, w_pb_ref, m_ref)


def _out_kernel(m_p_ref, m_s_ref, x_p_ref, x_s_ref, p_p_ref, p_s_ref,
                w_o_ref, g_post_ref, w_pg_ref, w_pe_ref, y_p_ref, y_s_ref, *, n_prompt_tiles):
    is_s = pl.program_id(0) >= n_prompt_tiles
    m = jnp.where(is_s, m_s_ref[...], m_p_ref[...])
    x = jnp.where(is_s, x_s_ref[...], x_p_ref[...])
    p_b = jnp.where(is_s, p_s_ref[...], p_p_ref[...]).astype(BF16)
    blocks = [slice(j * COL_BLOCK, (j + 1) * COL_BLOCK) for j in range(COL_BLOCKS)]
    mo = jnp.concatenate([_dot(m, w_o_ref[:, c]) for c in blocks], axis=1)
    h_res = x + _rms(mo, g_post_ref[...])
    h_b = h_res.astype(BF16)
    to_s = jnp.broadcast_to(is_s, (m.shape[0], COL_BLOCK))
    for c in blocks:
        pg = _sigmoid(_dot(h_b, w_pg_ref[:, c]))
        y = h_res[:, c] + pg * _dot(p_b, w_pe_ref[:, c])
        pltpu.store(y_p_ref.at[:, c], y, mask=jnp.logical_not(to_s))
        pltpu.store(y_s_ref.at[:, c], y, mask=to_s)


def _const_spec(shape):
    nd = len(shape)
    return pl.BlockSpec(shape, lambda *_: (0,) * nd, pipeline_mode=pl.Buffered(1))


def _mix_prompt(proj, proj_a, wts, n_batch, seq, tm):
    nt = seq // tm
    kern = functools.partial(_mix_prompt_kernel, tm=tm, seq=seq)
    row = lambda b, i: (b * nt + i, 0)
    names = ("g_v", "w_s", "b_col", "w_a2", "b_a", "g_o", "w_pa", "w_pb")
    w_args = [wts[n] for n in names]
    return pl.pallas_call(
        kern,
        grid=(n_batch, nt),
        in_specs=[
            pl.BlockSpec((tm, N_MAIN), row),
            pl.BlockSpec((tm, A_PAD), row),
        ] + [_const_spec(w.shape) for w in w_args],
        out_specs=[
            pl.BlockSpec((tm, D_MODEL), row),
            pl.BlockSpec((1, HB, DK, DV), lambda b, i: (b, 0, 0, 0)),
        ],
        out_shape=[
            jax.ShapeDtypeStruct((n_batch * seq, D_MODEL), BF16),
            jax.ShapeDtypeStruct((n_batch, HB, DK, DV), F32),
        ],
        scratch_shapes=[pltpu.VMEM((HB, DK, DV), F32)],
        compiler_params=pltpu.CompilerParams(
            dimension_semantics=("arbitrary", "arbitrary"),
            vmem_limit_bytes=VMEM_LIMIT),
        name="mix_prompt",
    )(proj, proj_a, *w_args)


def _mix_sample(proj, o_gla, wts, seq, tm):
    t = proj.shape[0]
    kern = functools.partial(_mix_sample_kernel, tm=tm, seq=seq)
    row = lambda i: (i, 0)
    names = ("g_v", "w_s_bd", "b_col_bd", "g_o", "w_pa", "w_pb")
    w_args = [wts[n] for n in names]
    return pl.pallas_call(
        kern,
        grid=(t // tm,),
        in_specs=[
            pl.BlockSpec((tm, N_MAIN), row),
            pl.BlockSpec((tm, E_B), row),
        ] + [_const_spec(w.shape) for w in w_args],
        out_specs=[
            pl.BlockSpec((tm, D_MODEL), row),
            pl.BlockSpec((tm, E_A), row),
        ],
        out_shape=[
            jax.ShapeDtypeStruct((t, D_MODEL), BF16),
            jax.ShapeDtypeStruct((t, E_A), F32),
        ],
        compiler_params=pltpu.CompilerParams(
            dimension_semantics=("arbitrary",),
            vmem_limit_bytes=VMEM_LIMIT),
        name="mix_sample",
    )(proj, o_gla, *w_args)


def _out_proj(m_p, m_s, x_p, x_s, p_p, p_s, wts, tm):
    n_p, n_s = x_p.shape[0] // tm, x_s.shape[0] // tm
    kern = functools.partial(_out_kernel, n_prompt_tiles=n_p)
    prompt = lambda i: (jnp.minimum(i, n_p - 1), 0)
    sample = lambda i: (jnp.maximum(i - n_p, 0), 0)
    prompt_spec = lambda width: pl.BlockSpec((tm, width), prompt)
    sample_spec = lambda width: pl.BlockSpec((tm, width), sample, pipeline_mode=pl.Buffered(1))
    names = ("w_o", "g_post", "w_pg", "w_pe")
    w_args = [wts[n] for n in names]
    return pl.pallas_call(
        kern,
        grid=(n_p + n_s,),
        in_specs=[
            prompt_spec(D_MODEL), sample_spec(D_MODEL),
            prompt_spec(D_MODEL), sample_spec(D_MODEL),
            prompt_spec(PLE_DIM), sample_spec(PLE_DIM),
        ] + [_const_spec(w.shape) for w in w_args],
        out_specs=[prompt_spec(D_MODEL), sample_spec(D_MODEL)],
        out_shape=[
            jax.ShapeDtypeStruct(x_p.shape, F32),
            jax.ShapeDtypeStruct(x_s.shape, F32),
        ],
        compiler_params=pltpu.CompilerParams(
            dimension_semantics=("arbitrary",),
            vmem_limit_bytes=VMEM_LIMIT),
        name="out_proj",
    )(m_p, m_s, x_p, x_s, p_p, p_s, *w_args)


def _layer_weights(i, dec_seq, g_pre, w_in, g_v, w_s, b_s, w_a2, b_a, g_o,
                   w_pa, w_pb, w_o, g_post, w_pg, w_pe):
    w_a2_p = jnp.concatenate(
        [w_a2[i], jnp.zeros((A_PAD - GATE_RANK, K_B), w_a2.dtype)], axis=0).astype(BF16)
    reps = CHUNK // dec_seq
    return dict(
        g_pre=g_pre[i][None, :],
        w_in_t=w_in[i].T,
        g_v=g_v[i][None, :],
        w_s=w_s[i],
        b_col=b_s[i].T,
        w_s_bd=jnp.tile(w_s[i][:, :dec_seq, :dec_seq], (1, reps, reps)),
        b_col_bd=jnp.tile(b_s[i][:, :dec_seq].T, (reps, 1)),
        w_a2=w_a2_p,
        b_a=b_a[i][None, :],
        g_o=g_o[i].reshape(1, E_B),
        g_post=g_post[i][None, :],
    )


SIDE_WEIGHTS = ("w_pa", "w_pb", "w_o", "w_pg", "w_pe")


def kernel(x_prompt, x_sample, p_prompt, p_sample, state_gla, g_pre, w_in, g_v, w_s, b_s,
           w_a2, b_a, g_o, w_pa, w_pb, w_o, g_post, w_pg, w_pe):
    depth = w_in.shape[0]
    n_batch, seq, _ = x_prompt.shape
    dec_batch, dec_seq, _ = x_sample.shape
    hp = x_prompt.reshape(n_batch * seq, D_MODEL)
    hs = x_sample.reshape(dec_batch * dec_seq, D_MODEL)
    sp_list, ss_list, cv_list = [], [], []
    for i in range(depth):
        wts = _layer_weights(i, dec_seq, g_pre, w_in, g_v, w_s, b_s, w_a2, b_a, g_o,
                             w_pa, w_pb, w_o, g_post, w_pg, w_pe)
        side_f32 = dict(w_pa=w_pa[i], w_pb=w_pb[i], w_o=w_o[i], w_pg=w_pg[i], w_pe=w_pe[i])
        proj_s, proj_a_s, *w_bf = _inproj(hs, wts["g_pre"], wts["w_in_t"], tm=1024)
        proj_p, proj_a_p, *side = _inproj(hp, wts["g_pre"], wts["w_in_t"], tm=1024, w_bf=w_bf,
                                          cast=[side_f32[n] for n in SIDE_WEIGHTS])
        wts.update(zip(SIDE_WEIGHTS, side))
        o_s, s_s = _gla_sample(proj_s, proj_a_s, wts["w_a2"], wts["b_a"], state_gla[i],
                               nb=CHUNK // dec_seq, seq=dec_seq)
        m_p, s_p = _mix_prompt(proj_p, proj_a_p, wts, n_batch, seq, tm=512)
        m_s, v_s = _mix_sample(proj_s, o_s, wts, dec_seq, tm=512)
        hp, hs = _out_proj(m_p, m_s, hp, hs, p_prompt[i].reshape(n_batch * seq, PLE_DIM),
                           p_sample[i].reshape(dec_batch * dec_seq, PLE_DIM), wts, tm=512)
        sp_list.append(s_p)
        ss_list.append(s_s)
        cv_list.append(v_s.reshape(dec_batch, dec_seq, E_A))
    stack = (lambda xs: xs[0][None]) if depth == 1 else jnp.stack
    return (hp.reshape(n_batch, seq, D_MODEL),
            hs.reshape(dec_batch, dec_seq, D_MODEL),
            stack(sp_list), stack(ss_list), stack(cv_list))
```

```python
import functools

import jax
import jax.numpy as jnp
from jax import lax
from jax.experimental import pallas as pl
from jax.experimental.pallas import tpu as pltpu

D_MODEL = 2048
E_A = D_MODEL // 2
HA = 4
DA = E_A // HA
CHUNK = 128
HB = 4
E_B = D_MODEL // 2
DV = E_B // HB
DK = DV // 2
K_B = HB * DK
GATE_RANK = 16
GATE_TAU = 16.0
GLA_SUB = 16
PLE_DIM = 256
EPS = 1e-6

LANES = 128
OFF_U = 0
OFF_V = OFF_U + E_A
OFF_Z = OFF_V + E_A
OFF_Q = OFF_Z + E_A
OFF_K = OFF_Q + K_B
OFF_VB = OFF_K + K_B
OFF_ZB = OFF_VB + E_B
OFF_MA = OFF_ZB + E_B
OFF_MB = OFF_MA + D_MODEL
N_MAIN = OFF_MB + D_MODEL
W_IN_A = OFF_MA
A_PAD = LANES
INPROJ_TN = 1024
INPROJ_TN_BF16 = 2048

F32 = jnp.float32
BF16 = jnp.bfloat16

VMEM_LIMIT = 56 * 1024 * 1024


def _dot(a, b):
    return jnp.dot(a, b, preferred_element_type=F32)


def _dot_nt(a, b):
    return lax.dot_general(a, b, (((1,), (1,)), ((), ())), preferred_element_type=F32)


def _sigmoid(x):
    return jax.nn.sigmoid(x)


def _rms(x, g):
    return x * lax.rsqrt(jnp.mean(x * x, axis=-1, keepdims=True) + EPS) * g


def _split_bf16(x):
    hi = x.astype(BF16)
    lo = (x - hi.astype(F32)).astype(BF16)
    return hi, lo


def _log_decay(a_lr, w_a2, b_a):
    pre = _dot(a_lr, w_a2) + b_a
    return (jnp.minimum(pre, 0.0) - jnp.log(1.0 + jnp.exp(-jnp.abs(pre)))) * (1.0 / GATE_TAU)


def _bcast_rows(b, idxs, seg):
    return jnp.concatenate(
        [jnp.broadcast_to(b[i:i + 1, :], (seg, b.shape[1])) for i in idxs], axis=0)


BF16_SUBLANES = 16


def _inproj_kernel(*refs, n_cast, emit_w):
    x_ref, g_ref, w_ref, w_a_ref = refs[:4]
    cast_in = refs[4:4 + n_cast]
    o_ref, o_a_ref = refs[4 + n_cast:6 + n_cast]
    cast_out = refs[6 + n_cast:6 + 2 * n_cast]
    rest = refs[6 + 2 * n_cast:]
    xn_ref = rest[-1]

    @pl.when(pl.program_id(1) == 0)
    def _():
        xn = _rms(x_ref[...], g_ref[...]).astype(BF16)
        xn_ref[...] = xn
        w_a = w_a_ref[...]
        if emit_w:
            w_a = w_a.astype(BF16)
            rest[1][...] = w_a
        o_a_ref[...] = _dot_nt(xn, w_a).astype(o_a_ref.dtype)

    for src, dst in zip(cast_in, cast_out):
        dst[...] = src[...].astype(dst.dtype)
    w = w_ref[...]
    if emit_w:
        w = w.astype(BF16)
        rest[0][...] = w
    o_ref[...] = _dot_nt(xn_ref[...], w).astype(o_ref.dtype)


def _inproj(x2d, g_pre, w_in_t, tm, w_bf=None, cast=()):
    t = x2d.shape[0]
    emit_w = w_bf is None
    tn = INPROJ_TN if emit_w else INPROJ_TN_BF16
    n_i, n_j = t // tm, N_MAIN // tn

    def w_rows(i, j):
        row0 = j * tn + jnp.where(j * tn >= OFF_MA, GATE_RANK, 0)
        return (pl.multiple_of(row0, GATE_RANK), 0)

    if emit_w:
        w_args = (w_in_t, w_in_t)
        w_specs = [pl.BlockSpec((pl.Element(tn), pl.Element(D_MODEL)), w_rows),
                   pl.BlockSpec((A_PAD, D_MODEL), lambda i, j: (W_IN_A // A_PAD, 0))]
        w_out_specs = [pl.BlockSpec((tn, D_MODEL), lambda i, j: (j, 0)),
                       pl.BlockSpec((A_PAD, D_MODEL), lambda i, j: (0, 0))]
        w_out_shapes = [jax.ShapeDtypeStruct((N_MAIN, D_MODEL), BF16),
                        jax.ShapeDtypeStruct((A_PAD, D_MODEL), BF16)]
        assert n_i == 1, "the bf16 weight copy is written once, by a single token tile"
    else:
        w_args = w_bf
        w_specs = [pl.BlockSpec((tn, D_MODEL), lambda i, j: (j, 0)),
                   pl.BlockSpec((A_PAD, D_MODEL), lambda i, j: (0, 0))]
        w_out_specs, w_out_shapes = [], []

    cast_specs = []
    for w in cast:
        per_i = w.shape[0] // n_i
        assert per_i * n_i == w.shape[0] and per_i % BF16_SUBLANES == 0
        n_sub = max(n for n in range(1, n_j + 1)
                    if per_i % n == 0 and (per_i // n) % BF16_SUBLANES == 0)
        cast_specs.append(pl.BlockSpec(
            (per_i // n_sub, w.shape[1]),
            lambda i, j, n_sub=n_sub: (i * n_sub + jnp.minimum(j, n_sub - 1), 0)))

    kern = functools.partial(_inproj_kernel, n_cast=len(cast), emit_w=emit_w)
    return pl.pallas_call(
        kern,
        grid=(n_i, n_j),
        in_specs=[
            pl.BlockSpec((tm, D_MODEL), lambda i, j: (i, 0)),
            pl.BlockSpec((1, D_MODEL), lambda i, j: (0, 0)),
        ] + w_specs + cast_specs,
        out_specs=[
            pl.BlockSpec((tm, tn), lambda i, j: (i, j)),
            pl.BlockSpec((tm, A_PAD), lambda i, j: (i, 0)),
        ] + cast_specs + w_out_specs,
        out_shape=[
            jax.ShapeDtypeStruct((t, N_MAIN), BF16),
            jax.ShapeDtypeStruct((t, A_PAD), BF16),
        ] + [jax.ShapeDtypeStruct(w.shape, BF16) for w in cast] + w_out_shapes,
        scratch_shapes=[pltpu.VMEM((tm, D_MODEL), BF16)],
        compiler_params=pltpu.CompilerParams(
            dimension_semantics=("arbitrary", "arbitrary"),
            vmem_limit_bytes=VMEM_LIMIT),
        name="inproj_f32w" if emit_w else "inproj",
    )(x2d, g_pre, *w_args, *cast)


def _gla_levels(c):
    levels = []
    blk = c
    while blk > GLA_SUB:
        per = blk // GLA_SUB
        levels.append((blk, [(j // per) * per + per // 2 - 1 for j in range(c // GLA_SUB)]))
        blk //= 2
    return levels


def _gla_masks(c):
    t = lax.broadcasted_iota(jnp.int32, (c, c), 0)
    s = lax.broadcasted_iota(jnp.int32, (c, c), 1)
    levels = _gla_levels(c)
    masks = []
    for blk, _ in levels:
        sh = blk.bit_length() - 1
        half = blk // 2
        same = (t >> sh) == (s >> sh)
        masks.append(same & ((t & (blk - 1)) >= half) & ((s & (blk - 1)) < half))
    sh = GLA_SUB.bit_length() - 1
    masks.append(((t >> sh) == (s >> sh)) & (s <= t))
    return masks


def _gla_chunk(q, k, vb, log_a, s_heads, tril_ones, masks):
    c = q.shape[0]
    hi, lo = _split_bf16(log_a)
    b = _dot(tril_ones, hi) + _dot(tril_ones, lo)
    n = c // GLA_SUB

    def rep(x):
        return _bcast_rows(x, list(range(n)), GLA_SUB)

    ends = jnp.concatenate([b[(j + 1) * GLA_SUB - 1:(j + 1) * GLA_SUB, :] for j in range(n)], axis=0)
    starts = jnp.concatenate([jnp.zeros((1, b.shape[1]), F32), ends[:n - 1]], axis=0)
    starts_r = rep(starts)
    q_e = q * jnp.exp(b - starts_r)
    k_e = k * jnp.exp(rep(ends) - b)

    qs, ks = [], []
    for _, ref_blocks in _gla_levels(c):
        ref = jnp.concatenate([ends[r:r + 1] for r in ref_blocks], axis=0)
        qs.append((q_e * rep(jnp.exp(jnp.minimum(starts - ref, 0.0)))).astype(BF16))
        ks.append((k_e * rep(jnp.exp(jnp.minimum(ref - ends, 0.0)))).astype(BF16))
    qs.append(q_e.astype(BF16))
    ks.append((k * jnp.exp(starts_r - b)).astype(BF16))

    q_in = (q_e * rep(jnp.exp(starts))).astype(BF16)
    k_out = k_e * rep(jnp.exp(ends[n - 1:n] - ends))

    outs, new_states = [], []
    for h in range(HB):
        kc = slice(h * DK, (h + 1) * DK)
        vc = slice(h * DV, (h + 1) * DV)
        att = jnp.zeros((c, c), F32)
        for ql, kl, m in zip(qs, ks, masks):
            att = jnp.where(m, _dot_nt(ql[:, kc], kl[:, kc]), att)
        v_h = vb[:, vc]
        s_h = s_heads[h]
        o_h = _dot(att.astype(BF16), v_h) + _dot(q_in[:, kc], s_h.astype(BF16))
        b_t = jnp.transpose(b[:, kc])
        decay = jnp.exp(jnp.broadcast_to(b_t[:, c - 1:c], (DK, DV)))
        k_t = jnp.transpose(k_out[:, kc]).astype(BF16)
        new_states.append(decay * s_h + _dot(k_t, v_h))
        outs.append(o_h)
    return jnp.concatenate(outs, axis=1), new_states


def _gla_sample_kernel(q_ref, k_ref, vb_ref, a_ref, w_a2_ref, b_a_ref, s0_ref,
                       o_ref, s1_ref, *, nb, seq):
    rows = nb * seq
    log_a = _log_decay(a_ref[...], w_a2_ref[...], b_a_ref[...])
    t = lax.broadcasted_iota(jnp.int32, (rows, rows), 0)
    s = lax.broadcasted_iota(jnp.int32, (rows, rows), 1)
    sh = seq.bit_length() - 1
    causal = ((t >> sh) == (s >> sh)) & (s <= t)
    tril_ones = jnp.where(causal, 1.0, 0.0).astype(BF16)
    hi, lo = _split_bf16(log_a)
    b = _dot(tril_ones, hi) + _dot(tril_ones, lo)

    q = q_ref[...].astype(F32) * (DK ** -0.5)
    k = k_ref[...].astype(F32)
    vb = vb_ref[...]
    q_t = q * jnp.exp(b)
    k_t = (k * jnp.exp(-b)).astype(BF16)
    b_last = _bcast_rows(b, [n * seq + seq - 1 for n in range(nb)], seq)
    k_out = k * jnp.exp(b_last - b)
    col = lax.broadcasted_iota(jnp.int32, (DK, rows), 1) >> sh

    outs = []
    for h in range(HB):
        kc = slice(h * DK, (h + 1) * DK)
        vc = slice(h * DV, (h + 1) * DV)
        v_h = vb[:, vc]
        att = jnp.where(causal, _dot_nt(q_t[:, kc].astype(BF16), k_t[:, kc]), 0.0)
        o_intra = _dot(att.astype(BF16), v_h)
        b_tr = jnp.transpose(b[:, kc])
        k_tr = jnp.transpose(k_out[:, kc])
        o_inter = []
        for n in range(nb):
            s_n = s0_ref[n, h]
            r0 = n * seq
            o_inter.append(_dot(q_t[r0:r0 + seq, kc], s_n))
            decay = jnp.exp(jnp.broadcast_to(b_tr[:, r0 + seq - 1:r0 + seq], (DK, DV)))
            k_n = jnp.where(col == n, k_tr, 0.0).astype(BF16)
            s1_ref[n, h] = decay * s_n + _dot(k_n, v_h)
        outs.append(o_intra + jnp.concatenate(o_inter, axis=0))
    o_ref[...] = jnp.concatenate(outs, axis=1)


def _gla_sample(proj, proj_a, w_a2_p, b_a, s0, nb, seq):
    n_batch = s0.shape[0]
    rows = nb * seq
    kern = functools.partial(_gla_sample_kernel, nb=nb, seq=seq)
    return pl.pallas_call(
        kern,
        grid=(n_batch // nb,),
        in_specs=[
            pl.BlockSpec((rows, K_B), lambda i: (i, OFF_Q // K_B)),
            pl.BlockSpec((rows, K_B), lambda i: (i, OFF_K // K_B)),
            pl.BlockSpec((rows, E_B), lambda i: (i, OFF_VB // E_B)),
            pl.BlockSpec((rows, A_PAD), lambda i: (i, 0)),
            pl.BlockSpec((A_PAD, K_B), lambda i: (0, 0)),
            pl.BlockSpec((1, K_B), lambda i: (0, 0)),
            pl.BlockSpec((nb, HB, DK, DV), lambda i: (i, 0, 0, 0)),
        ],
        out_specs=[
            pl.BlockSpec((rows, E_B), lambda i: (i, 0)),
            pl.BlockSpec((nb, HB, DK, DV), lambda i: (i, 0, 0, 0)),
        ],
        out_shape=[
            jax.ShapeDtypeStruct((n_batch * seq, E_B), F32),
            jax.ShapeDtypeStruct(s0.shape, F32),
        ],
        compiler_params=pltpu.CompilerParams(
            dimension_semantics=("arbitrary",),
            vmem_limit_bytes=VMEM_LIMIT),
        name="gla_sample",
    )(proj, proj, proj, proj_a, w_a2_p, b_a, s0)


def _spatial_masked_weights(w_s_ref, seq):
    t = lax.broadcasted_iota(jnp.int32, (CHUNK, CHUNK), 0)
    s = lax.broadcasted_iota(jnp.int32, (CHUNK, CHUNK), 1)
    causal = s <= t
    if seq < CHUNK:
        sh = seq.bit_length() - 1
        causal = causal & ((t >> sh) == (s >> sh))
    return [jnp.where(causal, w_s_ref[g], 0.0).astype(BF16) for g in range(HA)]


def _branch_a_chunk(proj_ref, r, g_v_ref, w_sp, b_col_ref):
    vn = _rms(proj_ref[r, OFF_V:OFF_V + E_A].astype(F32), g_v_ref[...])
    vn_b = vn.astype(BF16)
    gate = jnp.concatenate(
        [_dot(w_sp[g], vn_b[:, g * DA:(g + 1) * DA]) + b_col_ref[:, g:g + 1]
         for g in range(HA)], axis=1)
    z = proj_ref[r, OFF_Z:OFF_Z + E_A].astype(F32)
    return proj_ref[r, OFF_U:OFF_U + E_A].astype(F32) * gate * (z * _sigmoid(z)), vn


def _branch_b_out(o, zb, g_o_ref):
    g_o = g_o_ref[...]
    on = jnp.concatenate(
        [_rms(o[:, h * DV:(h + 1) * DV], g_o[:, h * DV:(h + 1) * DV]) for h in range(HB)], axis=1)
    zb = zb.astype(F32)
    return on * (zb * _sigmoid(zb))


COL_BLOCKS = 4
COL_BLOCK = D_MODEL // COL_BLOCKS


def _merge(y_a, y_b, proj_ref, w_pa_ref, w_pb_ref, m_ref):
    for j in range(COL_BLOCKS):
        c = slice(j * COL_BLOCK, (j + 1) * COL_BLOCK)
        m_a = proj_ref[:, OFF_MA + c.start:OFF_MA + c.stop].astype(F32)
        m_b = proj_ref[:, OFF_MB + c.start:OFF_MB + c.stop].astype(F32)
        m_ref[:, c] = (_sigmoid(m_a) * _dot(y_a, w_pa_ref[:, c])
                       + _sigmoid(m_b) * _dot(y_b, w_pb_ref[:, c])).astype(m_ref.dtype)


def _chunks(tm):
    return [slice(c * CHUNK, (c + 1) * CHUNK) for c in range(tm // CHUNK)]


def _mix_prompt_kernel(proj_ref, proj_a_ref, g_v_ref, w_s_ref, b_col_ref, w_a2_ref, b_a_ref,
                       g_o_ref, w_pa_ref, w_pb_ref, m_ref, s_out_ref, s_ref, *, tm, seq):
    keep = pl.program_id(1) != 0
    states = [jnp.where(keep, s_ref[h], 0.0) for h in range(HB)]
    w_sp = _spatial_masked_weights(w_s_ref, seq)
    t = lax.broadcasted_iota(jnp.int32, (CHUNK, CHUNK), 0)
    s = lax.broadcasted_iota(jnp.int32, (CHUNK, CHUNK), 1)
    tril_ones = jnp.where(s <= t, 1.0, 0.0).astype(BF16)
    masks = _gla_masks(CHUNK)
    log_a = _log_decay(proj_a_ref[...], w_a2_ref[...], b_a_ref[...])
    ya_rows, yb_rows = [], []
    for r in _chunks(tm):
        ya_rows.append(_branch_a_chunk(proj_ref, r, g_v_ref, w_sp, b_col_ref)[0].astype(BF16))
        q = proj_ref[r, OFF_Q:OFF_Q + K_B].astype(F32) * (DK ** -0.5)
        k = proj_ref[r, OFF_K:OFF_K + K_B].astype(F32)
        vb = proj_ref[r, OFF_VB:OFF_VB + E_B]
        o_c, states = _gla_chunk(q, k, vb, log_a[r], states, tril_ones, masks)
        yb_rows.append(_branch_b_out(o_c, proj_ref[r, OFF_ZB:OFF_ZB + E_B], g_o_ref).astype(BF16))
    for h in range(HB):
        s_ref[h] = states[h]
        s_out_ref[0, h] = states[h]
    _merge(jnp.concatenate(ya_rows, axis=0), jnp.concatenate(yb_rows, axis=0),
           proj_ref, w_pa_ref, w_pb_ref, m_ref)


def _mix_sample_kernel(proj_ref, o_ref, g_v_ref, w_s_ref, b_col_ref, g_o_ref, w_pa_ref, w_pb_ref,
                       m_ref, vn_ref, *, tm, seq):
    w_sp = _spatial_masked_weights(w_s_ref, seq)
    ya_rows, yb_rows = [], []
    for r in _chunks(tm):
        y_a, vn = _branch_a_chunk(proj_ref, r, g_v_ref, w_sp, b_col_ref)
        vn_ref[r, :] = vn
        ya_rows.append(y_a.astype(BF16))
        yb_rows.append(_branch_b_out(o_ref[r, :], proj_ref[r, OFF_ZB:OFF_ZB + E_B],
                                     g_o_ref).astype(BF16))
    _merge(jnp.concatenate(ya_rows, axis=0), jnp.concatenate(yb_rows, axis=0),
           proj_ref, w_pa_ref, w_pb_ref, m_ref)


def _out_kernel(m_p_ref, m_s_ref, x_p_ref, x_s_ref, p_p_ref, p_s_ref,
                w_o_ref, g_post_ref, w_pg_ref, w_pe_ref, y_p_ref, y_s_ref, *, n_prompt_tiles):
    is_s = pl.program_id(0) >= n_prompt_tiles
    m = jnp.where(is_s, m_s_ref[...], m_p_ref[...])
    x = jnp.where(is_s, x_s_ref[...], x_p_ref[...])
    p_b = jnp.where(is_s, p_s_ref[...], p_p_ref[...]).astype(BF16)
    blocks = [slice(j * COL_BLOCK, (j + 1) * COL_BLOCK) for j in range(COL_BLOCKS)]
    mo = jnp.concatenate([_dot(m, w_o_ref[:, c]) for c in blocks], axis=1)
    h_res = x + _rms(mo, g_post_ref[...])
    h_b = h_res.astype(BF16)
    to_s = jnp.broadcast_to(is_s, (m.shape[0], COL_BLOCK))
    for c in blocks:
        pg = _sigmoid(_dot(h_b, w_pg_ref[:, c]))
        y = h_res[:, c] + pg * _dot(p_b, w_pe_ref[:, c])
        pltpu.store(y_p_ref.at[:, c], y, mask=jnp.logical_not(to_s))
        pltpu.store(y_s_ref.at[:, c], y, mask=to_s)


def _const_spec(shape):
    nd = len(shape)
    return pl.BlockSpec(shape, lambda *_: (0,) * nd, pipeline_mode=pl.Buffered(1))


def _mix_prompt(proj, proj_a, wts, n_batch, seq, tm):
    nt = seq // tm
    kern = functools.partial(_mix_prompt_kernel, tm=tm, seq=seq)
    row = lambda b, i: (b * nt + i, 0)
    names = ("g_v", "w_s", "b_col", "w_a2", "b_a", "g_o", "w_pa", "w_pb")
    w_args = [wts[n] for n in names]
    return pl.pallas_call(
        kern,
        grid=(n_batch, nt),
        in_specs=[
            pl.BlockSpec((tm, N_MAIN), row),
            pl.BlockSpec((tm, A_PAD), row),
        ] + [_const_spec(w.shape) for w in w_args],
        out_specs=[
            pl.BlockSpec((tm, D_MODEL), row),
            pl.BlockSpec((1, HB, DK, DV), lambda b, i: (b, 0, 0, 0)),
        ],
        out_shape=[
            jax.ShapeDtypeStruct((n_batch * seq, D_MODEL), BF16),
            jax.ShapeDtypeStruct((n_batch, HB, DK, DV), F32),
        ],
        scratch_shapes=[pltpu.VMEM((HB, DK, DV), F32)],
        compiler_params=pltpu.CompilerParams(
            dimension_semantics=("arbitrary", "arbitrary"),
            vmem_limit_bytes=VMEM_LIMIT),
        name="mix_prompt",
    )(proj, proj_a, *w_args)


def _mix_sample(proj, o_gla, wts, seq, tm):
    t = proj.shape[0]
    kern = functools.partial(_mix_sample_kernel, tm=tm, seq=seq)
    row = lambda i: (i, 0)
    names = ("g_v", "w_s_bd", "b_col_bd", "g_o", "w_pa", "w_pb")
    w_args = [wts[n] for n in names]
    return pl.pallas_call(
        kern,
        grid=(t // tm,),
        in_specs=[
            pl.BlockSpec((tm, N_MAIN), row),
            pl.BlockSpec((tm, E_B), row),
        ] + [_const_spec(w.shape) for w in w_args],
        out_specs=[
            pl.BlockSpec((tm, D_MODEL), row),
            pl.BlockSpec((tm, E_A), row),
        ],
        out_shape=[
            jax.ShapeDtypeStruct((t, D_MODEL), BF16),
            jax.ShapeDtypeStruct((t, E_A), F32),
        ],
        compiler_params=pltpu.CompilerParams(
            dimension_semantics=("arbitrary",),
            vmem_limit_bytes=VMEM_LIMIT),
        name="mix_sample",
    )(proj, o_gla, *w_args)


def _out_proj(m_p, m_s, x_p, x_s, p_p, p_s, wts, tm):
    n_p, n_s = x_p.shape[0] // tm, x_s.shape[0] // tm
    kern = functools.partial(_out_kernel, n_prompt_tiles=n_p)
    prompt = lambda i: (jnp.minimum(i, n_p - 1), 0)
    sample = lambda i: (jnp.maximum(i - n_p, 0), 0)
    prompt_spec = lambda width: pl.BlockSpec((tm, width), prompt)
    sample_spec = lambda width: pl.BlockSpec((tm, width), sample, pipeline_mode=pl.Buffered(1))
    names = ("w_o", "g_post", "w_pg", "w_pe")
    w_args = [wts[n] for n in names]
    return pl.pallas_call(
        kern,
        grid=(n_p + n_s,),
        in_specs=[
            prompt_spec(D_MODEL), sample_spec(D_MODEL),
            prompt_spec(D_MODEL), sample_spec(D_MODEL),
            prompt_spec(PLE_DIM), sample_spec(PLE_DIM),
        ] + [_const_spec(w.shape) for w in w_args],
        out_specs=[prompt_spec(D_MODEL), sample_spec(D_MODEL)],
        out_shape=[
            jax.ShapeDtypeStruct(x_p.shape, F32),
            jax.ShapeDtypeStruct(x_s.shape, F32),
        ],
        compiler_params=pltpu.CompilerParams(
            dimension_semantics=("arbitrary",),
            vmem_limit_bytes=VMEM_LIMIT),
        name="out_proj",
    )(m_p, m_s, x_p, x_s, p_p, p_s, *w_args)


def _layer_weights(i, dec_seq, g_pre, w_in, g_v, w_s, b_s, w_a2, b_a, g_o, g_post):
    w_a2_p = jnp.concatenate(
        [w_a2[i], jnp.zeros((A_PAD - GATE_RANK, K_B), w_a2.dtype)], axis=0).astype(BF16)
    reps = CHUNK // dec_seq
    return dict(
        g_pre=g_pre[i][None, :],
        w_in_t=w_in[i].T,
        g_v=g_v[i][None, :],
        w_s=w_s[i],
        b_col=b_s[i].T,
        w_s_bd=jnp.tile(w_s[i][:, :dec_seq, :dec_seq], (1, reps, reps)),
        b_col_bd=jnp.tile(b_s[i][:, :dec_seq].T, (reps, 1)),
        w_a2=w_a2_p,
        b_a=b_a[i][None, :],
        g_o=g_o[i].reshape(1, E_B),
        g_post=g_post[i][None, :],
    )


SIDE_WEIGHTS = ("w_pa", "w_pb", "w_o", "w_pg", "w_pe")

INPROJ_TM = 1024
MIX_TM = 512
OUT_TM = 512


def kernel(x_prompt, x_sample, p_prompt, p_sample, state_gla, g_pre, w_in, g_v, w_s, b_s,
           w_a2, b_a, g_o, w_pa, w_pb, w_o, g_post, w_pg, w_pe):
    depth = w_in.shape[0]
    n_batch, seq, _ = x_prompt.shape
    dec_batch, dec_seq, _ = x_sample.shape
    hp = x_prompt.reshape(n_batch * seq, D_MODEL)
    hs = x_sample.reshape(dec_batch * dec_seq, D_MODEL)
    sp_list, ss_list, cv_list = [], [], []
    for i in range(depth):
        wts = _layer_weights(i, dec_seq, g_pre, w_in, g_v, w_s, b_s, w_a2, b_a, g_o, g_post)
        side_f32 = dict(w_pa=w_pa[i], w_pb=w_pb[i], w_o=w_o[i], w_pg=w_pg[i], w_pe=w_pe[i])
        proj_s, proj_a_s, *w_bf = _inproj(hs, wts["g_pre"], wts["w_in_t"], tm=INPROJ_TM)
        proj_p, proj_a_p, *side = _inproj(hp, wts["g_pre"], wts["w_in_t"], tm=INPROJ_TM, w_bf=w_bf,
                                          cast=[side_f32[n] for n in SIDE_WEIGHTS])
        wts.update(zip(SIDE_WEIGHTS, side))
        o_s, s_s = _gla_sample(proj_s, proj_a_s, wts["w_a2"], wts["b_a"], state_gla[i],
                               nb=CHUNK // dec_seq, seq=dec_seq)
        m_p, s_p = _mix_prompt(proj_p, proj_a_p, wts, n_batch, seq, tm=MIX_TM)
        m_s, v_s = _mix_sample(proj_s, o_s, wts, dec_seq, tm=MIX_TM)
        hp, hs = _out_proj(m_p, m_s, hp, hs, p_prompt[i].reshape(n_batch * seq, PLE_DIM),
                           p_sample[i].reshape(dec_batch * dec_seq, PLE_DIM), wts, tm=OUT_TM)
        sp_list.append(s_p)
        ss_list.append(s_s)
        cv_list.append(v_s.reshape(dec_batch, dec_seq, E_A))
    stack = (lambda xs: xs[0][None]) if depth == 1 else jnp.stack
    return (hp.reshape(n_batch, seq, D_MODEL),
            hs.reshape(dec_batch, dec_seq, D_MODEL),
            stack(sp_list), stack(ss_list), stack(cv_list))
```

```python
import functools

import jax
import jax.numpy as jnp
from jax import lax
from jax.experimental import pallas as pl
from jax.experimental.pallas import tpu as pltpu

D_MODEL = 2048
E_A = D_MODEL // 2
HA = 4
DA = E_A // HA
CHUNK = 128
HB = 4
E_B = D_MODEL // 2
DV = E_B // HB
DK = DV // 2
K_B = HB * DK
GATE_RANK = 16
GATE_TAU = 16.0
GLA_SUB = 16
PLE_DIM = 256
EPS = 1e-6

LANES = 128
OFF_U = 0
OFF_V = OFF_U + E_A
OFF_Z = OFF_V + E_A
OFF_Q = OFF_Z + E_A
OFF_K = OFF_Q + K_B
OFF_VB = OFF_K + K_B
OFF_ZB = OFF_VB + E_B
OFF_MA = OFF_ZB + E_B
OFF_MB = OFF_MA + D_MODEL
N_MAIN = OFF_MB + D_MODEL
W_IN_A = OFF_MA
A_PAD = LANES
INPROJ_TN = 1024
INPROJ_TN_BF16 = 2048

F32 = jnp.float32
BF16 = jnp.bfloat16

VMEM_LIMIT = 56 * 1024 * 1024


def _dot(a, b):
    return jnp.dot(a, b, preferred_element_type=F32)


def _dot_nt(a, b):
    return lax.dot_general(a, b, (((1,), (1,)), ((), ())), preferred_element_type=F32)


def _sigmoid(x):
    return jax.nn.sigmoid(x)


def _rms(x, g):
    return x * lax.rsqrt(jnp.mean(x * x, axis=-1, keepdims=True) + EPS) * g


def _split_bf16(x):
    hi = x.astype(BF16)
    lo = (x - hi.astype(F32)).astype(BF16)
    return hi, lo


def _log_decay(a_lr, w_a2, b_a):
    pre = _dot(a_lr, w_a2) + b_a
    return (jnp.minimum(pre, 0.0) - jnp.log(1.0 + jnp.exp(-jnp.abs(pre)))) * (1.0 / GATE_TAU)


def _bcast_rows(b, idxs, seg):
    return jnp.concatenate(
        [jnp.broadcast_to(b[i:i + 1, :], (seg, b.shape[1])) for i in idxs], axis=0)


BF16_SUBLANES = 16


def _inproj_kernel(*refs, n_cast, emit_w):
    x_ref, g_ref, w_ref, w_a_ref = refs[:4]
    cast_in = refs[4:4 + n_cast]
    o_ref, o_a_ref = refs[4 + n_cast:6 + n_cast]
    cast_out = refs[6 + n_cast:6 + 2 * n_cast]
    rest = refs[6 + 2 * n_cast:]
    xn_ref = rest[-1]

    @pl.when(pl.program_id(1) == 0)
    def _():
        xn = _rms(x_ref[...], g_ref[...]).astype(BF16)
        xn_ref[...] = xn
        w_a = w_a_ref[...]
        if emit_w:
            w_a = w_a.astype(BF16)
            rest[1][...] = w_a
        o_a_ref[...] = _dot_nt(xn, w_a).astype(o_a_ref.dtype)

    for src, dst in zip(cast_in, cast_out):
        dst[...] = src[...].astype(dst.dtype)
    w = w_ref[...]
    if emit_w:
        w = w.astype(BF16)
        rest[0][...] = w
    o_ref[...] = _dot_nt(xn_ref[...], w).astype(o_ref.dtype)


def _inproj(x2d, g_pre, w_in_t, tm, w_bf=None, cast=()):
    t = x2d.shape[0]
    emit_w = w_bf is None
    tn = INPROJ_TN if emit_w else INPROJ_TN_BF16
    n_i, n_j = t // tm, N_MAIN // tn

    def w_rows(i, j):
        row0 = j * tn + jnp.where(j * tn >= OFF_MA, GATE_RANK, 0)
        return (pl.multiple_of(row0, GATE_RANK), 0)

    if emit_w:
        w_args = (w_in_t, w_in_t)
        w_specs = [pl.BlockSpec((pl.Element(tn), pl.Element(D_MODEL)), w_rows),
                   pl.BlockSpec((A_PAD, D_MODEL), lambda i, j: (W_IN_A // A_PAD, 0))]
        w_out_specs = [pl.BlockSpec((tn, D_MODEL), lambda i, j: (j, 0)),
                       pl.BlockSpec((A_PAD, D_MODEL), lambda i, j: (0, 0))]
        w_out_shapes = [jax.ShapeDtypeStruct((N_MAIN, D_MODEL), BF16),
                        jax.ShapeDtypeStruct((A_PAD, D_MODEL), BF16)]
        assert n_i == 1, "the bf16 weight copy is written once, by a single token tile"
    else:
        w_args = w_bf
        w_specs = [pl.BlockSpec((tn, D_MODEL), lambda i, j: (j, 0)),
                   pl.BlockSpec((A_PAD, D_MODEL), lambda i, j: (0, 0))]
        w_out_specs, w_out_shapes = [], []

    cast_specs = []
    for w in cast:
        per_i = w.shape[0] // n_i
        assert per_i * n_i == w.shape[0] and per_i % BF16_SUBLANES == 0
        n_sub = max(n for n in range(1, n_j + 1)
                    if per_i % n == 0 and (per_i // n) % BF16_SUBLANES == 0)
        cast_specs.append(pl.BlockSpec(
            (per_i // n_sub, w.shape[1]),
            lambda i, j, n_sub=n_sub: (i * n_sub + jnp.minimum(j, n_sub - 1), 0)))

    kern = functools.partial(_inproj_kernel, n_cast=len(cast), emit_w=emit_w)
    return pl.pallas_call(
        kern,
        grid=(n_i, n_j),
        in_specs=[
            pl.BlockSpec((tm, D_MODEL), lambda i, j: (i, 0)),
            pl.BlockSpec((1, D_MODEL), lambda i, j: (0, 0)),
        ] + w_specs + cast_specs,
        out_specs=[
            pl.BlockSpec((tm, tn), lambda i, j: (i, j)),
            pl.BlockSpec((tm, A_PAD), lambda i, j: (i, 0)),
        ] + cast_specs + w_out_specs,
        out_shape=[
            jax.ShapeDtypeStruct((t, N_MAIN), BF16),
            jax.ShapeDtypeStruct((t, A_PAD), BF16),
        ] + [jax.ShapeDtypeStruct(w.shape, BF16) for w in cast] + w_out_shapes,
        scratch_shapes=[pltpu.VMEM((tm, D_MODEL), BF16)],
        compiler_params=pltpu.CompilerParams(
            dimension_semantics=("arbitrary", "arbitrary"),
            vmem_limit_bytes=VMEM_LIMIT),
        name="inproj_f32w" if emit_w else "inproj",
    )(x2d, g_pre, *w_args, *cast)


def _gla_levels(c):
    levels = []
    blk = c
    while blk > GLA_SUB:
        per = blk // GLA_SUB
        levels.append((blk, [(j // per) * per + per // 2 - 1 for j in range(c // GLA_SUB)]))
        blk //= 2
    return levels


def _gla_masks(c):
    t = lax.broadcasted_iota(jnp.int32, (c, c), 0)
    s = lax.broadcasted_iota(jnp.int32, (c, c), 1)
    levels = _gla_levels(c)
    masks = []
    for blk, _ in levels:
        sh = blk.bit_length() - 1
        half = blk // 2
        same = (t >> sh) == (s >> sh)
        masks.append(same & ((t & (blk - 1)) >= half) & ((s & (blk - 1)) < half))
    sh = GLA_SUB.bit_length() - 1
    masks.append(((t >> sh) == (s >> sh)) & (s <= t))
    return masks


def _gla_chunk(q, k, vb, log_a, s_heads, tril_ones, masks):
    c = q.shape[0]
    hi, lo = _split_bf16(log_a)
    b = _dot(tril_ones, hi) + _dot(tril_ones, lo)
    n = c // GLA_SUB

    def rep(x):
        return _bcast_rows(x, list(range(n)), GLA_SUB)

    ends = jnp.concatenate([b[(j + 1) * GLA_SUB - 1:(j + 1) * GLA_SUB, :] for j in range(n)], axis=0)
    starts = jnp.concatenate([jnp.zeros((1, b.shape[1]), F32), ends[:n - 1]], axis=0)
    starts_r = rep(starts)
    q_e = q * jnp.exp(b - starts_r)
    k_e = k * jnp.exp(rep(ends) - b)

    qs, ks = [], []
    for _, ref_blocks in _gla_levels(c):
        ref = jnp.concatenate([ends[r:r + 1] for r in ref_blocks], axis=0)
        qs.append((q_e * rep(jnp.exp(jnp.minimum(starts - ref, 0.0)))).astype(BF16))
        ks.append((k_e * rep(jnp.exp(jnp.minimum(ref - ends, 0.0)))).astype(BF16))
    qs.append(q_e.astype(BF16))
    ks.append((k * jnp.exp(starts_r - b)).astype(BF16))

    q_in = (q_e * rep(jnp.exp(starts))).astype(BF16)
    k_out = k_e * rep(jnp.exp(ends[n - 1:n] - ends))

    outs, new_states = [], []
    for h in range(HB):
        kc = slice(h * DK, (h + 1) * DK)
        vc = slice(h * DV, (h + 1) * DV)
        att = jnp.zeros((c, c), F32)
        for ql, kl, m in zip(qs, ks, masks):
            att = jnp.where(m, _dot_nt(ql[:, kc], kl[:, kc]), att)
        v_h = vb[:, vc]
        s_h = s_heads[h]
        o_h = _dot(att.astype(BF16), v_h) + _dot(q_in[:, kc], s_h.astype(BF16))
        b_t = jnp.transpose(b[:, kc])
        decay = jnp.exp(jnp.broadcast_to(b_t[:, c - 1:c], (DK, DV)))
        k_t = jnp.transpose(k_out[:, kc]).astype(BF16)
        new_states.append(decay * s_h + _dot(k_t, v_h))
        outs.append(o_h)
    return jnp.concatenate(outs, axis=1), new_states


def _gla_decode_rows(q, k, vb, log_a, s0_ref, s1_ref, nb, seq):
    rows = nb * seq
    t = lax.broadcasted_iota(jnp.int32, (rows, rows), 0)
    s = lax.broadcasted_iota(jnp.int32, (rows, rows), 1)
    sh = seq.bit_length() - 1
    causal = ((t >> sh) == (s >> sh)) & (s <= t)
    tril_ones = jnp.where(causal, 1.0, 0.0).astype(BF16)
    hi, lo = _split_bf16(log_a)
    b = _dot(tril_ones, hi) + _dot(tril_ones, lo)

    q_t = q * jnp.exp(b)
    k_t = (k * jnp.exp(-b)).astype(BF16)
    b_last = _bcast_rows(b, [n * seq + seq - 1 for n in range(nb)], seq)
    k_out = k * jnp.exp(b_last - b)
    col = lax.broadcasted_iota(jnp.int32, (DK, rows), 1) >> sh

    outs = []
    for h in range(HB):
        kc = slice(h * DK, (h + 1) * DK)
        vc = slice(h * DV, (h + 1) * DV)
        v_h = vb[:, vc]
        att = jnp.where(causal, _dot_nt(q_t[:, kc].astype(BF16), k_t[:, kc]), 0.0)
        o_intra = _dot(att.astype(BF16), v_h)
        b_tr = jnp.transpose(b[:, kc])
        k_tr = jnp.transpose(k_out[:, kc])
        o_inter = []
        for n in range(nb):
            s_n = s0_ref[n, h]
            r0 = n * seq
            o_inter.append(_dot(q_t[r0:r0 + seq, kc], s_n))
            decay = jnp.exp(jnp.broadcast_to(b_tr[:, r0 + seq - 1:r0 + seq], (DK, DV)))
            k_n = jnp.where(col == n, k_tr, 0.0).astype(BF16)
            s1_ref[n, h] = decay * s_n + _dot(k_n, v_h)
        outs.append(o_intra + jnp.concatenate(o_inter, axis=0))
    return jnp.concatenate(outs, axis=1)


def _spatial_masked_weights(w_s_ref, seq):
    t = lax.broadcasted_iota(jnp.int32, (CHUNK, CHUNK), 0)
    s = lax.broadcasted_iota(jnp.int32, (CHUNK, CHUNK), 1)
    causal = s <= t
    if seq < CHUNK:
        sh = seq.bit_length() - 1
        causal = causal & ((t >> sh) == (s >> sh))
    return [jnp.where(causal, w_s_ref[g], 0.0).astype(BF16) for g in range(HA)]


def _branch_a_chunk(proj_ref, r, g_v_ref, w_sp, b_col_ref):
    vn = _rms(proj_ref[r, OFF_V:OFF_V + E_A].astype(F32), g_v_ref[...])
    vn_b = vn.astype(BF16)
    gate = jnp.concatenate(
        [_dot(w_sp[g], vn_b[:, g * DA:(g + 1) * DA]) + b_col_ref[:, g:g + 1]
         for g in range(HA)], axis=1)
    z = proj_ref[r, OFF_Z:OFF_Z + E_A].astype(F32)
    return proj_ref[r, OFF_U:OFF_U + E_A].astype(F32) * gate * (z * _sigmoid(z)), vn


def _branch_b_out(o, zb, g_o_ref):
    g_o = g_o_ref[...]
    on = jnp.concatenate(
        [_rms(o[:, h * DV:(h + 1) * DV], g_o[:, h * DV:(h + 1) * DV]) for h in range(HB)], axis=1)
    zb = zb.astype(F32)
    return on * (zb * _sigmoid(zb))


COL_BLOCKS = 4
COL_BLOCK = D_MODEL // COL_BLOCKS


def _merge(y_a, y_b, proj_ref, w_pa_ref, w_pb_ref, m_ref):
    for j in range(COL_BLOCKS):
        c = slice(j * COL_BLOCK, (j + 1) * COL_BLOCK)
        m_a = proj_ref[:, OFF_MA + c.start:OFF_MA + c.stop].astype(F32)
        m_b = proj_ref[:, OFF_MB + c.start:OFF_MB + c.stop].astype(F32)
        m_ref[:, c] = (_sigmoid(m_a) * _dot(y_a, w_pa_ref[:, c])
                       + _sigmoid(m_b) * _dot(y_b, w_pb_ref[:, c])).astype(m_ref.dtype)


def _chunks(tm):
    return [slice(c * CHUNK, (c + 1) * CHUNK) for c in range(tm // CHUNK)]


def _mix_prompt_kernel(proj_ref, proj_a_ref, g_v_ref, w_s_ref, b_col_ref, w_a2_ref, b_a_ref,
                       g_o_ref, w_pa_ref, w_pb_ref, m_ref, s_out_ref, s_ref, *, tm, seq):
    keep = pl.program_id(1) != 0
    states = [jnp.where(keep, s_ref[h], 0.0) for h in range(HB)]
    w_sp = _spatial_masked_weights(w_s_ref, seq)
    t = lax.broadcasted_iota(jnp.int32, (CHUNK, CHUNK), 0)
    s = lax.broadcasted_iota(jnp.int32, (CHUNK, CHUNK), 1)
    tril_ones = jnp.where(s <= t, 1.0, 0.0).astype(BF16)
    masks = _gla_masks(CHUNK)
    log_a = _log_decay(proj_a_ref[...], w_a2_ref[...], b_a_ref[...])
    ya_rows, yb_rows = [], []
    for r in _chunks(tm):
        ya_rows.append(_branch_a_chunk(proj_ref, r, g_v_ref, w_sp, b_col_ref)[0].astype(BF16))
        q = proj_ref[r, OFF_Q:OFF_Q + K_B].astype(F32) * (DK ** -0.5)
        k = proj_ref[r, OFF_K:OFF_K + K_B].astype(F32)
        vb = proj_ref[r, OFF_VB:OFF_VB + E_B]
        o_c, states = _gla_chunk(q, k, vb, log_a[r], states, tril_ones, masks)
        yb_rows.append(_branch_b_out(o_c, proj_ref[r, OFF_ZB:OFF_ZB + E_B], g_o_ref).astype(BF16))
    for h in range(HB):
        s_ref[h] = states[h]
        s_out_ref[0, h] = states[h]
    _merge(jnp.concatenate(ya_rows, axis=0), jnp.concatenate(yb_rows, axis=0),
           proj_ref, w_pa_ref, w_pb_ref, m_ref)


def _mix_decode_kernel(proj_ref, proj_a_ref, s0_ref, g_v_ref, w_s_ref, b_col_ref, w_a2_ref, b_a_ref,
                       g_o_ref, w_pa_ref, w_pb_ref, m_ref, vn_ref, s1_ref, *, nb, seq):
    rows = slice(0, nb * seq)
    log_a = _log_decay(proj_a_ref[...], w_a2_ref[...], b_a_ref[...])
    q = proj_ref[:, OFF_Q:OFF_Q + K_B].astype(F32) * (DK ** -0.5)
    k = proj_ref[:, OFF_K:OFF_K + K_B].astype(F32)
    o = _gla_decode_rows(q, k, proj_ref[:, OFF_VB:OFF_VB + E_B], log_a, s0_ref, s1_ref, nb, seq)
    y_a, vn = _branch_a_chunk(proj_ref, rows, g_v_ref, _spatial_masked_weights(w_s_ref, seq),
                              b_col_ref)
    vn_ref[...] = vn
    y_b = _branch_b_out(o, proj_ref[:, OFF_ZB:OFF_ZB + E_B], g_o_ref)
    _merge(y_a.astype(BF16), y_b.astype(BF16), proj_ref, w_pa_ref, w_pb_ref, m_ref)


def _out_kernel(m_p_ref, m_s_ref, x_p_ref, x_s_ref, p_p_ref, p_s_ref,
                w_o_ref, g_post_ref, w_pg_ref, w_pe_ref, y_p_ref, y_s_ref, *, n_prompt_tiles):
    is_s = pl.program_id(0) >= n_prompt_tiles
    m = jnp.where(is_s, m_s_ref[...], m_p_ref[...])
    x = jnp.where(is_s, x_s_ref[...], x_p_ref[...])
    p_b = jnp.where(is_s, p_s_ref[...], p_p_ref[...]).astype(BF16)
    blocks = [slice(j * COL_BLOCK, (j + 1) * COL_BLOCK) for j in range(COL_BLOCKS)]
    mo = jnp.concatenate([_dot(m, w_o_ref[:, c]) for c in blocks], axis=1)
    h_res = x + _rms(mo, g_post_ref[...])
    h_b = h_res.astype(BF16)
    to_s = jnp.broadcast_to(is_s, (m.shape[0], COL_BLOCK))
    for c in blocks:
        pg = _sigmoid(_dot(h_b, w_pg_ref[:, c]))
        y = h_res[:, c] + pg * _dot(p_b, w_pe_ref[:, c])
        pltpu.store(y_p_ref.at[:, c], y, mask=jnp.logical_not(to_s))
        pltpu.store(y_s_ref.at[:, c], y, mask=to_s)


def _const_spec(shape):
    nd = len(shape)
    return pl.BlockSpec(shape, lambda *_: (0,) * nd, pipeline_mode=pl.Buffered(1))


def _mix_prompt(proj, proj_a, wts, n_batch, seq, tm):
    nt = seq // tm
    kern = functools.partial(_mix_prompt_kernel, tm=tm, seq=seq)
    row = lambda b, i: (b * nt + i, 0)
    names = ("g_v", "w_s", "b_col", "w_a2", "b_a", "g_o", "w_pa", "w_pb")
    w_args = [wts[n] for n in names]
    return pl.pallas_call(
        kern,
        grid=(n_batch, nt),
        in_specs=[
            pl.BlockSpec((tm, N_MAIN), row),
            pl.BlockSpec((tm, A_PAD), row),
        ] + [_const_spec(w.shape) for w in w_args],
        out_specs=[
            pl.BlockSpec((tm, D_MODEL), row),
            pl.BlockSpec((1, HB, DK, DV), lambda b, i: (b, 0, 0, 0)),
        ],
        out_shape=[
            jax.ShapeDtypeStruct((n_batch * seq, D_MODEL), BF16),
            jax.ShapeDtypeStruct((n_batch, HB, DK, DV), F32),
        ],
        scratch_shapes=[pltpu.VMEM((HB, DK, DV), F32)],
        compiler_params=pltpu.CompilerParams(
            dimension_semantics=("arbitrary", "arbitrary"),
            vmem_limit_bytes=VMEM_LIMIT),
        name="mix_prompt",
    )(proj, proj_a, *w_args)


def _mix_decode(proj, proj_a, s0, wts, nb, seq):
    n_batch = s0.shape[0]
    rows = nb * seq
    assert rows == CHUNK
    kern = functools.partial(_mix_decode_kernel, nb=nb, seq=seq)
    row = lambda i: (i, 0)
    state = lambda i: (i, 0, 0, 0)
    names = ("g_v", "w_s_bd", "b_col_bd", "w_a2", "b_a", "g_o", "w_pa", "w_pb")
    w_args = [wts[n] for n in names]
    return pl.pallas_call(
        kern,
        grid=(n_batch // nb,),
        in_specs=[
            pl.BlockSpec((rows, N_MAIN), row),
            pl.BlockSpec((rows, A_PAD), row),
            pl.BlockSpec((nb, HB, DK, DV), state),
        ] + [_const_spec(w.shape) for w in w_args],
        out_specs=[
            pl.BlockSpec((rows, D_MODEL), row),
            pl.BlockSpec((rows, E_A), row),
            pl.BlockSpec((nb, HB, DK, DV), state),
        ],
        out_shape=[
            jax.ShapeDtypeStruct((n_batch * seq, D_MODEL), BF16),
            jax.ShapeDtypeStruct((n_batch * seq, E_A), F32),
            jax.ShapeDtypeStruct(s0.shape, F32),
        ],
        compiler_params=pltpu.CompilerParams(
            dimension_semantics=("arbitrary",),
            vmem_limit_bytes=VMEM_LIMIT),
        name="mix_decode",
    )(proj, proj_a, s0, *w_args)


def _out_proj(m_p, m_s, x_p, x_s, p_p, p_s, wts, tm):
    n_p, n_s = x_p.shape[0] // tm, x_s.shape[0] // tm
    kern = functools.partial(_out_kernel, n_prompt_tiles=n_p)
    prompt = lambda i: (jnp.minimum(i, n_p - 1), 0)
    sample = lambda i: (jnp.maximum(i - n_p, 0), 0)
    prompt_spec = lambda width: pl.BlockSpec((tm, width), prompt)
    sample_spec = lambda width: pl.BlockSpec((tm, width), sample, pipeline_mode=pl.Buffered(1))
    names = ("w_o", "g_post", "w_pg", "w_pe")
    w_args = [wts[n] for n in names]
    return pl.pallas_call(
        kern,
        grid=(n_p + n_s,),
        in_specs=[
            prompt_spec(D_MODEL), sample_spec(D_MODEL),
            prompt_spec(D_MODEL), sample_spec(D_MODEL),
            prompt_spec(PLE_DIM), sample_spec(PLE_DIM),
        ] + [_const_spec(w.shape) for w in w_args],
        out_specs=[prompt_spec(D_MODEL), sample_spec(D_MODEL)],
        out_shape=[
            jax.ShapeDtypeStruct(x_p.shape, F32),
            jax.ShapeDtypeStruct(x_s.shape, F32),
        ],
        compiler_params=pltpu.CompilerParams(
            dimension_semantics=("arbitrary",),
            vmem_limit_bytes=VMEM_LIMIT),
        name="out_proj",
    )(m_p, m_s, x_p, x_s, p_p, p_s, *w_args)


def _layer_weights(i, dec_seq, g_pre, w_in, g_v, w_s, b_s, w_a2, b_a, g_o, g_post):
    w_a2_p = jnp.concatenate(
        [w_a2[i], jnp.zeros((A_PAD - GATE_RANK, K_B), w_a2.dtype)], axis=0).astype(BF16)
    reps = CHUNK // dec_seq
    return dict(
        g_pre=g_pre[i][None, :],
        w_in_t=w_in[i].T,
        g_v=g_v[i][None, :],
        w_s=w_s[i],
        b_col=b_s[i].T,
        w_s_bd=jnp.tile(w_s[i][:, :dec_seq, :dec_seq], (1, reps, reps)),
        b_col_bd=jnp.tile(b_s[i][:, :dec_seq].T, (reps, 1)),
        w_a2=w_a2_p,
        b_a=b_a[i][None, :],
        g_o=g_o[i].reshape(1, E_B),
        g_post=g_post[i][None, :],
    )


SIDE_WEIGHTS = ("w_pa", "w_pb", "w_o", "w_pg", "w_pe")

INPROJ_TM = 1024
MIX_TM = 512
OUT_TM = 512


def kernel(x_prompt, x_sample, p_prompt, p_sample, state_gla, g_pre, w_in, g_v, w_s, b_s,
           w_a2, b_a, g_o, w_pa, w_pb, w_o, g_post, w_pg, w_pe):
    depth = w_in.shape[0]
    n_batch, seq, _ = x_prompt.shape
    dec_batch, dec_seq, _ = x_sample.shape
    hp = x_prompt.reshape(n_batch * seq, D_MODEL)
    hs = x_sample.reshape(dec_batch * dec_seq, D_MODEL)
    sp_list, ss_list, cv_list = [], [], []
    for i in range(depth):
        wts = _layer_weights(i, dec_seq, g_pre, w_in, g_v, w_s, b_s, w_a2, b_a, g_o, g_post)
        side_f32 = dict(w_pa=w_pa[i], w_pb=w_pb[i], w_o=w_o[i], w_pg=w_pg[i], w_pe=w_pe[i])
        proj_s, proj_a_s, *w_bf = _inproj(hs, wts["g_pre"], wts["w_in_t"], tm=INPROJ_TM)
        proj_p, proj_a_p, *side = _inproj(hp, wts["g_pre"], wts["w_in_t"], tm=INPROJ_TM, w_bf=w_bf,
                                          cast=[side_f32[n] for n in SIDE_WEIGHTS])
        wts.update(zip(SIDE_WEIGHTS, side))
        m_s, v_s, s_s = _mix_decode(proj_s, proj_a_s, state_gla[i], wts,
                                    nb=CHUNK // dec_seq, seq=dec_seq)
        m_p, s_p = _mix_prompt(proj_p, proj_a_p, wts, n_batch, seq, tm=MIX_TM)
        hp, hs = _out_proj(m_p, m_s, hp, hs, p_prompt[i].reshape(n_batch * seq, PLE_DIM),
                           p_sample[i].reshape(dec_batch * dec_seq, PLE_DIM), wts, tm=OUT_TM)
        sp_list.append(s_p)
        ss_list.append(s_s)
        cv_list.append(v_s.reshape(dec_batch, dec_seq, E_A))
    stack = (lambda xs: xs[0][None]) if depth == 1 else jnp.stack
    return (hp.reshape(n_batch, seq, D_MODEL),
            hs.reshape(dec_batch, dec_seq, D_MODEL),
            stack(sp_list), stack(ss_list), stack(cv_list))
```

```python
import functools

import jax
import jax.numpy as jnp
from jax import lax
from jax.experimental import pallas as pl
from jax.experimental.pallas import tpu as pltpu

D_MODEL = 2048
E_A = D_MODEL // 2
HA = 4
DA = E_A // HA
CHUNK = 128
HB = 4
E_B = D_MODEL // 2
DV = E_B // HB
DK = DV // 2
K_B = HB * DK
GATE_RANK = 16
GATE_TAU = 16.0
GLA_SUB = 16
PLE_DIM = 256
EPS = 1e-6

LANES = 128
OFF_U = 0
OFF_V = OFF_U + E_A
OFF_Z = OFF_V + E_A
OFF_Q = OFF_Z + E_A
OFF_K = OFF_Q + K_B
OFF_VB = OFF_K + K_B
OFF_ZB = OFF_VB + E_B
OFF_MA = OFF_ZB + E_B
OFF_MB = OFF_MA + D_MODEL
N_MAIN = OFF_MB + D_MODEL
W_IN_A = OFF_MA
A_PAD = LANES
INPROJ_TN = 1024
INPROJ_TN_BF16 = 2048

F32 = jnp.float32
BF16 = jnp.bfloat16

VMEM_LIMIT = 56 * 1024 * 1024


def _dot(a, b):
    return jnp.dot(a, b, preferred_element_type=F32)


def _dot_nt(a, b):
    return lax.dot_general(a, b, (((1,), (1,)), ((), ())), preferred_element_type=F32)


def _sigmoid(x):
    return jax.nn.sigmoid(x)


def _rms(x, g):
    return x * lax.rsqrt(jnp.mean(x * x, axis=-1, keepdims=True) + EPS) * g


def _split_bf16(x):
    hi = x.astype(BF16)
    lo = (x - hi.astype(F32)).astype(BF16)
    return hi, lo


def _log_decay(a_lr, w_a2, b_a):
    pre = _dot(a_lr, w_a2) + b_a
    return (jnp.minimum(pre, 0.0) - jnp.log(1.0 + jnp.exp(-jnp.abs(pre)))) * (1.0 / GATE_TAU)


def _bcast_rows(b, idxs, seg):
    return jnp.concatenate(
        [jnp.broadcast_to(b[i:i + 1, :], (seg, b.shape[1])) for i in idxs], axis=0)


BF16_SUBLANES = 16


def _inproj_kernel(*refs, n_cast, emit_w):
    x_ref, g_ref, w_ref, w_a_ref = refs[:4]
    cast_in = refs[4:4 + n_cast]
    o_ref, o_a_ref = refs[4 + n_cast:6 + n_cast]
    cast_out = refs[6 + n_cast:6 + 2 * n_cast]
    rest = refs[6 + 2 * n_cast:]
    xn_ref = rest[-1]

    @pl.when(pl.program_id(1) == 0)
    def _():
        xn = _rms(x_ref[...], g_ref[...]).astype(BF16)
        xn_ref[...] = xn
        w_a = w_a_ref[...]
        if emit_w:
            w_a = w_a.astype(BF16)
            rest[1][...] = w_a
        o_a_ref[...] = _dot_nt(xn, w_a).astype(o_a_ref.dtype)

    for src, dst in zip(cast_in, cast_out):
        dst[...] = src[...].astype(dst.dtype)
    w = w_ref[...]
    if emit_w:
        w = w.astype(BF16)
        rest[0][...] = w
    o_ref[...] = _dot_nt(xn_ref[...], w).astype(o_ref.dtype)


def _inproj(x2d, g_pre, w_in_t, tm, w_bf=None, cast=()):
    t = x2d.shape[0]
    emit_w = w_bf is None
    tn = INPROJ_TN if emit_w else INPROJ_TN_BF16
    n_i, n_j = t // tm, N_MAIN // tn

    def w_rows(i, j):
        row0 = j * tn + jnp.where(j * tn >= OFF_MA, GATE_RANK, 0)
        return (pl.multiple_of(row0, GATE_RANK), 0)

    if emit_w:
        w_args = (w_in_t, w_in_t)
        w_specs = [pl.BlockSpec((pl.Element(tn), pl.Element(D_MODEL)), w_rows),
                   pl.BlockSpec((A_PAD, D_MODEL), lambda i, j: (W_IN_A // A_PAD, 0))]
        w_out_specs = [pl.BlockSpec((tn, D_MODEL), lambda i, j: (j, 0)),
                       pl.BlockSpec((A_PAD, D_MODEL), lambda i, j: (0, 0))]
        w_out_shapes = [jax.ShapeDtypeStruct((N_MAIN, D_MODEL), BF16),
                        jax.ShapeDtypeStruct((A_PAD, D_MODEL), BF16)]
        assert n_i == 1, "the bf16 weight copy is written once, by a single token tile"
    else:
        w_args = w_bf
        w_specs = [pl.BlockSpec((tn, D_MODEL), lambda i, j: (j, 0)),
                   pl.BlockSpec((A_PAD, D_MODEL), lambda i, j: (0, 0))]
        w_out_specs, w_out_shapes = [], []

    cast_specs = []
    for w in cast:
        per_i = w.shape[0] // n_i
        assert per_i * n_i == w.shape[0] and per_i % BF16_SUBLANES == 0
        n_sub = max(n for n in range(1, n_j + 1)
                    if per_i % n == 0 and (per_i // n) % BF16_SUBLANES == 0)
        cast_specs.append(pl.BlockSpec(
            (per_i // n_sub, w.shape[1]),
            lambda i, j, n_sub=n_sub: (i * n_sub + jnp.minimum(j, n_sub - 1), 0)))

    kern = functools.partial(_inproj_kernel, n_cast=len(cast), emit_w=emit_w)
    return pl.pallas_call(
        kern,
        grid=(n_i, n_j),
        in_specs=[
            pl.BlockSpec((tm, D_MODEL), lambda i, j: (i, 0)),
            pl.BlockSpec((1, D_MODEL), lambda i, j: (0, 0)),
        ] + w_specs + cast_specs,
        out_specs=[
            pl.BlockSpec((tm, tn), lambda i, j: (i, j)),
            pl.BlockSpec((tm, A_PAD), lambda i, j: (i, 0)),
        ] + cast_specs + w_out_specs,
        out_shape=[
            jax.ShapeDtypeStruct((t, N_MAIN), BF16),
            jax.ShapeDtypeStruct((t, A_PAD), BF16),
        ] + [jax.ShapeDtypeStruct(w.shape, BF16) for w in cast] + w_out_shapes,
        scratch_shapes=[pltpu.VMEM((tm, D_MODEL), BF16)],
        compiler_params=pltpu.CompilerParams(
            dimension_semantics=("arbitrary", "arbitrary"),
            vmem_limit_bytes=VMEM_LIMIT),
        name="inproj_f32w" if emit_w else "inproj",
    )(x2d, g_pre, *w_args, *cast)


def _gla_levels(c):
    levels = []
    blk = c
    while blk > GLA_SUB:
        per = blk // GLA_SUB
        levels.append((blk, [(j // per) * per + per // 2 - 1 for j in range(c // GLA_SUB)]))
        blk //= 2
    return levels


def _gla_masks(c):
    t = lax.broadcasted_iota(jnp.int32, (c, c), 0)
    s = lax.broadcasted_iota(jnp.int32, (c, c), 1)
    levels = _gla_levels(c)
    masks = []
    for blk, _ in levels:
        sh = blk.bit_length() - 1
        half = blk // 2
        same = (t >> sh) == (s >> sh)
        masks.append(same & ((t & (blk - 1)) >= half) & ((s & (blk - 1)) < half))
    sh = GLA_SUB.bit_length() - 1
    masks.append(((t >> sh) == (s >> sh)) & (s <= t))
    return masks


def _gla_chunk(q, k, vb, log_a, s_heads, tril_ones, masks):
    c = q.shape[0]
    hi, lo = _split_bf16(log_a)
    b = _dot(tril_ones, hi) + _dot(tril_ones, lo)
    n = c // GLA_SUB

    def rep(x):
        return _bcast_rows(x, list(range(n)), GLA_SUB)

    ends = jnp.concatenate([b[(j + 1) * GLA_SUB - 1:(j + 1) * GLA_SUB, :] for j in range(n)], axis=0)
    starts = jnp.concatenate([jnp.zeros((1, b.shape[1]), F32), ends[:n - 1]], axis=0)
    starts_r = rep(starts)
    q_e = q * jnp.exp(b - starts_r)
    k_e = k * jnp.exp(rep(ends) - b)

    qs, ks = [], []
    for _, ref_blocks in _gla_levels(c):
        ref = jnp.concatenate([ends[r:r + 1] for r in ref_blocks], axis=0)
        qs.append((q_e * rep(jnp.exp(jnp.minimum(starts - ref, 0.0)))).astype(BF16))
        ks.append((k_e * rep(jnp.exp(jnp.minimum(ref - ends, 0.0)))).astype(BF16))
    qs.append(q_e.astype(BF16))
    ks.append((k * jnp.exp(starts_r - b)).astype(BF16))

    q_in = (q_e * rep(jnp.exp(starts))).astype(BF16)
    k_out = k_e * rep(jnp.exp(ends[n - 1:n] - ends))

    outs, new_states = [], []
    for h in range(HB):
        kc = slice(h * DK, (h + 1) * DK)
        vc = slice(h * DV, (h + 1) * DV)
        att = jnp.zeros((c, c), F32)
        for ql, kl, m in zip(qs, ks, masks):
            att = jnp.where(m, _dot_nt(ql[:, kc], kl[:, kc]), att)
        v_h = vb[:, vc]
        s_h = s_heads[h]
        o_h = _dot(att.astype(BF16), v_h) + _dot(q_in[:, kc], s_h.astype(BF16))
        b_t = jnp.transpose(b[:, kc])
        decay = jnp.broadcast_to(jnp.exp(b_t[:, c - 1:c]), (DK, DV))
        k_t = jnp.transpose(k_out[:, kc]).astype(BF16)
        new_states.append(decay * s_h + _dot(k_t, v_h))
        outs.append(o_h)
    return jnp.concatenate(outs, axis=1), new_states


def _gla_decode_rows(q, k, vb, log_a, s0_ref, s1_ref, nb, seq):
    rows = nb * seq
    t = lax.broadcasted_iota(jnp.int32, (rows, rows), 0)
    s = lax.broadcasted_iota(jnp.int32, (rows, rows), 1)
    sh = seq.bit_length() - 1
    causal = ((t >> sh) == (s >> sh)) & (s <= t)
    tril_ones = jnp.where(causal, 1.0, 0.0).astype(BF16)
    hi, lo = _split_bf16(log_a)
    b = _dot(tril_ones, hi) + _dot(tril_ones, lo)

    q_t = q * jnp.exp(b)
    k_t = (k * jnp.exp(-b)).astype(BF16)
    b_last = _bcast_rows(b, [n * seq + seq - 1 for n in range(nb)], seq)
    k_out = k * jnp.exp(b_last - b)
    col = lax.broadcasted_iota(jnp.int32, (DK, rows), 1) >> sh

    outs = []
    for h in range(HB):
        kc = slice(h * DK, (h + 1) * DK)
        vc = slice(h * DV, (h + 1) * DV)
        v_h = vb[:, vc]
        att = jnp.where(causal, _dot_nt(q_t[:, kc].astype(BF16), k_t[:, kc]), 0.0)
        o_intra = _dot(att.astype(BF16), v_h)
        decay_tr = jnp.exp(jnp.transpose(b[:, kc]))
        k_tr = jnp.transpose(k_out[:, kc])
        o_inter = []
        for n in range(nb):
            s_n = s0_ref[n, h]
            r0 = n * seq
            o_inter.append(_dot(q_t[r0:r0 + seq, kc], s_n))
            decay = jnp.broadcast_to(decay_tr[:, r0 + seq - 1:r0 + seq], (DK, DV))
            k_n = jnp.where(col == n, k_tr, 0.0).astype(BF16)
            s1_ref[n, h] = decay * s_n + _dot(k_n, v_h)
        outs.append(o_intra + jnp.concatenate(o_inter, axis=0))
    return jnp.concatenate(outs, axis=1)


def _spatial_masked_weights(w_s_ref, seq):
    t = lax.broadcasted_iota(jnp.int32, (CHUNK, CHUNK), 0)
    s = lax.broadcasted_iota(jnp.int32, (CHUNK, CHUNK), 1)
    causal = s <= t
    if seq < CHUNK:
        sh = seq.bit_length() - 1
        causal = causal & ((t >> sh) == (s >> sh))
    return [jnp.where(causal, w_s_ref[g], 0.0).astype(BF16) for g in range(HA)]


def _branch_a_chunk(proj_ref, r, g_v_ref, w_sp, b_col_ref):
    vn = _rms(proj_ref[r, OFF_V:OFF_V + E_A].astype(F32), g_v_ref[...])
    vn_b = vn.astype(BF16)
    gate = jnp.concatenate(
        [_dot(w_sp[g], vn_b[:, g * DA:(g + 1) * DA]) + b_col_ref[:, g:g + 1]
         for g in range(HA)], axis=1)
    z = proj_ref[r, OFF_Z:OFF_Z + E_A].astype(F32)
    return proj_ref[r, OFF_U:OFF_U + E_A].astype(F32) * gate * (z * _sigmoid(z)), vn


def _branch_b_out(o, zb, g_o_ref):
    g_o = g_o_ref[...]
    on = jnp.concatenate(
        [_rms(o[:, h * DV:(h + 1) * DV], g_o[:, h * DV:(h + 1) * DV]) for h in range(HB)], axis=1)
    zb = zb.astype(F32)
    return on * (zb * _sigmoid(zb))


COL_BLOCKS = 4
COL_BLOCK = D_MODEL // COL_BLOCKS


def _merge(y_a, y_b, proj_ref, w_pa_ref, w_pb_ref, m_ref):
    for j in range(COL_BLOCKS):
        c = slice(j * COL_BLOCK, (j + 1) * COL_BLOCK)
        m_a = proj_ref[:, OFF_MA + c.start:OFF_MA + c.stop].astype(F32)
        m_b = proj_ref[:, OFF_MB + c.start:OFF_MB + c.stop].astype(F32)
        m_ref[:, c] = (_sigmoid(m_a) * _dot(y_a, w_pa_ref[:, c])
                       + _sigmoid(m_b) * _dot(y_b, w_pb_ref[:, c])).astype(m_ref.dtype)


def _chunks(tm):
    return [slice(c * CHUNK, (c + 1) * CHUNK) for c in range(tm // CHUNK)]


def _mix_prompt_kernel(proj_ref, proj_a_ref, g_v_ref, w_s_ref, b_col_ref, w_a2_ref, b_a_ref,
                       g_o_ref, w_pa_ref, w_pb_ref, m_ref, s_out_ref, s_ref, *, tm, seq):
    keep = pl.program_id(1) != 0
    states = [jnp.where(keep, s_ref[h], 0.0) for h in range(HB)]
    w_sp = _spatial_masked_weights(w_s_ref, seq)
    t = lax.broadcasted_iota(jnp.int32, (CHUNK, CHUNK), 0)
    s = lax.broadcasted_iota(jnp.int32, (CHUNK, CHUNK), 1)
    tril_ones = jnp.where(s <= t, 1.0, 0.0).astype(BF16)
    masks = _gla_masks(CHUNK)
    log_a = _log_decay(proj_a_ref[...], w_a2_ref[...], b_a_ref[...])
    ya_rows, yb_rows = [], []
    for r in _chunks(tm):
        ya_rows.append(_branch_a_chunk(proj_ref, r, g_v_ref, w_sp, b_col_ref)[0].astype(BF16))
        q = proj_ref[r, OFF_Q:OFF_Q + K_B].astype(F32) * (DK ** -0.5)
        k = proj_ref[r, OFF_K:OFF_K + K_B].astype(F32)
        vb = proj_ref[r, OFF_VB:OFF_VB + E_B]
        o_c, states = _gla_chunk(q, k, vb, log_a[r], states, tril_ones, masks)
        yb_rows.append(_branch_b_out(o_c, proj_ref[r, OFF_ZB:OFF_ZB + E_B], g_o_ref).astype(BF16))
    for h in range(HB):
        s_ref[h] = states[h]
        s_out_ref[0, h] = states[h]
    _merge(jnp.concatenate(ya_rows, axis=0), jnp.concatenate(yb_rows, axis=0),
           proj_ref, w_pa_ref, w_pb_ref, m_ref)


def _mix_decode_kernel(proj_ref, proj_a_ref, s0_ref, g_v_ref, w_s_ref, b_col_ref, w_a2_ref, b_a_ref,
                       g_o_ref, w_pa_ref, w_pb_ref, m_ref, vn_ref, s1_ref, *, nb, seq):
    rows = slice(0, nb * seq)
    log_a = _log_decay(proj_a_ref[...], w_a2_ref[...], b_a_ref[...])
    q = proj_ref[:, OFF_Q:OFF_Q + K_B].astype(F32) * (DK ** -0.5)
    k = proj_ref[:, OFF_K:OFF_K + K_B].astype(F32)
    o = _gla_decode_rows(q, k, proj_ref[:, OFF_VB:OFF_VB + E_B], log_a, s0_ref, s1_ref, nb, seq)
    y_a, vn = _branch_a_chunk(proj_ref, rows, g_v_ref, _spatial_masked_weights(w_s_ref, seq),
                              b_col_ref)
    vn_ref[...] = vn
    y_b = _branch_b_out(o, proj_ref[:, OFF_ZB:OFF_ZB + E_B], g_o_ref)
    _merge(y_a.astype(BF16), y_b.astype(BF16), proj_ref, w_pa_ref, w_pb_ref, m_ref)


def _out_kernel(m_p_ref, m_s_ref, x_p_ref, x_s_ref, p_p_ref, p_s_ref,
                w_o_ref, g_post_ref, w_pg_ref, w_pe_ref, y_p_ref, y_s_ref, *, n_prompt_tiles):
    is_s = pl.program_id(0) >= n_prompt_tiles
    m = jnp.where(is_s, m_s_ref[...], m_p_ref[...])
    x = jnp.where(is_s, x_s_ref[...], x_p_ref[...])
    p_b = jnp.where(is_s, p_s_ref[...], p_p_ref[...]).astype(BF16)
    blocks = [slice(j * COL_BLOCK, (j + 1) * COL_BLOCK) for j in range(COL_BLOCKS)]
    mo = jnp.concatenate([_dot(m, w_o_ref[:, c]) for c in blocks], axis=1)
    h_res = x + _rms(mo, g_post_ref[...])
    h_b = h_res.astype(BF16)
    to_s = jnp.broadcast_to(is_s, (m.shape[0], COL_BLOCK))
    for c in blocks:
        pg = _sigmoid(_dot(h_b, w_pg_ref[:, c]))
        y = h_res[:, c] + pg * _dot(p_b, w_pe_ref[:, c])
        pltpu.store(y_p_ref.at[:, c], y, mask=jnp.logical_not(to_s))
        pltpu.store(y_s_ref.at[:, c], y, mask=to_s)


def _const_spec(shape):
    nd = len(shape)
    return pl.BlockSpec(shape, lambda *_: (0,) * nd, pipeline_mode=pl.Buffered(1))


def _mix_prompt(proj, proj_a, wts, n_batch, seq, tm):
    nt = seq // tm
    kern = functools.partial(_mix_prompt_kernel, tm=tm, seq=seq)
    row = lambda b, i: (b * nt + i, 0)
    names = ("g_v", "w_s", "b_col", "w_a2", "b_a", "g_o", "w_pa", "w_pb")
    w_args = [wts[n] for n in names]
    return pl.pallas_call(
        kern,
        grid=(n_batch, nt),
        in_specs=[
            pl.BlockSpec((tm, N_MAIN), row),
            pl.BlockSpec((tm, A_PAD), row),
        ] + [_const_spec(w.shape) for w in w_args],
        out_specs=[
            pl.BlockSpec((tm, D_MODEL), row),
            pl.BlockSpec((1, HB, DK, DV), lambda b, i: (b, 0, 0, 0)),
        ],
        out_shape=[
            jax.ShapeDtypeStruct((n_batch * seq, D_MODEL), BF16),
            jax.ShapeDtypeStruct((n_batch, HB, DK, DV), F32),
        ],
        scratch_shapes=[pltpu.VMEM((HB, DK, DV), F32)],
        compiler_params=pltpu.CompilerParams(
            dimension_semantics=("arbitrary", "arbitrary"),
            vmem_limit_bytes=VMEM_LIMIT),
        name="mix_prompt",
    )(proj, proj_a, *w_args)


def _mix_decode(proj, proj_a, s0, wts, nb, seq):
    n_batch = s0.shape[0]
    rows = nb * seq
    assert rows == CHUNK
    kern = functools.partial(_mix_decode_kernel, nb=nb, seq=seq)
    row = lambda i: (i, 0)
    state = lambda i: (i, 0, 0, 0)
    names = ("g_v", "w_s_bd", "b_col_bd", "w_a2", "b_a", "g_o", "w_pa", "w_pb")
    w_args = [wts[n] for n in names]
    return pl.pallas_call(
        kern,
        grid=(n_batch // nb,),
        in_specs=[
            pl.BlockSpec((rows, N_MAIN), row),
            pl.BlockSpec((rows, A_PAD), row),
            pl.BlockSpec((nb, HB, DK, DV), state),
        ] + [_const_spec(w.shape) for w in w_args],
        out_specs=[
            pl.BlockSpec((rows, D_MODEL), row),
            pl.BlockSpec((rows, E_A), row),
            pl.BlockSpec((nb, HB, DK, DV), state),
        ],
        out_shape=[
            jax.ShapeDtypeStruct((n_batch * seq, D_MODEL), BF16),
            jax.ShapeDtypeStruct((n_batch * seq, E_A), F32),
            jax.ShapeDtypeStruct(s0.shape, F32),
        ],
        compiler_params=pltpu.CompilerParams(
            dimension_semantics=("arbitrary",),
            vmem_limit_bytes=VMEM_LIMIT),
        name="mix_decode",
    )(proj, proj_a, s0, *w_args)


def _out_proj(m_p, m_s, x_p, x_s, p_p, p_s, wts, tm):
    n_p, n_s = x_p.shape[0] // tm, x_s.shape[0] // tm
    kern = functools.partial(_out_kernel, n_prompt_tiles=n_p)
    prompt = lambda i: (jnp.minimum(i, n_p - 1), 0)
    sample = lambda i: (jnp.maximum(i - n_p, 0), 0)
    prompt_spec = lambda width: pl.BlockSpec((tm, width), prompt)
    sample_spec = lambda width: pl.BlockSpec((tm, width), sample, pipeline_mode=pl.Buffered(1))
    names = ("w_o", "g_post", "w_pg", "w_pe")
    w_args = [wts[n] for n in names]
    return pl.pallas_call(
        kern,
        grid=(n_p + n_s,),
        in_specs=[
            prompt_spec(D_MODEL), sample_spec(D_MODEL),
            prompt_spec(D_MODEL), sample_spec(D_MODEL),
            prompt_spec(PLE_DIM), sample_spec(PLE_DIM),
        ] + [_const_spec(w.shape) for w in w_args],
        out_specs=[prompt_spec(D_MODEL), sample_spec(D_MODEL)],
        out_shape=[
            jax.ShapeDtypeStruct(x_p.shape, F32),
            jax.ShapeDtypeStruct(x_s.shape, F32),
        ],
        compiler_params=pltpu.CompilerParams(
            dimension_semantics=("arbitrary",),
            vmem_limit_bytes=VMEM_LIMIT),
        name="out_proj",
    )(m_p, m_s, x_p, x_s, p_p, p_s, *w_args)


def _layer_weights(i, dec_seq, g_pre, w_in, g_v, w_s, b_s, w_a2, b_a, g_o, g_post):
    w_a2_p = jnp.concatenate(
        [w_a2[i], jnp.zeros((A_PAD - GATE_RANK, K_B), w_a2.dtype)], axis=0).astype(BF16)
    reps = CHUNK // dec_seq
    return dict(
        g_pre=g_pre[i][None, :],
        w_in_t=w_in[i].T,
        g_v=g_v[i][None, :],
        w_s=w_s[i],
        b_col=b_s[i].T,
        w_s_bd=jnp.tile(w_s[i][:, :dec_seq, :dec_seq], (1, reps, reps)),
        b_col_bd=jnp.tile(b_s[i][:, :dec_seq].T, (reps, 1)),
        w_a2=w_a2_p,
        b_a=b_a[i][None, :],
        g_o=g_o[i].reshape(1, E_B),
        g_post=g_post[i][None, :],
    )


SIDE_WEIGHTS = ("w_pa", "w_pb", "w_o", "w_pg", "w_pe")

INPROJ_TM = 1024
MIX_TM = 512
OUT_TM = 512


def kernel(x_prompt, x_sample, p_prompt, p_sample, state_gla, g_pre, w_in, g_v, w_s, b_s,
           w_a2, b_a, g_o, w_pa, w_pb, w_o, g_post, w_pg, w_pe):
    depth = w_in.shape[0]
    n_batch, seq, _ = x_prompt.shape
    dec_batch, dec_seq, _ = x_sample.shape
    hp = x_prompt.reshape(n_batch * seq, D_MODEL)
    hs = x_sample.reshape(dec_batch * dec_seq, D_MODEL)
    sp_list, ss_list, cv_list = [], [], []
    for i in range(depth):
        wts = _layer_weights(i, dec_seq, g_pre, w_in, g_v, w_s, b_s, w_a2, b_a, g_o, g_post)
        side_f32 = dict(w_pa=w_pa[i], w_pb=w_pb[i], w_o=w_o[i], w_pg=w_pg[i], w_pe=w_pe[i])
        proj_s, proj_a_s, *w_bf = _inproj(hs, wts["g_pre"], wts["w_in_t"], tm=INPROJ_TM)
        proj_p, proj_a_p, *side = _inproj(hp, wts["g_pre"], wts["w_in_t"], tm=INPROJ_TM, w_bf=w_bf,
                                          cast=[side_f32[n] for n in SIDE_WEIGHTS])
        wts.update(zip(SIDE_WEIGHTS, side))
        m_s, v_s, s_s = _mix_decode(proj_s, proj_a_s, state_gla[i], wts,
                                    nb=CHUNK // dec_seq, seq=dec_seq)
        m_p, s_p = _mix_prompt(proj_p, proj_a_p, wts, n_batch, seq, tm=MIX_TM)
        hp, hs = _out_proj(m_p, m_s, hp, hs, p_prompt[i].reshape(n_batch * seq, PLE_DIM),
                           p_sample[i].reshape(dec_batch * dec_seq, PLE_DIM), wts, tm=OUT_TM)
        sp_list.append(s_p)
        ss_list.append(s_s)
        cv_list.append(v_s.reshape(dec_batch, dec_seq, E_A))
    stack = (lambda xs: xs[0][None]) if depth == 1 else jnp.stack
    return (hp.reshape(n_batch, seq, D_MODEL),
            hs.reshape(dec_batch, dec_seq, D_MODEL),
            stack(sp_list), stack(ss_list), stack(cv_list))
```

```python
import functools

import jax
import jax.numpy as jnp
from jax import lax
from jax.experimental import pallas as pl
from jax.experimental.pallas import tpu as pltpu

D_MODEL = 2048
E_A = D_MODEL // 2
HA = 4
DA = E_A // HA
CHUNK = 128
HB = 4
E_B = D_MODEL // 2
DV = E_B // HB
DK = DV // 2
K_B = HB * DK
GATE_RANK = 16
GATE_TAU = 16.0
GLA_SUB = 16
PLE_DIM = 256
EPS = 1e-6

LANES = 128
OFF_U = 0
OFF_V = OFF_U + E_A
OFF_Z = OFF_V + E_A
OFF_Q = OFF_Z + E_A
OFF_K = OFF_Q + K_B
OFF_VB = OFF_K + K_B
OFF_ZB = OFF_VB + E_B
OFF_MA = OFF_ZB + E_B
OFF_MB = OFF_MA + D_MODEL
N_MAIN = OFF_MB + D_MODEL
W_IN_A = OFF_MA
A_PAD = LANES
INPROJ_TN = 1024
INPROJ_TN_BF16 = 2048

F32 = jnp.float32
BF16 = jnp.bfloat16

VMEM_LIMIT = 56 * 1024 * 1024


def _dot(a, b):
    return jnp.dot(a, b, preferred_element_type=F32)


def _dot_nt(a, b):
    return lax.dot_general(a, b, (((1,), (1,)), ((), ())), preferred_element_type=F32)


def _sigmoid(x):
    return jax.nn.sigmoid(x)


def _rms(x, g):
    return x * lax.rsqrt(jnp.mean(x * x, axis=-1, keepdims=True) + EPS) * g


def _split_bf16(x):
    hi = x.astype(BF16)
    lo = (x - hi.astype(F32)).astype(BF16)
    return hi, lo


def _log_decay(a_lr, w_a2, b_a):
    pre = _dot(a_lr, w_a2) + b_a
    return (jnp.minimum(pre, 0.0) - jnp.log(1.0 + jnp.exp(-jnp.abs(pre)))) * (1.0 / GATE_TAU)


def _bcast_rows(b, idxs, seg):
    return jnp.concatenate(
        [jnp.broadcast_to(b[i:i + 1, :], (seg, b.shape[1])) for i in idxs], axis=0)


BF16_SUBLANES = 16


def _inproj_kernel(*refs, n_cast, emit_w):
    x_ref, g_ref, w_ref, w_a_ref = refs[:4]
    cast_in = refs[4:4 + n_cast]
    o_ref, o_a_ref = refs[4 + n_cast:6 + n_cast]
    cast_out = refs[6 + n_cast:6 + 2 * n_cast]
    rest = refs[6 + 2 * n_cast:]
    xn_ref = rest[-1]

    @pl.when(pl.program_id(1) == 0)
    def _():
        xn = _rms(x_ref[...], g_ref[...]).astype(BF16)
        xn_ref[...] = xn
        w_a = w_a_ref[...]
        if emit_w:
            w_a = w_a.astype(BF16)
            rest[1][...] = w_a
        o_a_ref[...] = _dot_nt(xn, w_a).astype(o_a_ref.dtype)

    for src, dst in zip(cast_in, cast_out):
        dst[...] = src[...].astype(dst.dtype)
    w = w_ref[...]
    if emit_w:
        w = w.astype(BF16)
        rest[0][...] = w
    o_ref[...] = _dot_nt(xn_ref[...], w).astype(o_ref.dtype)


def _inproj(x2d, g_pre, w_in_t, tm, w_bf=None, cast=()):
    t = x2d.shape[0]
    emit_w = w_bf is None
    tn = INPROJ_TN if emit_w else INPROJ_TN_BF16
    n_i, n_j = t // tm, N_MAIN // tn

    def w_rows(i, j):
        row0 = j * tn + jnp.where(j * tn >= OFF_MA, GATE_RANK, 0)
        return (pl.multiple_of(row0, GATE_RANK), 0)

    if emit_w:
        w_args = (w_in_t, w_in_t)
        w_specs = [pl.BlockSpec((pl.Element(tn), pl.Element(D_MODEL)), w_rows),
                   pl.BlockSpec((A_PAD, D_MODEL), lambda i, j: (W_IN_A // A_PAD, 0))]
        w_out_specs = [pl.BlockSpec((tn, D_MODEL), lambda i, j: (j, 0)),
                       pl.BlockSpec((A_PAD, D_MODEL), lambda i, j: (0, 0))]
        w_out_shapes = [jax.ShapeDtypeStruct((N_MAIN, D_MODEL), BF16),
                        jax.ShapeDtypeStruct((A_PAD, D_MODEL), BF16)]
        assert n_i == 1, "the bf16 weight copy is written once, by a single token tile"
    else:
        w_args = w_bf
        w_specs = [pl.BlockSpec((tn, D_MODEL), lambda i, j: (j, 0)),
                   pl.BlockSpec((A_PAD, D_MODEL), lambda i, j: (0, 0))]
        w_out_specs, w_out_shapes = [], []

    cast_specs = []
    for w in cast:
        per_i = w.shape[0] // n_i
        assert per_i * n_i == w.shape[0] and per_i % BF16_SUBLANES == 0
        n_sub = max(n for n in range(1, n_j + 1)
                    if per_i % n == 0 and (per_i // n) % BF16_SUBLANES == 0)
        cast_specs.append(pl.BlockSpec(
            (per_i // n_sub, w.shape[1]),
            lambda i, j, n_sub=n_sub: (i * n_sub + jnp.minimum(j, n_sub - 1), 0)))

    kern = functools.partial(_inproj_kernel, n_cast=len(cast), emit_w=emit_w)
    return pl.pallas_call(
        kern,
        grid=(n_i, n_j),
        in_specs=[
            pl.BlockSpec((tm, D_MODEL), lambda i, j: (i, 0)),
            pl.BlockSpec((1, D_MODEL), lambda i, j: (0, 0)),
        ] + w_specs + cast_specs,
        out_specs=[
            pl.BlockSpec((tm, tn), lambda i, j: (i, j)),
            pl.BlockSpec((tm, A_PAD), lambda i, j: (i, 0)),
        ] + cast_specs + w_out_specs,
        out_shape=[
            jax.ShapeDtypeStruct((t, N_MAIN), BF16),
            jax.ShapeDtypeStruct((t, A_PAD), BF16),
        ] + [jax.ShapeDtypeStruct(w.shape, BF16) for w in cast] + w_out_shapes,
        scratch_shapes=[pltpu.VMEM((tm, D_MODEL), BF16)],
        compiler_params=pltpu.CompilerParams(
            dimension_semantics=("arbitrary", "arbitrary"),
            vmem_limit_bytes=VMEM_LIMIT),
        name="inproj_f32w" if emit_w else "inproj",
    )(x2d, g_pre, *w_args, *cast)


def _gla_levels(c):
    levels = []
    blk = c
    while blk > GLA_SUB:
        per = blk // GLA_SUB
        levels.append((blk, [(j // per) * per + per // 2 - 1 for j in range(c // GLA_SUB)]))
        blk //= 2
    return levels


def _gla_masks(c):
    t = lax.broadcasted_iota(jnp.int32, (c, c), 0)
    s = lax.broadcasted_iota(jnp.int32, (c, c), 1)
    levels = _gla_levels(c)
    masks = []
    for blk, _ in levels:
        sh = blk.bit_length() - 1
        half = blk // 2
        same = (t >> sh) == (s >> sh)
        masks.append(same & ((t & (blk - 1)) >= half) & ((s & (blk - 1)) < half))
    sh = GLA_SUB.bit_length() - 1
    masks.append(((t >> sh) == (s >> sh)) & (s <= t))
    return masks


def _gla_chunk(q, k, vb, log_a, s_heads, tril_ones, masks):
    c = q.shape[0]
    hi, lo = _split_bf16(log_a)
    b = _dot(tril_ones, hi) + _dot(tril_ones, lo)
    n = c // GLA_SUB

    def rep(x):
        return _bcast_rows(x, list(range(n)), GLA_SUB)

    ends = jnp.concatenate([b[(j + 1) * GLA_SUB - 1:(j + 1) * GLA_SUB, :] for j in range(n)], axis=0)
    starts = jnp.concatenate([jnp.zeros((1, b.shape[1]), F32), ends[:n - 1]], axis=0)
    starts_r = rep(starts)
    q_e = q * jnp.exp(b - starts_r)
    k_e = k * jnp.exp(rep(ends) - b)

    qs, ks = [], []
    for _, ref_blocks in _gla_levels(c):
        ref = jnp.concatenate([ends[r:r + 1] for r in ref_blocks], axis=0)
        qs.append((q_e * rep(jnp.exp(jnp.minimum(starts - ref, 0.0)))).astype(BF16))
        ks.append((k_e * rep(jnp.exp(jnp.minimum(ref - ends, 0.0)))).astype(BF16))
    qs.append(q_e.astype(BF16))
    ks.append((k * jnp.exp(starts_r - b)).astype(BF16))

    q_in = (q_e * rep(jnp.exp(starts))).astype(BF16)
    k_out = k_e * rep(jnp.exp(ends[n - 1:n] - ends))

    outs, new_states = [], []
    for h in range(HB):
        kc = slice(h * DK, (h + 1) * DK)
        vc = slice(h * DV, (h + 1) * DV)
        att = jnp.zeros((c, c), F32)
        for ql, kl, m in zip(qs, ks, masks):
            att = jnp.where(m, _dot_nt(ql[:, kc], kl[:, kc]), att)
        v_h = vb[:, vc]
        s_h = s_heads[h]
        o_h = _dot(att.astype(BF16), v_h) + _dot(q_in[:, kc], s_h.astype(BF16))
        b_t = jnp.transpose(b[:, kc])
        decay = jnp.broadcast_to(jnp.exp(b_t[:, c - 1:c]), (DK, DV))
        k_t = jnp.transpose(k_out[:, kc]).astype(BF16)
        new_states.append(decay * s_h + _dot(k_t, v_h))
        outs.append(o_h)
    return jnp.concatenate(outs, axis=1), new_states


def _gla_decode_rows(q, k, vb, log_a, s0_ref, s1_ref, nb, seq):
    rows = nb * seq
    t = lax.broadcasted_iota(jnp.int32, (rows, rows), 0)
    s = lax.broadcasted_iota(jnp.int32, (rows, rows), 1)
    sh = seq.bit_length() - 1
    causal = ((t >> sh) == (s >> sh)) & (s <= t)
    tril_ones = jnp.where(causal, 1.0, 0.0).astype(BF16)
    hi, lo = _split_bf16(log_a)
    b = _dot(tril_ones, hi) + _dot(tril_ones, lo)

    q_t = q * jnp.exp(b)
    k_t = (k * jnp.exp(-b)).astype(BF16)
    b_last = _bcast_rows(b, [n * seq + seq - 1 for n in range(nb)], seq)
    k_out = k * jnp.exp(b_last - b)
    col = lax.broadcasted_iota(jnp.int32, (DK, rows), 1) >> sh

    outs = []
    for h in range(HB):
        kc = slice(h * DK, (h + 1) * DK)
        vc = slice(h * DV, (h + 1) * DV)
        v_h = vb[:, vc]
        att = jnp.where(causal, _dot_nt(q_t[:, kc].astype(BF16), k_t[:, kc]), 0.0)
        o_intra = _dot(att.astype(BF16), v_h)
        decay_tr = jnp.exp(jnp.transpose(b[:, kc]))
        k_tr = jnp.transpose(k_out[:, kc])
        o_inter = []
        for n in range(nb):
            s_n = s0_ref[n, h]
            r0 = n * seq
            o_inter.append(_dot(q_t[r0:r0 + seq, kc], s_n))
            decay = jnp.broadcast_to(decay_tr[:, r0 + seq - 1:r0 + seq], (DK, DV))
            k_n = jnp.where(col == n, k_tr, 0.0).astype(BF16)
            s1_ref[n, h] = decay * s_n + _dot(k_n, v_h)
        outs.append(o_intra + jnp.concatenate(o_inter, axis=0))
    return jnp.concatenate(outs, axis=1)


def _spatial_masked_weights(w_s_ref, seq):
    t = lax.broadcasted_iota(jnp.int32, (CHUNK, CHUNK), 0)
    s = lax.broadcasted_iota(jnp.int32, (CHUNK, CHUNK), 1)
    causal = s <= t
    if seq < CHUNK:
        sh = seq.bit_length() - 1
        causal = causal & ((t >> sh) == (s >> sh))
    return [jnp.where(causal, w_s_ref[g], 0.0).astype(BF16) for g in range(HA)]


def _branch_a_chunk(proj_ref, r, g_v_ref, w_sp, b_col_ref):
    vn = _rms(proj_ref[r, OFF_V:OFF_V + E_A].astype(F32), g_v_ref[...])
    vn_b = vn.astype(BF16)
    gate = jnp.concatenate(
        [_dot(w_sp[g], vn_b[:, g * DA:(g + 1) * DA]) + b_col_ref[:, g:g + 1]
         for g in range(HA)], axis=1)
    z = proj_ref[r, OFF_Z:OFF_Z + E_A].astype(F32)
    return proj_ref[r, OFF_U:OFF_U + E_A].astype(F32) * gate * (z * _sigmoid(z)), vn


def _branch_b_out(o, zb, g_o_ref):
    g_o = g_o_ref[...]
    on = jnp.concatenate(
        [_rms(o[:, h * DV:(h + 1) * DV], g_o[:, h * DV:(h + 1) * DV]) for h in range(HB)], axis=1)
    zb = zb.astype(F32)
    return on * (zb * _sigmoid(zb))


MXU_COLS = 256


def _col_blocks(rows):
    width = MXU_COLS if rows >= 4 * CHUNK else 2 * MXU_COLS
    return [slice(c, c + width) for c in range(0, D_MODEL, width)]


def _merge(y_a, y_b, proj_ref, w_pa_ref, w_pb_ref, m_ref):
    for c in _col_blocks(y_a.shape[0]):
        m_a = proj_ref[:, OFF_MA + c.start:OFF_MA + c.stop].astype(F32)
        m_b = proj_ref[:, OFF_MB + c.start:OFF_MB + c.stop].astype(F32)
        m_ref[:, c] = (_sigmoid(m_a) * _dot(y_a, w_pa_ref[:, c])
                       + _sigmoid(m_b) * _dot(y_b, w_pb_ref[:, c])).astype(m_ref.dtype)


def _chunks(tm):
    return [slice(c * CHUNK, (c + 1) * CHUNK) for c in range(tm // CHUNK)]


def _mix_prompt_kernel(proj_ref, proj_a_ref, g_v_ref, w_s_ref, b_col_ref, w_a2_ref, b_a_ref,
                       g_o_ref, w_pa_ref, w_pb_ref, m_ref, s_out_ref, s_ref, *, tm, seq):
    keep = pl.program_id(1) != 0
    states = [jnp.where(keep, s_ref[h], 0.0) for h in range(HB)]
    w_sp = _spatial_masked_weights(w_s_ref, seq)
    t = lax.broadcasted_iota(jnp.int32, (CHUNK, CHUNK), 0)
    s = lax.broadcasted_iota(jnp.int32, (CHUNK, CHUNK), 1)
    tril_ones = jnp.where(s <= t, 1.0, 0.0).astype(BF16)
    masks = _gla_masks(CHUNK)
    log_a = _log_decay(proj_a_ref[...], w_a2_ref[...], b_a_ref[...])
    ya_rows, yb_rows = [], []
    for r in _chunks(tm):
        ya_rows.append(_branch_a_chunk(proj_ref, r, g_v_ref, w_sp, b_col_ref)[0].astype(BF16))
        q = proj_ref[r, OFF_Q:OFF_Q + K_B].astype(F32) * (DK ** -0.5)
        k = proj_ref[r, OFF_K:OFF_K + K_B].astype(F32)
        vb = proj_ref[r, OFF_VB:OFF_VB + E_B]
        o_c, states = _gla_chunk(q, k, vb, log_a[r], states, tril_ones, masks)
        yb_rows.append(_branch_b_out(o_c, proj_ref[r, OFF_ZB:OFF_ZB + E_B], g_o_ref).astype(BF16))
    for h in range(HB):
        s_ref[h] = states[h]
        s_out_ref[0, h] = states[h]
    _merge(jnp.concatenate(ya_rows, axis=0), jnp.concatenate(yb_rows, axis=0),
           proj_ref, w_pa_ref, w_pb_ref, m_ref)


def _mix_decode_kernel(proj_ref, proj_a_ref, s0_ref, g_v_ref, w_s_ref, b_col_ref, w_a2_ref, b_a_ref,
                       g_o_ref, w_pa_ref, w_pb_ref, m_ref, vn_ref, s1_ref, *, nb, seq):
    rows = slice(0, nb * seq)
    log_a = _log_decay(proj_a_ref[...], w_a2_ref[...], b_a_ref[...])
    q = proj_ref[:, OFF_Q:OFF_Q + K_B].astype(F32) * (DK ** -0.5)
    k = proj_ref[:, OFF_K:OFF_K + K_B].astype(F32)
    o = _gla_decode_rows(q, k, proj_ref[:, OFF_VB:OFF_VB + E_B], log_a, s0_ref, s1_ref, nb, seq)
    y_a, vn = _branch_a_chunk(proj_ref, rows, g_v_ref, _spatial_masked_weights(w_s_ref, seq),
                              b_col_ref)
    vn_ref[...] = vn
    y_b = _branch_b_out(o, proj_ref[:, OFF_ZB:OFF_ZB + E_B], g_o_ref)
    _merge(y_a.astype(BF16), y_b.astype(BF16), proj_ref, w_pa_ref, w_pb_ref, m_ref)


def _out_kernel(m_p_ref, m_s_ref, x_p_ref, x_s_ref, p_p_ref, p_s_ref,
                w_o_ref, g_post_ref, w_pg_ref, w_pe_ref, y_p_ref, y_s_ref, *, n_prompt_tiles):
    is_s = pl.program_id(0) >= n_prompt_tiles
    m = jnp.where(is_s, m_s_ref[...], m_p_ref[...])
    x = jnp.where(is_s, x_s_ref[...], x_p_ref[...])
    p_b = jnp.where(is_s, p_s_ref[...], p_p_ref[...]).astype(BF16)
    blocks = _col_blocks(m.shape[0])
    mo = jnp.concatenate([_dot(m, w_o_ref[:, c]) for c in blocks], axis=1)
    h_res = x + _rms(mo, g_post_ref[...])
    h_b = h_res.astype(BF16)
    for k in range(0, len(blocks), 2):
        cols = slice(blocks[k].start, blocks[k + 1].stop)
        y = jnp.concatenate(
            [h_res[:, c] + _sigmoid(_dot(h_b, w_pg_ref[:, c])) * _dot(p_b, w_pe_ref[:, c])
             for c in blocks[k:k + 2]], axis=1)
        to_s = jnp.broadcast_to(is_s, y.shape)
        pltpu.store(y_p_ref.at[:, cols], y, mask=jnp.logical_not(to_s))
        pltpu.store(y_s_ref.at[:, cols], y, mask=to_s)


def _const_spec(shape):
    nd = len(shape)
    return pl.BlockSpec(shape, lambda *_: (0,) * nd, pipeline_mode=pl.Buffered(1))


def _mix_prompt(proj, proj_a, wts, n_batch, seq, tm):
    nt = seq // tm
    kern = functools.partial(_mix_prompt_kernel, tm=tm, seq=seq)
    row = lambda b, i: (b * nt + i, 0)
    names = ("g_v", "w_s", "b_col", "w_a2", "b_a", "g_o", "w_pa", "w_pb")
    w_args = [wts[n] for n in names]
    return pl.pallas_call(
        kern,
        grid=(n_batch, nt),
        in_specs=[
            pl.BlockSpec((tm, N_MAIN), row),
            pl.BlockSpec((tm, A_PAD), row),
        ] + [_const_spec(w.shape) for w in w_args],
        out_specs=[
            pl.BlockSpec((tm, D_MODEL), row),
            pl.BlockSpec((1, HB, DK, DV), lambda b, i: (b, 0, 0, 0)),
        ],
        out_shape=[
            jax.ShapeDtypeStruct((n_batch * seq, D_MODEL), BF16),
            jax.ShapeDtypeStruct((n_batch, HB, DK, DV), F32),
        ],
        scratch_shapes=[pltpu.VMEM((HB, DK, DV), F32)],
        compiler_params=pltpu.CompilerParams(
            dimension_semantics=("arbitrary", "arbitrary"),
            vmem_limit_bytes=VMEM_LIMIT),
        name="mix_prompt",
    )(proj, proj_a, *w_args)


def _mix_decode(proj, proj_a, s0, wts, nb, seq):
    n_batch = s0.shape[0]
    rows = nb * seq
    assert rows == CHUNK
    kern = functools.partial(_mix_decode_kernel, nb=nb, seq=seq)
    row = lambda i: (i, 0)
    state = lambda i: (i, 0, 0, 0)
    names = ("g_v", "w_s_bd", "b_col_bd", "w_a2", "b_a", "g_o", "w_pa", "w_pb")
    w_args = [wts[n] for n in names]
    return pl.pallas_call(
        kern,
        grid=(n_batch // nb,),
        in_specs=[
            pl.BlockSpec((rows, N_MAIN), row),
            pl.BlockSpec((rows, A_PAD), row),
            pl.BlockSpec((nb, HB, DK, DV), state),
        ] + [_const_spec(w.shape) for w in w_args],
        out_specs=[
            pl.BlockSpec((rows, D_MODEL), row),
            pl.BlockSpec((rows, E_A), row),
            pl.BlockSpec((nb, HB, DK, DV), state),
        ],
        out_shape=[
            jax.ShapeDtypeStruct((n_batch * seq, D_MODEL), BF16),
            jax.ShapeDtypeStruct((n_batch * seq, E_A), F32),
            jax.ShapeDtypeStruct(s0.shape, F32),
        ],
        compiler_params=pltpu.CompilerParams(
            dimension_semantics=("arbitrary",),
            vmem_limit_bytes=VMEM_LIMIT),
        name="mix_decode",
    )(proj, proj_a, s0, *w_args)


def _out_proj(m_p, m_s, x_p, x_s, p_p, p_s, wts, tm):
    n_p, n_s = x_p.shape[0] // tm, x_s.shape[0] // tm
    kern = functools.partial(_out_kernel, n_prompt_tiles=n_p)
    prompt = lambda i: (jnp.minimum(i, n_p - 1), 0)
    sample = lambda i: (jnp.maximum(i - n_p, 0), 0)
    prompt_spec = lambda width: pl.BlockSpec((tm, width), prompt)
    sample_spec = lambda width: pl.BlockSpec((tm, width), sample, pipeline_mode=pl.Buffered(1))
    names = ("w_o", "g_post", "w_pg", "w_pe")
    w_args = [wts[n] for n in names]
    return pl.pallas_call(
        kern,
        grid=(n_p + n_s,),
        in_specs=[
            prompt_spec(D_MODEL), sample_spec(D_MODEL),
            prompt_spec(D_MODEL), sample_spec(D_MODEL),
            prompt_spec(PLE_DIM), sample_spec(PLE_DIM),
        ] + [_const_spec(w.shape) for w in w_args],
        out_specs=[prompt_spec(D_MODEL), sample_spec(D_MODEL)],
        out_shape=[
            jax.ShapeDtypeStruct(x_p.shape, F32),
            jax.ShapeDtypeStruct(x_s.shape, F32),
        ],
        compiler_params=pltpu.CompilerParams(
            dimension_semantics=("arbitrary",),
            vmem_limit_bytes=VMEM_LIMIT),
        name="out_proj",
    )(m_p, m_s, x_p, x_s, p_p, p_s, *w_args)


def _layer_weights(i, dec_seq, g_pre, w_in, g_v, w_s, b_s, w_a2, b_a, g_o, g_post):
    w_a2_p = jnp.concatenate(
        [w_a2[i], jnp.zeros((A_PAD - GATE_RANK, K_B), w_a2.dtype)], axis=0).astype(BF16)
    reps = CHUNK // dec_seq
    return dict(
        g_pre=g_pre[i][None, :],
        w_in_t=w_in[i].T,
        g_v=g_v[i][None, :],
        w_s=w_s[i],
        b_col=b_s[i].T,
        w_s_bd=jnp.tile(w_s[i][:, :dec_seq, :dec_seq], (1, reps, reps)),
        b_col_bd=jnp.tile(b_s[i][:, :dec_seq].T, (reps, 1)),
        w_a2=w_a2_p,
        b_a=b_a[i][None, :],
        g_o=g_o[i].reshape(1, E_B),
        g_post=g_post[i][None, :],
    )


SIDE_WEIGHTS = ("w_pa", "w_pb", "w_o", "w_pg", "w_pe")

INPROJ_TM = 1024
MIX_TM = 512
OUT_TM = 512


def kernel(x_prompt, x_sample, p_prompt, p_sample, state_gla, g_pre, w_in, g_v, w_s, b_s,
           w_a2, b_a, g_o, w_pa, w_pb, w_o, g_post, w_pg, w_pe):
    depth = w_in.shape[0]
    n_batch, seq, _ = x_prompt.shape
    dec_batch, dec_seq, _ = x_sample.shape
    hp = x_prompt.reshape(n_batch * seq, D_MODEL)
    hs = x_sample.reshape(dec_batch * dec_seq, D_MODEL)
    sp_list, ss_list, cv_list = [], [], []
    for i in range(depth):
        wts = _layer_weights(i, dec_seq, g_pre, w_in, g_v, w_s, b_s, w_a2, b_a, g_o, g_post)
        side_f32 = dict(w_pa=w_pa[i], w_pb=w_pb[i], w_o=w_o[i], w_pg=w_pg[i], w_pe=w_pe[i])
        proj_s, proj_a_s, *w_bf = _inproj(hs, wts["g_pre"], wts["w_in_t"], tm=INPROJ_TM)
        proj_p, proj_a_p, *side = _inproj(hp, wts["g_pre"], wts["w_in_t"], tm=INPROJ_TM, w_bf=w_bf,
                                          cast=[side_f32[n] for n in SIDE_WEIGHTS])
        wts.update(zip(SIDE_WEIGHTS, side))
        m_s, v_s, s_s = _mix_decode(proj_s, proj_a_s, state_gla[i], wts,
                                    nb=CHUNK // dec_seq, seq=dec_seq)
        m_p, s_p = _mix_prompt(proj_p, proj_a_p, wts, n_batch, seq, tm=MIX_TM)
        hp, hs = _out_proj(m_p, m_s, hp, hs, p_prompt[i].reshape(n_batch * seq, PLE_DIM),
                           p_sample[i].reshape(dec_batch * dec_seq, PLE_DIM), wts, tm=OUT_TM)
        sp_list.append(s_p)
        ss_list.append(s_s)
        cv_list.append(v_s.reshape(dec_batch, dec_seq, E_A))
    stack = (lambda xs: xs[0][None]) if depth == 1 else jnp.stack
    return (hp.reshape(n_batch, seq, D_MODEL),
            hs.reshape(dec_batch, dec_seq, D_MODEL),
            stack(sp_list), stack(ss_list), stack(cv_list))
```

```python
import functools

import jax
import jax.numpy as jnp
from jax import lax
from jax.experimental import pallas as pl
from jax.experimental.pallas import tpu as pltpu

D_MODEL = 2048
E_A = D_MODEL // 2
HA = 4
DA = E_A // HA
CHUNK = 128
HB = 4
E_B = D_MODEL // 2
DV = E_B // HB
DK = DV // 2
K_B = HB * DK
GATE_RANK = 16
GATE_TAU = 16.0
GLA_SUB = 16
PLE_DIM = 256
EPS = 1e-6

LANES = 128
OFF_U = 0
OFF_V = OFF_U + E_A
OFF_Z = OFF_V + E_A
OFF_Q = OFF_Z + E_A
OFF_K = OFF_Q + K_B
OFF_VB = OFF_K + K_B
OFF_ZB = OFF_VB + E_B
OFF_MA = OFF_ZB + E_B
OFF_MB = OFF_MA + D_MODEL
N_MAIN = OFF_MB + D_MODEL
W_IN_A = OFF_MA
A_PAD = LANES
INPROJ_TN = 1024
INPROJ_TN_BF16 = 2048

F32 = jnp.float32
BF16 = jnp.bfloat16

VMEM_LIMIT = 56 * 1024 * 1024


def _dot(a, b):
    return jnp.dot(a, b, preferred_element_type=F32)


def _dot_nt(a, b):
    return lax.dot_general(a, b, (((1,), (1,)), ((), ())), preferred_element_type=F32)


def _sigmoid(x):
    return jax.nn.sigmoid(x)


def _rms(x, g):
    return x * lax.rsqrt(jnp.mean(x * x, axis=-1, keepdims=True) + EPS) * g


def _split_bf16(x):
    hi = x.astype(BF16)
    lo = (x - hi.astype(F32)).astype(BF16)
    return hi, lo


def _log_decay(a_lr, w_a2, b_a):
    pre = _dot(a_lr, w_a2) + b_a
    return (jnp.minimum(pre, 0.0) - jnp.log(1.0 + jnp.exp(-jnp.abs(pre)))) * (1.0 / GATE_TAU)


def _bcast_rows(b, idxs, seg):
    return jnp.concatenate(
        [jnp.broadcast_to(b[i:i + 1, :], (seg, b.shape[1])) for i in idxs], axis=0)


BF16_SUBLANES = 16


def _inproj_kernel(*refs, n_cast, emit_w):
    x_ref, g_ref, w_ref, w_a_ref = refs[:4]
    cast_in = refs[4:4 + n_cast]
    o_ref, o_a_ref = refs[4 + n_cast:6 + n_cast]
    cast_out = refs[6 + n_cast:6 + 2 * n_cast]
    rest = refs[6 + 2 * n_cast:]
    xn_ref = rest[-1]

    @pl.when(pl.program_id(1) == 0)
    def _():
        xn = _rms(x_ref[...], g_ref[...]).astype(BF16)
        xn_ref[...] = xn
        w_a = w_a_ref[...]
        if emit_w:
            w_a = w_a.astype(BF16)
            rest[1][...] = w_a
        o_a_ref[...] = _dot_nt(xn, w_a).astype(o_a_ref.dtype)

    for src, dst in zip(cast_in, cast_out):
        dst[...] = src[...].astype(dst.dtype)
    w = w_ref[...]
    if emit_w:
        w = w.astype(BF16)
        rest[0][...] = w
    o_ref[...] = _dot_nt(xn_ref[...], w).astype(o_ref.dtype)


def _inproj(x2d, g_pre, w_in_t, tm, w_bf=None, cast=()):
    t = x2d.shape[0]
    emit_w = w_bf is None
    tn = INPROJ_TN if emit_w else INPROJ_TN_BF16
    n_i, n_j = t // tm, N_MAIN // tn

    def w_rows(i, j):
        row0 = j * tn + jnp.where(j * tn >= OFF_MA, GATE_RANK, 0)
        return (pl.multiple_of(row0, GATE_RANK), 0)

    if emit_w:
        w_args = (w_in_t, w_in_t)
        w_specs = [pl.BlockSpec((pl.Element(tn), pl.Element(D_MODEL)), w_rows),
                   pl.BlockSpec((A_PAD, D_MODEL), lambda i, j: (W_IN_A // A_PAD, 0))]
        w_out_specs = [pl.BlockSpec((tn, D_MODEL), lambda i, j: (j, 0)),
                       pl.BlockSpec((A_PAD, D_MODEL), lambda i, j: (0, 0))]
        w_out_shapes = [jax.ShapeDtypeStruct((N_MAIN, D_MODEL), BF16),
                        jax.ShapeDtypeStruct((A_PAD, D_MODEL), BF16)]
        assert n_i == 1, "the bf16 weight copy is written once, by a single token tile"
    else:
        w_args = w_bf
        w_specs = [pl.BlockSpec((tn, D_MODEL), lambda i, j: (j, 0)),
                   pl.BlockSpec((A_PAD, D_MODEL), lambda i, j: (0, 0))]
        w_out_specs, w_out_shapes = [], []

    cast_specs = []
    for w in cast:
        per_i = w.shape[0] // n_i
        assert per_i * n_i == w.shape[0] and per_i % BF16_SUBLANES == 0
        n_sub = max(n for n in range(1, n_j + 1)
                    if per_i % n == 0 and (per_i // n) % BF16_SUBLANES == 0)
        cast_specs.append(pl.BlockSpec(
            (per_i // n_sub, w.shape[1]),
            lambda i, j, n_sub=n_sub: (i * n_sub + jnp.minimum(j, n_sub - 1), 0)))

    kern = functools.partial(_inproj_kernel, n_cast=len(cast), emit_w=emit_w)
    return pl.pallas_call(
        kern,
        grid=(n_i, n_j),
        in_specs=[
            pl.BlockSpec((tm, D_MODEL), lambda i, j: (i, 0)),
            pl.BlockSpec((1, D_MODEL), lambda i, j: (0, 0)),
        ] + w_specs + cast_specs,
        out_specs=[
            pl.BlockSpec((tm, tn), lambda i, j: (i, j)),
            pl.BlockSpec((tm, A_PAD), lambda i, j: (i, 0)),
        ] + cast_specs + w_out_specs,
        out_shape=[
            jax.ShapeDtypeStruct((t, N_MAIN), BF16),
            jax.ShapeDtypeStruct((t, A_PAD), BF16),
        ] + [jax.ShapeDtypeStruct(w.shape, BF16) for w in cast] + w_out_shapes,
        scratch_shapes=[pltpu.VMEM((tm, D_MODEL), BF16)],
        compiler_params=pltpu.CompilerParams(
            dimension_semantics=("arbitrary", "arbitrary"),
            vmem_limit_bytes=VMEM_LIMIT),
        name="inproj_f32w" if emit_w else "inproj",
    )(x2d, g_pre, *w_args, *cast)


def _gla_levels(c):
    levels = []
    blk = c
    while blk > GLA_SUB:
        per = blk // GLA_SUB
        levels.append((blk, [(j // per) * per + per // 2 - 1 for j in range(c // GLA_SUB)]))
        blk //= 2
    return levels


def _gla_masks(c):
    t = lax.broadcasted_iota(jnp.int32, (c, c), 0)
    s = lax.broadcasted_iota(jnp.int32, (c, c), 1)
    levels = _gla_levels(c)
    masks = []
    for blk, _ in levels:
        sh = blk.bit_length() - 1
        half = blk // 2
        same = (t >> sh) == (s >> sh)
        masks.append(same & ((t & (blk - 1)) >= half) & ((s & (blk - 1)) < half))
    sh = GLA_SUB.bit_length() - 1
    masks.append(((t >> sh) == (s >> sh)) & (s <= t))
    return masks


def _gla_chunk(q, k, vb, log_a, s_heads, tril_ones, masks):
    c = q.shape[0]
    hi, lo = _split_bf16(log_a)
    b = _dot(tril_ones, hi) + _dot(tril_ones, lo)
    n = c // GLA_SUB

    def rep(x):
        return _bcast_rows(x, list(range(n)), GLA_SUB)

    ends = jnp.concatenate([b[(j + 1) * GLA_SUB - 1:(j + 1) * GLA_SUB, :] for j in range(n)], axis=0)
    starts = jnp.concatenate([jnp.zeros((1, b.shape[1]), F32), ends[:n - 1]], axis=0)
    starts_r = rep(starts)
    q_e = q * jnp.exp(b - starts_r)
    k_e = k * jnp.exp(rep(ends) - b)

    qs, ks = [], []
    for _, ref_blocks in _gla_levels(c):
        ref = jnp.concatenate([ends[r:r + 1] for r in ref_blocks], axis=0)
        qs.append((q_e * rep(jnp.exp(jnp.minimum(starts - ref, 0.0)))).astype(BF16))
        ks.append((k_e * rep(jnp.exp(jnp.minimum(ref - ends, 0.0)))).astype(BF16))
    qs.append(q_e.astype(BF16))
    ks.append((k * jnp.exp(starts_r - b)).astype(BF16))

    q_in = (q_e * rep(jnp.exp(starts))).astype(BF16)
    k_out = k_e * rep(jnp.exp(ends[n - 1:n] - ends))

    outs, new_states = [], []
    for h in range(HB):
        kc = slice(h * DK, (h + 1) * DK)
        vc = slice(h * DV, (h + 1) * DV)
        att = jnp.zeros((c, c), F32)
        for ql, kl, m in zip(qs, ks, masks):
            att = jnp.where(m, _dot_nt(ql[:, kc], kl[:, kc]), att)
        v_h = vb[:, vc]
        s_h = s_heads[h]
        o_h = _dot(att.astype(BF16), v_h) + _dot(q_in[:, kc], s_h.astype(BF16))
        b_t = jnp.transpose(b[:, kc])
        decay = jnp.broadcast_to(jnp.exp(b_t[:, c - 1:c]), (DK, DV))
        k_t = jnp.transpose(k_out[:, kc]).astype(BF16)
        new_states.append(decay * s_h + _dot(k_t, v_h))
        outs.append(o_h)
    return jnp.concatenate(outs, axis=1), new_states


def _gla_decode_rows(q, k, vb, log_a, s0_ref, s1_ref, nb, seq):
    rows = nb * seq
    t = lax.broadcasted_iota(jnp.int32, (rows, rows), 0)
    s = lax.broadcasted_iota(jnp.int32, (rows, rows), 1)
    sh = seq.bit_length() - 1
    causal = ((t >> sh) == (s >> sh)) & (s <= t)
    tril_ones = jnp.where(causal, 1.0, 0.0).astype(BF16)
    hi, lo = _split_bf16(log_a)
    b = _dot(tril_ones, hi) + _dot(tril_ones, lo)

    q_t = q * jnp.exp(b)
    k_t = (k * jnp.exp(-b)).astype(BF16)
    b_last = _bcast_rows(b, [n * seq + seq - 1 for n in range(nb)], seq)
    k_out = k * jnp.exp(b_last - b)
    col = lax.broadcasted_iota(jnp.int32, (DK, rows), 1) >> sh

    outs = []
    for h in range(HB):
        kc = slice(h * DK, (h + 1) * DK)
        vc = slice(h * DV, (h + 1) * DV)
        v_h = vb[:, vc]
        att = jnp.where(causal, _dot_nt(q_t[:, kc].astype(BF16), k_t[:, kc]), 0.0)
        o_intra = _dot(att.astype(BF16), v_h)
        decay_tr = jnp.exp(jnp.transpose(b[:, kc]))
        k_tr = jnp.transpose(k_out[:, kc])
        o_inter = []
        for n in range(nb):
            s_n = s0_ref[n, h]
            r0 = n * seq
            o_inter.append(_dot(q_t[r0:r0 + seq, kc], s_n))
            decay = jnp.broadcast_to(decay_tr[:, r0 + seq - 1:r0 + seq], (DK, DV))
            k_n = jnp.where(col == n, k_tr, 0.0).astype(BF16)
            s1_ref[n, h] = decay * s_n + _dot(k_n, v_h)
        outs.append(o_intra + jnp.concatenate(o_inter, axis=0))
    return jnp.concatenate(outs, axis=1)


def _spatial_masked_weights(w_s_ref, b_col_ref, seq):
    t = lax.broadcasted_iota(jnp.int32, (CHUNK, CHUNK), 0)
    s = lax.broadcasted_iota(jnp.int32, (CHUNK, CHUNK), 1)
    causal = s <= t
    if seq >= CHUNK:
        return [jnp.where(causal, w_s_ref[g], 0.0).astype(BF16) for g in range(HA)], b_col_ref[...]
    sh = seq.bit_length() - 1
    causal = causal & ((t >> sh) == (s >> sh))
    pick_rows = jnp.where((s < seq) & ((t & (seq - 1)) == s), 1.0, 0.0)
    pick_cols = jnp.where((t < seq) & ((s & (seq - 1)) == t), 1.0, 0.0)
    w_sp = [jnp.where(causal, _dot(_dot(pick_rows, w_s_ref[g]), pick_cols), 0.0).astype(BF16)
            for g in range(HA)]
    pos = lax.broadcasted_iota(jnp.int32, (CHUNK, HA), 0) & (seq - 1)
    bias = jnp.zeros((CHUNK, HA), F32)
    for i in range(seq):
        bias = jnp.where(pos == i, b_col_ref[i:i + 1, :], bias)
    return w_sp, bias


def _branch_a_chunk(proj_ref, r, g_v_ref, w_sp, b_col):
    vn = _rms(proj_ref[r, OFF_V:OFF_V + E_A].astype(F32), g_v_ref[...])
    vn_b = vn.astype(BF16)
    gate = jnp.concatenate(
        [_dot(w_sp[g], vn_b[:, g * DA:(g + 1) * DA]) + b_col[:, g:g + 1]
         for g in range(HA)], axis=1)
    z = proj_ref[r, OFF_Z:OFF_Z + E_A].astype(F32)
    return proj_ref[r, OFF_U:OFF_U + E_A].astype(F32) * gate * (z * _sigmoid(z)), vn


def _branch_b_out(o, zb, g_o_ref):
    g_o = g_o_ref[...]
    on = jnp.concatenate(
        [_rms(o[:, h * DV:(h + 1) * DV], g_o[:, h * DV:(h + 1) * DV]) for h in range(HB)], axis=1)
    zb = zb.astype(F32)
    return on * (zb * _sigmoid(zb))


MXU_COLS = 256


def _col_blocks(rows):
    width = MXU_COLS if rows >= 4 * CHUNK else 2 * MXU_COLS
    return [slice(c, c + width) for c in range(0, D_MODEL, width)]


def _merge(y_a, y_b, proj_ref, w_pa_ref, w_pb_ref, m_ref):
    for c in _col_blocks(y_a.shape[0]):
        m_a = proj_ref[:, OFF_MA + c.start:OFF_MA + c.stop].astype(F32)
        m_b = proj_ref[:, OFF_MB + c.start:OFF_MB + c.stop].astype(F32)
        m_ref[:, c] = (_sigmoid(m_a) * _dot(y_a, w_pa_ref[:, c])
                       + _sigmoid(m_b) * _dot(y_b, w_pb_ref[:, c])).astype(m_ref.dtype)


def _chunks(tm):
    return [slice(c * CHUNK, (c + 1) * CHUNK) for c in range(tm // CHUNK)]


def _mix_prompt_kernel(proj_ref, proj_a_ref, g_v_ref, w_s_ref, b_col_ref, w_a2_ref, b_a_ref,
                       g_o_ref, w_pa_ref, w_pb_ref, m_ref, s_out_ref, s_ref, *, tm, seq):
    keep = pl.program_id(1) != 0
    states = [jnp.where(keep, s_ref[h], 0.0) for h in range(HB)]
    w_sp, b_col = _spatial_masked_weights(w_s_ref, b_col_ref, seq)
    t = lax.broadcasted_iota(jnp.int32, (CHUNK, CHUNK), 0)
    s = lax.broadcasted_iota(jnp.int32, (CHUNK, CHUNK), 1)
    tril_ones = jnp.where(s <= t, 1.0, 0.0).astype(BF16)
    masks = _gla_masks(CHUNK)
    log_a = _log_decay(proj_a_ref[...], w_a2_ref[...], b_a_ref[...])
    ya_rows, yb_rows = [], []
    for r in _chunks(tm):
        ya_rows.append(_branch_a_chunk(proj_ref, r, g_v_ref, w_sp, b_col)[0].astype(BF16))
        q = proj_ref[r, OFF_Q:OFF_Q + K_B].astype(F32) * (DK ** -0.5)
        k = proj_ref[r, OFF_K:OFF_K + K_B].astype(F32)
        vb = proj_ref[r, OFF_VB:OFF_VB + E_B]
        o_c, states = _gla_chunk(q, k, vb, log_a[r], states, tril_ones, masks)
        yb_rows.append(_branch_b_out(o_c, proj_ref[r, OFF_ZB:OFF_ZB + E_B], g_o_ref).astype(BF16))
    for h in range(HB):
        s_ref[h] = states[h]
        s_out_ref[0, h] = states[h]
    _merge(jnp.concatenate(ya_rows, axis=0), jnp.concatenate(yb_rows, axis=0),
           proj_ref, w_pa_ref, w_pb_ref, m_ref)


def _mix_decode_kernel(proj_ref, proj_a_ref, s0_ref, g_v_ref, w_s_ref, b_col_ref, w_a2_ref, b_a_ref,
                       g_o_ref, w_pa_ref, w_pb_ref, m_ref, vn_ref, s1_ref, *, nb, seq):
    rows = slice(0, nb * seq)
    log_a = _log_decay(proj_a_ref[...], w_a2_ref[...], b_a_ref[...])
    q = proj_ref[:, OFF_Q:OFF_Q + K_B].astype(F32) * (DK ** -0.5)
    k = proj_ref[:, OFF_K:OFF_K + K_B].astype(F32)
    o = _gla_decode_rows(q, k, proj_ref[:, OFF_VB:OFF_VB + E_B], log_a, s0_ref, s1_ref, nb, seq)
    y_a, vn = _branch_a_chunk(proj_ref, rows, g_v_ref,
                              *_spatial_masked_weights(w_s_ref, b_col_ref, seq))
    vn_ref[...] = vn
    y_b = _branch_b_out(o, proj_ref[:, OFF_ZB:OFF_ZB + E_B], g_o_ref)
    _merge(y_a.astype(BF16), y_b.astype(BF16), proj_ref, w_pa_ref, w_pb_ref, m_ref)


def _out_kernel(m_p_ref, m_s_ref, x_p_ref, x_s_ref, p_p_ref, p_s_ref,
                w_o_ref, g_post_ref, w_pg_ref, w_pe_ref, y_p_ref, y_s_ref, *, n_prompt_tiles):
    is_s = pl.program_id(0) >= n_prompt_tiles
    m = jnp.where(is_s, m_s_ref[...], m_p_ref[...])
    x = jnp.where(is_s, x_s_ref[...], x_p_ref[...])
    p_b = jnp.where(is_s, p_s_ref[...], p_p_ref[...]).astype(BF16)
    blocks = _col_blocks(m.shape[0])
    mo = jnp.concatenate([_dot(m, w_o_ref[:, c]) for c in blocks], axis=1)
    h_res = x + _rms(mo, g_post_ref[...])
    h_b = h_res.astype(BF16)
    for k in range(0, len(blocks), 2):
        cols = slice(blocks[k].start, blocks[k + 1].stop)
        y = jnp.concatenate(
            [h_res[:, c] + _sigmoid(_dot(h_b, w_pg_ref[:, c])) * _dot(p_b, w_pe_ref[:, c])
             for c in blocks[k:k + 2]], axis=1)
        to_s = jnp.broadcast_to(is_s, y.shape)
        pltpu.store(y_p_ref.at[:, cols], y, mask=jnp.logical_not(to_s))
        pltpu.store(y_s_ref.at[:, cols], y, mask=to_s)


def _const_spec(shape):
    nd = len(shape)
    return pl.BlockSpec(shape, lambda *_: (0,) * nd, pipeline_mode=pl.Buffered(1))


def _mix_prompt(proj, proj_a, wts, n_batch, seq, tm):
    nt = seq // tm
    kern = functools.partial(_mix_prompt_kernel, tm=tm, seq=seq)
    row = lambda b, i: (b * nt + i, 0)
    names = ("g_v", "w_s", "b_col", "w_a2", "b_a", "g_o", "w_pa", "w_pb")
    w_args = [wts[n] for n in names]
    return pl.pallas_call(
        kern,
        grid=(n_batch, nt),
        in_specs=[
            pl.BlockSpec((tm, N_MAIN), row),
            pl.BlockSpec((tm, A_PAD), row),
        ] + [_const_spec(w.shape) for w in w_args],
        out_specs=[
            pl.BlockSpec((tm, D_MODEL), row),
            pl.BlockSpec((1, HB, DK, DV), lambda b, i: (b, 0, 0, 0)),
        ],
        out_shape=[
            jax.ShapeDtypeStruct((n_batch * seq, D_MODEL), BF16),
            jax.ShapeDtypeStruct((n_batch, HB, DK, DV), F32),
        ],
        scratch_shapes=[pltpu.VMEM((HB, DK, DV), F32)],
        compiler_params=pltpu.CompilerParams(
            dimension_semantics=("arbitrary", "arbitrary"),
            vmem_limit_bytes=VMEM_LIMIT),
        name="mix_prompt",
    )(proj, proj_a, *w_args)


def _mix_decode(proj, proj_a, s0, wts, nb, seq):
    n_batch = s0.shape[0]
    rows = nb * seq
    assert rows == CHUNK
    kern = functools.partial(_mix_decode_kernel, nb=nb, seq=seq)
    row = lambda i: (i, 0)
    state = lambda i: (i, 0, 0, 0)
    names = ("g_v", "w_s", "b_col", "w_a2", "b_a", "g_o", "w_pa", "w_pb")
    w_args = [wts[n] for n in names]
    return pl.pallas_call(
        kern,
        grid=(n_batch // nb,),
        in_specs=[
            pl.BlockSpec((rows, N_MAIN), row),
            pl.BlockSpec((rows, A_PAD), row),
            pl.BlockSpec((nb, HB, DK, DV), state),
        ] + [_const_spec(w.shape) for w in w_args],
        out_specs=[
            pl.BlockSpec((rows, D_MODEL), row),
            pl.BlockSpec((rows, E_A), row),
            pl.BlockSpec((nb, HB, DK, DV), state),
        ],
        out_shape=[
            jax.ShapeDtypeStruct((n_batch * seq, D_MODEL), BF16),
            jax.ShapeDtypeStruct((n_batch * seq, E_A), F32),
            jax.ShapeDtypeStruct(s0.shape, F32),
        ],
        compiler_params=pltpu.CompilerParams(
            dimension_semantics=("arbitrary",),
            vmem_limit_bytes=VMEM_LIMIT),
        name="mix_decode",
    )(proj, proj_a, s0, *w_args)


def _out_proj(m_p, m_s, x_p, x_s, p_p, p_s, wts, tm):
    n_p, n_s = x_p.shape[0] // tm, x_s.shape[0] // tm
    kern = functools.partial(_out_kernel, n_prompt_tiles=n_p)
    prompt = lambda i: (jnp.minimum(i, n_p - 1), 0)
    sample = lambda i: (jnp.maximum(i - n_p, 0), 0)
    prompt_spec = lambda width: pl.BlockSpec((tm, width), prompt)
    sample_spec = lambda width: pl.BlockSpec((tm, width), sample, pipeline_mode=pl.Buffered(1))
    names = ("w_o", "g_post", "w_pg", "w_pe")
    w_args = [wts[n] for n in names]
    return pl.pallas_call(
        kern,
        grid=(n_p + n_s,),
        in_specs=[
            prompt_spec(D_MODEL), sample_spec(D_MODEL),
            prompt_spec(D_MODEL), sample_spec(D_MODEL),
            prompt_spec(PLE_DIM), sample_spec(PLE_DIM),
        ] + [_const_spec(w.shape) for w in w_args],
        out_specs=[prompt_spec(D_MODEL), sample_spec(D_MODEL)],
        out_shape=[
            jax.ShapeDtypeStruct(x_p.shape, F32),
            jax.ShapeDtypeStruct(x_s.shape, F32),
        ],
        compiler_params=pltpu.CompilerParams(
            dimension_semantics=("arbitrary",),
            vmem_limit_bytes=VMEM_LIMIT),
        name="out_proj",
    )(m_p, m_s, x_p, x_s, p_p, p_s, *w_args)


def _layer_weights(i, g_pre, w_in, g_v, w_s, b_s, w_a2, b_a, g_o, g_post):
    w_a2_p = jnp.concatenate(
        [w_a2[i], jnp.zeros((A_PAD - GATE_RANK, K_B), w_a2.dtype)], axis=0).astype(BF16)
    return dict(
        g_pre=g_pre[i][None, :],
        w_in_t=w_in[i].T,
        g_v=g_v[i][None, :],
        w_s=w_s[i],
        b_col=b_s[i].T,
        w_a2=w_a2_p,
        b_a=b_a[i][None, :],
        g_o=g_o[i].reshape(1, E_B),
        g_post=g_post[i][None, :],
    )


SIDE_WEIGHTS = ("w_pa", "w_pb", "w_o", "w_pg", "w_pe")

INPROJ_TM = 1024
MIX_TM = 512
OUT_TM = 512


def kernel(x_prompt, x_sample, p_prompt, p_sample, state_gla, g_pre, w_in, g_v, w_s, b_s,
           w_a2, b_a, g_o, w_pa, w_pb, w_o, g_post, w_pg, w_pe):
    depth = w_in.shape[0]
    n_batch, seq, _ = x_prompt.shape
    dec_batch, dec_seq, _ = x_sample.shape
    hp = x_prompt.reshape(n_batch * seq, D_MODEL)
    hs = x_sample.reshape(dec_batch * dec_seq, D_MODEL)
    sp_list, ss_list, cv_list = [], [], []
    for i in range(depth):
        wts = _layer_weights(i, g_pre, w_in, g_v, w_s, b_s, w_a2, b_a, g_o, g_post)
        side_f32 = dict(w_pa=w_pa[i], w_pb=w_pb[i], w_o=w_o[i], w_pg=w_pg[i], w_pe=w_pe[i])
        proj_s, proj_a_s, *w_bf = _inproj(hs, wts["g_pre"], wts["w_in_t"], tm=INPROJ_TM)
        proj_p, proj_a_p, *side = _inproj(hp, wts["g_pre"], wts["w_in_t"], tm=INPROJ_TM, w_bf=w_bf,
                                          cast=[side_f32[n] for n in SIDE_WEIGHTS])
        wts.update(zip(SIDE_WEIGHTS, side))
        m_s, v_s, s_s = _mix_decode(proj_s, proj_a_s, state_gla[i], wts,
                                    nb=CHUNK // dec_seq, seq=dec_seq)
        m_p, s_p = _mix_prompt(proj_p, proj_a_p, wts, n_batch, seq, tm=MIX_TM)
        hp, hs = _out_proj(m_p, m_s, hp, hs, p_prompt[i].reshape(n_batch * seq, PLE_DIM),
                           p_sample[i].reshape(dec_batch * dec_seq, PLE_DIM), wts, tm=OUT_TM)
        sp_list.append(s_p)
        ss_list.append(s_s)
        cv_list.append(v_s.reshape(dec_batch, dec_seq, E_A))
    stack = (lambda xs: xs[0][None]) if depth == 1 else jnp.stack
    return (hp.reshape(n_batch, seq, D_MODEL),
            hs.reshape(dec_batch, dec_seq, D_MODEL),
            stack(sp_list), stack(ss_list), stack(cv_list))
```

```python
import functools

import jax
import jax.numpy as jnp
from jax import lax
from jax.experimental import pallas as pl
from jax.experimental.pallas import tpu as pltpu

D_MODEL = 2048
E_A = D_MODEL // 2
HA = 4
DA = E_A // HA
CHUNK = 128
HB = 4
E_B = D_MODEL // 2
DV = E_B // HB
DK = DV // 2
K_B = HB * DK
GATE_RANK = 16
GATE_TAU = 16.0
GLA_SUB = 16
PLE_DIM = 256
EPS = 1e-6

LANES = 128
OFF_U = 0
OFF_V = OFF_U + E_A
OFF_Z = OFF_V + E_A
OFF_Q = OFF_Z + E_A
OFF_K = OFF_Q + K_B
OFF_VB = OFF_K + K_B
OFF_ZB = OFF_VB + E_B
OFF_MA = OFF_ZB + E_B
OFF_MB = OFF_MA + D_MODEL
N_MAIN = OFF_MB + D_MODEL
W_IN_A = OFF_MA
A_PAD = LANES
INPROJ_TN = 1024
INPROJ_TN_BF16 = 2048

F32 = jnp.float32
BF16 = jnp.bfloat16

VMEM_LIMIT = 56 * 1024 * 1024


def _dot(a, b):
    return jnp.dot(a, b, preferred_element_type=F32)


def _dot_nt(a, b):
    return lax.dot_general(a, b, (((1,), (1,)), ((), ())), preferred_element_type=F32)


def _sigmoid(x):
    return jax.nn.sigmoid(x)


def _rms(x, g):
    return x * lax.rsqrt(jnp.mean(x * x, axis=-1, keepdims=True) + EPS) * g


def _split_bf16(x):
    hi = x.astype(BF16)
    lo = (x - hi.astype(F32)).astype(BF16)
    return hi, lo


def _log_decay(a_tile, w_a2, b_a):
    lane = lax.broadcasted_iota(jnp.int32, a_tile.shape, 1)
    a_lr = jnp.where(lane < GATE_RANK, a_tile, jnp.zeros_like(a_tile))
    pre = _dot(a_lr, w_a2) + b_a
    return (jnp.minimum(pre, 0.0) - jnp.log(1.0 + jnp.exp(-jnp.abs(pre)))) * (1.0 / GATE_TAU)


def _bcast_rows(b, idxs, seg):
    return jnp.concatenate(
        [jnp.broadcast_to(b[i:i + 1, :], (seg, b.shape[1])) for i in idxs], axis=0)


BF16_SUBLANES = 16


def _inproj_kernel(*refs, n_cast, emit_w):
    x_ref, g_ref, w_ref, w_a_ref = refs[:4]
    cast_in = refs[4:4 + n_cast]
    o_ref, o_a_ref = refs[4 + n_cast:6 + n_cast]
    cast_out = refs[6 + n_cast:6 + 2 * n_cast]
    rest = refs[6 + 2 * n_cast:]
    xn_ref = rest[-1]

    @pl.when(pl.program_id(1) == 0)
    def _():
        xn = _rms(x_ref[...], g_ref[...]).astype(BF16)
        xn_ref[...] = xn
        w_a = w_a_ref[...]
        if emit_w:
            w_a = w_a.astype(BF16)
            rest[1][...] = w_a
        o_a_ref[...] = _dot_nt(xn, w_a).astype(o_a_ref.dtype)

    for src, dst in zip(cast_in, cast_out):
        dst[...] = src[...].astype(dst.dtype)
    w = w_ref[...]
    if emit_w:
        w = w.astype(BF16)
        rest[0][...] = w
    o_ref[...] = _dot_nt(xn_ref[...], w).astype(o_ref.dtype)


def _inproj(x2d, g_pre, w_in_t, tm, w_bf=None, cast=()):
    t = x2d.shape[0]
    emit_w = w_bf is None
    tn = INPROJ_TN if emit_w else INPROJ_TN_BF16
    n_i, n_j = t // tm, N_MAIN // tn

    def w_rows(i, j):
        row0 = j * tn + jnp.where(j * tn >= OFF_MA, GATE_RANK, 0)
        return (pl.multiple_of(row0, GATE_RANK), 0)

    if emit_w:
        w_args = (w_in_t, w_in_t)
        w_specs = [pl.BlockSpec((pl.Element(tn), pl.Element(D_MODEL)), w_rows),
                   pl.BlockSpec((A_PAD, D_MODEL), lambda i, j: (W_IN_A // A_PAD, 0))]
        w_out_specs = [pl.BlockSpec((tn, D_MODEL), lambda i, j: (j, 0)),
                       pl.BlockSpec((A_PAD, D_MODEL), lambda i, j: (0, 0))]
        w_out_shapes = [jax.ShapeDtypeStruct((N_MAIN, D_MODEL), BF16),
                        jax.ShapeDtypeStruct((A_PAD, D_MODEL), BF16)]
        assert n_i == 1, "the bf16 weight copy is written once, by a single token tile"
    else:
        w_args = w_bf
        w_specs = [pl.BlockSpec((tn, D_MODEL), lambda i, j: (j, 0)),
                   pl.BlockSpec((A_PAD, D_MODEL), lambda i, j: (0, 0))]
        w_out_specs, w_out_shapes = [], []

    cast_specs = []
    for w in cast:
        per_i = w.shape[0] // n_i
        assert per_i * n_i == w.shape[0] and per_i % BF16_SUBLANES == 0
        n_sub = max(n for n in range(1, n_j + 1)
                    if per_i % n == 0 and (per_i // n) % BF16_SUBLANES == 0)
        cast_specs.append(pl.BlockSpec(
            (per_i // n_sub, w.shape[1]),
            lambda i, j, n_sub=n_sub: (i * n_sub + jnp.minimum(j, n_sub - 1), 0)))

    kern = functools.partial(_inproj_kernel, n_cast=len(cast), emit_w=emit_w)
    return pl.pallas_call(
        kern,
        grid=(n_i, n_j),
        in_specs=[
            pl.BlockSpec((tm, D_MODEL), lambda i, j: (i, 0)),
            pl.BlockSpec((1, D_MODEL), lambda i, j: (0, 0)),
        ] + w_specs + cast_specs,
        out_specs=[
            pl.BlockSpec((tm, tn), lambda i, j: (i, j)),
            pl.BlockSpec((tm, A_PAD), lambda i, j: (i, 0)),
        ] + cast_specs + w_out_specs,
        out_shape=[
            jax.ShapeDtypeStruct((t, N_MAIN), BF16),
            jax.ShapeDtypeStruct((t, A_PAD), BF16),
        ] + [jax.ShapeDtypeStruct(w.shape, BF16) for w in cast] + w_out_shapes,
        scratch_shapes=[pltpu.VMEM((tm, D_MODEL), BF16)],
        compiler_params=pltpu.CompilerParams(
            dimension_semantics=("arbitrary", "arbitrary"),
            vmem_limit_bytes=VMEM_LIMIT),
        name="inproj_f32w" if emit_w else "inproj",
    )(x2d, g_pre, *w_args, *cast)


def _gla_levels(c):
    levels = []
    blk = c
    while blk > GLA_SUB:
        per = blk // GLA_SUB
        levels.append((blk, [(j // per) * per + per // 2 - 1 for j in range(c // GLA_SUB)]))
        blk //= 2
    return levels


def _gla_masks(c):
    t = lax.broadcasted_iota(jnp.int32, (c, c), 0)
    s = lax.broadcasted_iota(jnp.int32, (c, c), 1)
    levels = _gla_levels(c)
    masks = []
    for blk, _ in levels:
        sh = blk.bit_length() - 1
        half = blk // 2
        same = (t >> sh) == (s >> sh)
        masks.append(same & ((t & (blk - 1)) >= half) & ((s & (blk - 1)) < half))
    sh = GLA_SUB.bit_length() - 1
    masks.append(((t >> sh) == (s >> sh)) & (s <= t))
    return masks


def _gla_chunk(q, k, vb, log_a, s_heads, tril_ones, masks):
    c = q.shape[0]
    hi, lo = _split_bf16(log_a)
    b = _dot(tril_ones, hi) + _dot(tril_ones, lo)
    n = c // GLA_SUB

    def rep(x):
        return _bcast_rows(x, list(range(n)), GLA_SUB)

    ends = jnp.concatenate([b[(j + 1) * GLA_SUB - 1:(j + 1) * GLA_SUB, :] for j in range(n)], axis=0)
    starts = jnp.concatenate([jnp.zeros((1, b.shape[1]), F32), ends[:n - 1]], axis=0)
    starts_r = rep(starts)
    q_e = q * jnp.exp(b - starts_r)
    k_e = k * jnp.exp(rep(ends) - b)

    qs, ks = [], []
    for _, ref_blocks in _gla_levels(c):
        ref = jnp.concatenate([ends[r:r + 1] for r in ref_blocks], axis=0)
        qs.append((q_e * rep(jnp.exp(jnp.minimum(starts - ref, 0.0)))).astype(BF16))
        ks.append((k_e * rep(jnp.exp(jnp.minimum(ref - ends, 0.0)))).astype(BF16))
    qs.append(q_e.astype(BF16))
    ks.append((k * jnp.exp(starts_r - b)).astype(BF16))

    q_in = (q_e * rep(jnp.exp(starts))).astype(BF16)
    k_out = k_e * rep(jnp.exp(ends[n - 1:n] - ends))

    outs, new_states = [], []
    for h in range(HB):
        kc = slice(h * DK, (h + 1) * DK)
        vc = slice(h * DV, (h + 1) * DV)
        att = jnp.zeros((c, c), F32)
        for ql, kl, m in zip(qs, ks, masks):
            att = jnp.where(m, _dot_nt(ql[:, kc], kl[:, kc]), att)
        v_h = vb[:, vc]
        s_h = s_heads[h]
        o_h = _dot(att.astype(BF16), v_h) + _dot(q_in[:, kc], s_h.astype(BF16))
        b_t = jnp.transpose(b[:, kc])
        decay = jnp.broadcast_to(jnp.exp(b_t[:, c - 1:c]), (DK, DV))
        k_t = jnp.transpose(k_out[:, kc]).astype(BF16)
        new_states.append(decay * s_h + _dot(k_t, v_h))
        outs.append(o_h)
    return jnp.concatenate(outs, axis=1), new_states


def _gla_decode_rows(q, k, vb, log_a, s0_ref, s1_ref, nb, seq):
    rows = nb * seq
    t = lax.broadcasted_iota(jnp.int32, (rows, rows), 0)
    s = lax.broadcasted_iota(jnp.int32, (rows, rows), 1)
    sh = seq.bit_length() - 1
    causal = ((t >> sh) == (s >> sh)) & (s <= t)
    tril_ones = jnp.where(causal, 1.0, 0.0).astype(BF16)
    hi, lo = _split_bf16(log_a)
    b = _dot(tril_ones, hi) + _dot(tril_ones, lo)

    q_t = q * jnp.exp(b)
    k_t = (k * jnp.exp(-b)).astype(BF16)
    b_last = _bcast_rows(b, [n * seq + seq - 1 for n in range(nb)], seq)
    k_out = k * jnp.exp(b_last - b)
    col = lax.broadcasted_iota(jnp.int32, (DK, rows), 1) >> sh

    outs = []
    for h in range(HB):
        kc = slice(h * DK, (h + 1) * DK)
        vc = slice(h * DV, (h + 1) * DV)
        v_h = vb[:, vc]
        att = jnp.where(causal, _dot_nt(q_t[:, kc].astype(BF16), k_t[:, kc]), 0.0)
        o_intra = _dot(att.astype(BF16), v_h)
        decay_tr = jnp.exp(jnp.transpose(b[:, kc]))
        k_tr = jnp.transpose(k_out[:, kc])
        o_inter = []
        for n in range(nb):
            s_n = s0_ref[n, h]
            r0 = n * seq
            o_inter.append(_dot(q_t[r0:r0 + seq, kc], s_n))
            decay = jnp.broadcast_to(decay_tr[:, r0 + seq - 1:r0 + seq], (DK, DV))
            k_n = jnp.where(col == n, k_tr, 0.0).astype(BF16)
            s1_ref[n, h] = decay * s_n + _dot(k_n, v_h)
        outs.append(o_intra + jnp.concatenate(o_inter, axis=0))
    return jnp.concatenate(outs, axis=1)


def _spatial_masked_weights(w_s_ref, b_col_ref, seq):
    t = lax.broadcasted_iota(jnp.int32, (CHUNK, CHUNK), 0)
    s = lax.broadcasted_iota(jnp.int32, (CHUNK, CHUNK), 1)
    causal = s <= t
    if seq >= CHUNK:
        return [jnp.where(causal, w_s_ref[g], 0.0).astype(BF16) for g in range(HA)], b_col_ref[...]
    sh = seq.bit_length() - 1
    causal = causal & ((t >> sh) == (s >> sh))
    pick_rows = jnp.where((s < seq) & ((t & (seq - 1)) == s), 1.0, 0.0)
    pick_cols = jnp.where((t < seq) & ((s & (seq - 1)) == t), 1.0, 0.0)
    w_sp = [jnp.where(causal, _dot(_dot(pick_rows, w_s_ref[g]), pick_cols), 0.0).astype(BF16)
            for g in range(HA)]
    pos = lax.broadcasted_iota(jnp.int32, (CHUNK, HA), 0) & (seq - 1)
    bias = jnp.zeros((CHUNK, HA), F32)
    for i in range(seq):
        bias = jnp.where(pos == i, b_col_ref[i:i + 1, :], bias)
    return w_sp, bias


def _branch_a_chunk(proj_ref, r, g_v_ref, w_sp, b_col):
    vn = _rms(proj_ref[r, OFF_V:OFF_V + E_A].astype(F32), g_v_ref[...])
    vn_b = vn.astype(BF16)
    gate = jnp.concatenate(
        [_dot(w_sp[g], vn_b[:, g * DA:(g + 1) * DA]) + b_col[:, g:g + 1]
         for g in range(HA)], axis=1)
    z = proj_ref[r, OFF_Z:OFF_Z + E_A].astype(F32)
    return proj_ref[r, OFF_U:OFF_U + E_A].astype(F32) * gate * (z * _sigmoid(z)), vn


def _branch_b_out(o, zb, g_o_ref):
    g_o = g_o_ref[...]
    on = jnp.concatenate(
        [_rms(o[:, h * DV:(h + 1) * DV], g_o[:, h * DV:(h + 1) * DV]) for h in range(HB)], axis=1)
    zb = zb.astype(F32)
    return on * (zb * _sigmoid(zb))


MXU_COLS = 256


def _col_blocks(rows):
    width = MXU_COLS if rows >= 4 * CHUNK else 2 * MXU_COLS
    return [slice(c, c + width) for c in range(0, D_MODEL, width)]


def _merge(y_a, y_b, proj_ref, w_pa_ref, w_pb_ref, m_ref):
    for c in _col_blocks(y_a.shape[0]):
        m_a = proj_ref[:, OFF_MA + c.start:OFF_MA + c.stop].astype(F32)
        m_b = proj_ref[:, OFF_MB + c.start:OFF_MB + c.stop].astype(F32)
        m_ref[:, c] = (_sigmoid(m_a) * _dot(y_a, w_pa_ref[:, c])
                       + _sigmoid(m_b) * _dot(y_b, w_pb_ref[:, c])).astype(m_ref.dtype)


def _chunks(tm):
    return [slice(c * CHUNK, (c + 1) * CHUNK) for c in range(tm // CHUNK)]


def _mix_prompt_kernel(proj_ref, proj_a_ref, g_v_ref, w_s_ref, b_col_ref, w_a2_ref, b_a_ref,
                       g_o_ref, w_pa_ref, w_pb_ref, m_ref, s_out_ref, s_ref, *, tm, seq):
    keep = pl.program_id(1) != 0
    states = [jnp.where(keep, s_ref[h], 0.0) for h in range(HB)]
    w_sp, b_col = _spatial_masked_weights(w_s_ref, b_col_ref, seq)
    t = lax.broadcasted_iota(jnp.int32, (CHUNK, CHUNK), 0)
    s = lax.broadcasted_iota(jnp.int32, (CHUNK, CHUNK), 1)
    tril_ones = jnp.where(s <= t, 1.0, 0.0).astype(BF16)
    masks = _gla_masks(CHUNK)
    log_a = _log_decay(proj_a_ref[...], w_a2_ref[...], b_a_ref[...])
    ya_rows, yb_rows = [], []
    for r in _chunks(tm):
        ya_rows.append(_branch_a_chunk(proj_ref, r, g_v_ref, w_sp, b_col)[0].astype(BF16))
        q = proj_ref[r, OFF_Q:OFF_Q + K_B].astype(F32) * (DK ** -0.5)
        k = proj_ref[r, OFF_K:OFF_K + K_B].astype(F32)
        vb = proj_ref[r, OFF_VB:OFF_VB + E_B]
        o_c, states = _gla_chunk(q, k, vb, log_a[r], states, tril_ones, masks)
        yb_rows.append(_branch_b_out(o_c, proj_ref[r, OFF_ZB:OFF_ZB + E_B], g_o_ref).astype(BF16))
    for h in range(HB):
        s_ref[h] = states[h]
        s_out_ref[0, h] = states[h]
    _merge(jnp.concatenate(ya_rows, axis=0), jnp.concatenate(yb_rows, axis=0),
           proj_ref, w_pa_ref, w_pb_ref, m_ref)


def _mix_decode_kernel(proj_ref, proj_a_ref, s0_ref, g_v_ref, w_s_ref, b_col_ref, w_a2_ref, b_a_ref,
                       g_o_ref, w_pa_ref, w_pb_ref, m_ref, vn_ref, s1_ref, *, nb, seq):
    rows = slice(0, nb * seq)
    log_a = _log_decay(proj_a_ref[...], w_a2_ref[...], b_a_ref[...])
    q = proj_ref[:, OFF_Q:OFF_Q + K_B].astype(F32) * (DK ** -0.5)
    k = proj_ref[:, OFF_K:OFF_K + K_B].astype(F32)
    o = _gla_decode_rows(q, k, proj_ref[:, OFF_VB:OFF_VB + E_B], log_a, s0_ref, s1_ref, nb, seq)
    y_a, vn = _branch_a_chunk(proj_ref, rows, g_v_ref,
                              *_spatial_masked_weights(w_s_ref, b_col_ref, seq))
    vn_ref[...] = vn
    y_b = _branch_b_out(o, proj_ref[:, OFF_ZB:OFF_ZB + E_B], g_o_ref)
    _merge(y_a.astype(BF16), y_b.astype(BF16), proj_ref, w_pa_ref, w_pb_ref, m_ref)


def _out_kernel(m_p_ref, m_s_ref, x_p_ref, x_s_ref, p_p_ref, p_s_ref,
                w_o_ref, g_post_ref, w_pg_ref, w_pe_ref, y_p_ref, y_s_ref, *, n_prompt_tiles):
    is_s = pl.program_id(0) >= n_prompt_tiles
    m = jnp.where(is_s, m_s_ref[...], m_p_ref[...])
    x = jnp.where(is_s, x_s_ref[...], x_p_ref[...])
    p_b = jnp.where(is_s, p_s_ref[...], p_p_ref[...]).astype(BF16)
    blocks = _col_blocks(m.shape[0])
    mo = jnp.concatenate([_dot(m, w_o_ref[:, c]) for c in blocks], axis=1)
    h_res = x + _rms(mo, g_post_ref[...])
    h_b = h_res.astype(BF16)
    for k in range(0, len(blocks), 2):
        cols = slice(blocks[k].start, blocks[k + 1].stop)
        y = jnp.concatenate(
            [h_res[:, c] + _sigmoid(_dot(h_b, w_pg_ref[:, c])) * _dot(p_b, w_pe_ref[:, c])
             for c in blocks[k:k + 2]], axis=1)
        to_s = jnp.broadcast_to(is_s, y.shape)
        pltpu.store(y_p_ref.at[:, cols], y, mask=jnp.logical_not(to_s))
        pltpu.store(y_s_ref.at[:, cols], y, mask=to_s)


def _const_spec(shape):
    nd = len(shape)
    return pl.BlockSpec(shape, lambda *_: (0,) * nd, pipeline_mode=pl.Buffered(1))


def _mix_prompt(proj, proj_a, wts, n_batch, seq, tm):
    nt = seq // tm
    kern = functools.partial(_mix_prompt_kernel, tm=tm, seq=seq)
    row = lambda b, i: (b * nt + i, 0)
    names = ("g_v", "w_s", "b_col", "w_a2", "b_a", "g_o", "w_pa", "w_pb")
    w_args = [wts[n] for n in names]
    return pl.pallas_call(
        kern,
        grid=(n_batch, nt),
        in_specs=[
            pl.BlockSpec((tm, N_MAIN), row),
            pl.BlockSpec((tm, A_PAD), row),
        ] + [_const_spec(w.shape) for w in w_args],
        out_specs=[
            pl.BlockSpec((tm, D_MODEL), row),
            pl.BlockSpec((1, HB, DK, DV), lambda b, i: (b, 0, 0, 0)),
        ],
        out_shape=[
            jax.ShapeDtypeStruct((n_batch * seq, D_MODEL), BF16),
            jax.ShapeDtypeStruct((n_batch, HB, DK, DV), F32),
        ],
        scratch_shapes=[pltpu.VMEM((HB, DK, DV), F32)],
        compiler_params=pltpu.CompilerParams(
            dimension_semantics=("arbitrary", "arbitrary"),
            vmem_limit_bytes=VMEM_LIMIT),
        name="mix_prompt",
    )(proj, proj_a, *w_args)


def _mix_decode(proj, proj_a, s0, wts, nb, seq):
    n_batch = s0.shape[0]
    rows = nb * seq
    assert rows == CHUNK
    kern = functools.partial(_mix_decode_kernel, nb=nb, seq=seq)
    row = lambda i: (i, 0)
    state = lambda i: (i, 0, 0, 0)
    names = ("g_v", "w_s", "b_col", "w_a2", "b_a", "g_o", "w_pa", "w_pb")
    w_args = [wts[n] for n in names]
    return pl.pallas_call(
        kern,
        grid=(n_batch // nb,),
        in_specs=[
            pl.BlockSpec((rows, N_MAIN), row),
            pl.BlockSpec((rows, A_PAD), row),
            pl.BlockSpec((nb, HB, DK, DV), state),
        ] + [_const_spec(w.shape) for w in w_args],
        out_specs=[
            pl.BlockSpec((rows, D_MODEL), row),
            pl.BlockSpec((rows, E_A), row),
            pl.BlockSpec((nb, HB, DK, DV), state),
        ],
        out_shape=[
            jax.ShapeDtypeStruct((n_batch * seq, D_MODEL), BF16),
            jax.ShapeDtypeStruct((n_batch * seq, E_A), F32),
            jax.ShapeDtypeStruct(s0.shape, F32),
        ],
        compiler_params=pltpu.CompilerParams(
            dimension_semantics=("arbitrary",),
            vmem_limit_bytes=VMEM_LIMIT),
        name="mix_decode",
    )(proj, proj_a, s0, *w_args)


def _out_proj(m_p, m_s, x_p, x_s, p_p, p_s, wts, tm):
    n_p, n_s = x_p.shape[0] // tm, x_s.shape[0] // tm
    kern = functools.partial(_out_kernel, n_prompt_tiles=n_p)
    prompt = lambda i: (jnp.minimum(i, n_p - 1), 0)
    sample = lambda i: (jnp.maximum(i - n_p, 0), 0)
    prompt_spec = lambda width: pl.BlockSpec((tm, width), prompt)
    sample_spec = lambda width: pl.BlockSpec((tm, width), sample, pipeline_mode=pl.Buffered(1))
    names = ("w_o", "g_post", "w_pg", "w_pe")
    w_args = [wts[n] for n in names]
    return pl.pallas_call(
        kern,
        grid=(n_p + n_s,),
        in_specs=[
            prompt_spec(D_MODEL), sample_spec(D_MODEL),
            prompt_spec(D_MODEL), sample_spec(D_MODEL),
            prompt_spec(PLE_DIM), sample_spec(PLE_DIM),
        ] + [_const_spec(w.shape) for w in w_args],
        out_specs=[prompt_spec(D_MODEL), sample_spec(D_MODEL)],
        out_shape=[
            jax.ShapeDtypeStruct(x_p.shape, F32),
            jax.ShapeDtypeStruct(x_s.shape, F32),
        ],
        compiler_params=pltpu.CompilerParams(
            dimension_semantics=("arbitrary",),
            vmem_limit_bytes=VMEM_LIMIT),
        name="out_proj",
    )(m_p, m_s, x_p, x_s, p_p, p_s, *w_args)


def _layer_weights(i, g_pre, w_in, g_v, w_s, b_s, w_a2, b_a, g_o, g_post):
    w_a2_p = jnp.concatenate(
        [w_a2[i], jnp.zeros((A_PAD - GATE_RANK, K_B), w_a2.dtype)], axis=0).astype(BF16)
    return dict(
        g_pre=g_pre[i][None, :],
        w_in_t=w_in[i].T,
        g_v=g_v[i][None, :],
        w_s=w_s[i],
        b_col=b_s[i].T,
        w_a2=w_a2_p,
        b_a=b_a[i][None, :],
        g_o=g_o[i].reshape(1, E_B),
        g_post=g_post[i][None, :],
    )


SIDE_WEIGHTS = ("w_pa", "w_pb", "w_o", "w_pg", "w_pe")

INPROJ_TM = 1024
MIX_TM = 512
OUT_TM = 512


def kernel(x_prompt, x_sample, p_prompt, p_sample, state_gla, g_pre, w_in, g_v, w_s, b_s,
           w_a2, b_a, g_o, w_pa, w_pb, w_o, g_post, w_pg, w_pe):
    depth = w_in.shape[0]
    n_batch, seq, _ = x_prompt.shape
    dec_batch, dec_seq, _ = x_sample.shape
    hp = x_prompt.reshape(n_batch * seq, D_MODEL)
    hs = x_sample.reshape(dec_batch * dec_seq, D_MODEL)
    sp_list, ss_list, cv_list = [], [], []
    for i in range(depth):
        wts = _layer_weights(i, g_pre, w_in, g_v, w_s, b_s, w_a2, b_a, g_o, g_post)
        side_f32 = dict(w_pa=w_pa[i], w_pb=w_pb[i], w_o=w_o[i], w_pg=w_pg[i], w_pe=w_pe[i])
        proj_s, proj_a_s, *w_bf = _inproj(hs, wts["g_pre"], wts["w_in_t"], tm=INPROJ_TM)
        proj_p, proj_a_p, *side = _inproj(hp, wts["g_pre"], wts["w_in_t"], tm=INPROJ_TM, w_bf=w_bf,
                                          cast=[side_f32[n] for n in SIDE_WEIGHTS])
        wts.update(zip(SIDE_WEIGHTS, side))
        m_s, v_s, s_s = _mix_decode(proj_s, proj_a_s, state_gla[i], wts,
                                    nb=CHUNK // dec_seq, seq=dec_seq)
        m_p, s_p = _mix_prompt(proj_p, proj_a_p, wts, n_batch, seq, tm=MIX_TM)
        hp, hs = _out_proj(m_p, m_s, hp, hs, p_prompt[i].reshape(n_batch * seq, PLE_DIM),
                           p_sample[i].reshape(dec_batch * dec_seq, PLE_DIM), wts, tm=OUT_TM)
        sp_list.append(s_p)
        ss_list.append(s_s)
        cv_list.append(v_s.reshape(dec_batch, dec_seq, E_A))
    stack = (lambda xs: xs[0][None]) if depth == 1 else jnp.stack
    return (hp.reshape(n_batch, seq, D_MODEL),
            hs.reshape(dec_batch, dec_seq, D_MODEL),
            stack(sp_list), stack(ss_list), stack(cv_list))
```

```python
import functools

import jax
import jax.numpy as jnp
from jax import lax
from jax.experimental import pallas as pl
from jax.experimental.pallas import tpu as pltpu

D_MODEL = 2048
E_A = D_MODEL // 2
HA = 4
DA = E_A // HA
CHUNK = 128
HB = 4
E_B = D_MODEL // 2
DV = E_B // HB
DK = DV // 2
K_B = HB * DK
GATE_RANK = 16
GATE_TAU = 16.0
GLA_SUB = 16
PLE_DIM = 256
EPS = 1e-6

LANES = 128
OFF_U = 0
OFF_V = OFF_U + E_A
OFF_Z = OFF_V + E_A
OFF_Q = OFF_Z + E_A
OFF_K = OFF_Q + K_B
OFF_VB = OFF_K + K_B
OFF_ZB = OFF_VB + E_B
OFF_MA = OFF_ZB + E_B
OFF_MB = OFF_MA + D_MODEL
N_MAIN = OFF_MB + D_MODEL
W_IN_A = OFF_MA
A_PAD = LANES
INPROJ_TN = 1024
INPROJ_TN_BF16 = 2048

F32 = jnp.float32
BF16 = jnp.bfloat16

VMEM_LIMIT = 56 * 1024 * 1024


def _dot(a, b):
    return jnp.dot(a, b, preferred_element_type=F32)


def _dot_nt(a, b):
    return lax.dot_general(a, b, (((1,), (1,)), ((), ())), preferred_element_type=F32)


def _sigmoid(x):
    return jax.nn.sigmoid(x)


def _rms(x, g):
    return x * lax.rsqrt(jnp.mean(x * x, axis=-1, keepdims=True) + EPS) * g


def _split_bf16(x):
    hi = x.astype(BF16)
    lo = (x - hi.astype(F32)).astype(BF16)
    return hi, lo


def _log_decay(a_tile, w_a2, b_a):
    lane = lax.broadcasted_iota(jnp.int32, a_tile.shape, 1)
    a_lr = jnp.where(lane < GATE_RANK, a_tile, jnp.zeros_like(a_tile))
    pre = _dot(a_lr, w_a2) + b_a
    return (jnp.minimum(pre, 0.0) - jnp.log(1.0 + jnp.exp(-jnp.abs(pre)))) * (1.0 / GATE_TAU)


def _bcast_rows(b, idxs, seg):
    return jnp.concatenate(
        [jnp.broadcast_to(b[i:i + 1, :], (seg, b.shape[1])) for i in idxs], axis=0)


BF16_SUBLANES = 16


def _inproj_kernel(*refs, n_cast, emit_w):
    x_ref, g_ref, w_ref, w_a_ref = refs[:4]
    cast_in = refs[4:4 + n_cast]
    o_ref, o_a_ref = refs[4 + n_cast:6 + n_cast]
    cast_out = refs[6 + n_cast:6 + 2 * n_cast]
    rest = refs[6 + 2 * n_cast:]
    xn_ref = rest[-1]

    @pl.when(pl.program_id(1) == 0)
    def _():
        xn = _rms(x_ref[...], g_ref[...]).astype(BF16)
        xn_ref[...] = xn
        w_a = w_a_ref[...]
        if emit_w:
            w_a = w_a.astype(BF16)
            rest[1][...] = w_a
        o_a_ref[...] = _dot_nt(xn, w_a).astype(o_a_ref.dtype)

    for src, dst in zip(cast_in, cast_out):
        dst[...] = src[...].astype(dst.dtype)
    w = w_ref[...]
    if emit_w:
        w = w.astype(BF16)
        rest[0][...] = w
    o_ref[...] = _dot_nt(xn_ref[...], w).astype(o_ref.dtype)


def _inproj(x2d, g_pre, w_in_t, tm, w_bf=None, cast=()):
    t = x2d.shape[0]
    emit_w = w_bf is None
    tn = INPROJ_TN if emit_w else INPROJ_TN_BF16
    n_i, n_j = t // tm, N_MAIN // tn

    def w_rows(i, j):
        row0 = j * tn + jnp.where(j * tn >= OFF_MA, GATE_RANK, 0)
        return (pl.multiple_of(row0, GATE_RANK), 0)

    if emit_w:
        w_args = (w_in_t, w_in_t)
        w_specs = [pl.BlockSpec((pl.Element(tn), pl.Element(D_MODEL)), w_rows),
                   pl.BlockSpec((A_PAD, D_MODEL), lambda i, j: (W_IN_A // A_PAD, 0))]
        w_out_specs = [pl.BlockSpec((tn, D_MODEL), lambda i, j: (j, 0)),
                       pl.BlockSpec((A_PAD, D_MODEL), lambda i, j: (0, 0))]
        w_out_shapes = [jax.ShapeDtypeStruct((N_MAIN, D_MODEL), BF16),
                        jax.ShapeDtypeStruct((A_PAD, D_MODEL), BF16)]
        assert n_i == 1, "the bf16 weight copy is written once, by a single token tile"
    else:
        w_args = w_bf
        w_specs = [pl.BlockSpec((tn, D_MODEL), lambda i, j: (j, 0)),
                   pl.BlockSpec((A_PAD, D_MODEL), lambda i, j: (0, 0))]
        w_out_specs, w_out_shapes = [], []

    cast_specs = []
    for w in cast:
        per_i = w.shape[0] // n_i
        assert per_i * n_i == w.shape[0] and per_i % BF16_SUBLANES == 0
        n_sub = max(n for n in range(1, n_j + 1)
                    if per_i % n == 0 and (per_i // n) % BF16_SUBLANES == 0)
        cast_specs.append(pl.BlockSpec(
            (per_i // n_sub, w.shape[1]),
            lambda i, j, n_sub=n_sub: (i * n_sub + jnp.minimum(j, n_sub - 1), 0)))

    kern = functools.partial(_inproj_kernel, n_cast=len(cast), emit_w=emit_w)
    return pl.pallas_call(
        kern,
        grid=(n_i, n_j),
        in_specs=[
            pl.BlockSpec((tm, D_MODEL), lambda i, j: (i, 0)),
            pl.BlockSpec((1, D_MODEL), lambda i, j: (0, 0)),
        ] + w_specs + cast_specs,
        out_specs=[
            pl.BlockSpec((tm, tn), lambda i, j: (i, j)),
            pl.BlockSpec((tm, A_PAD), lambda i, j: (i, 0)),
        ] + cast_specs + w_out_specs,
        out_shape=[
            jax.ShapeDtypeStruct((t, N_MAIN), BF16),
            jax.ShapeDtypeStruct((t, A_PAD), BF16),
        ] + [jax.ShapeDtypeStruct(w.shape, BF16) for w in cast] + w_out_shapes,
        scratch_shapes=[pltpu.VMEM((tm, D_MODEL), BF16)],
        compiler_params=pltpu.CompilerParams(
            dimension_semantics=("arbitrary", "arbitrary"),
            vmem_limit_bytes=VMEM_LIMIT),
        name="inproj_f32w" if emit_w else "inproj",
    )(x2d, g_pre, *w_args, *cast)


def _gla_levels(c):
    levels = []
    blk = c
    while blk > GLA_SUB:
        per = blk // GLA_SUB
        levels.append((blk, [(j // per) * per + per // 2 - 1 for j in range(c // GLA_SUB)]))
        blk //= 2
    return levels


def _gla_masks(c):
    t = lax.broadcasted_iota(jnp.int32, (c, c), 0)
    s = lax.broadcasted_iota(jnp.int32, (c, c), 1)
    levels = _gla_levels(c)
    masks = []
    for blk, _ in levels:
        sh = blk.bit_length() - 1
        half = blk // 2
        same = (t >> sh) == (s >> sh)
        masks.append(same & ((t & (blk - 1)) >= half) & ((s & (blk - 1)) < half))
    sh = GLA_SUB.bit_length() - 1
    masks.append(((t >> sh) == (s >> sh)) & (s <= t))
    return masks


def _gla_chunk(q, k, vb, log_a, s_heads, tril_ones, masks):
    c = q.shape[0]
    hi, lo = _split_bf16(log_a)
    b = _dot(tril_ones, hi) + _dot(tril_ones, lo)
    n = c // GLA_SUB

    def rep(x):
        return _bcast_rows(x, list(range(n)), GLA_SUB)

    ends = jnp.concatenate([b[(j + 1) * GLA_SUB - 1:(j + 1) * GLA_SUB, :] for j in range(n)], axis=0)
    starts = jnp.concatenate([jnp.zeros((1, b.shape[1]), F32), ends[:n - 1]], axis=0)
    starts_r = rep(starts)
    q_e = q * jnp.exp(b - starts_r)
    k_e = k * jnp.exp(rep(ends) - b)

    qs, ks = [], []
    for _, ref_blocks in _gla_levels(c):
        ref = jnp.concatenate([ends[r:r + 1] for r in ref_blocks], axis=0)
        qs.append((q_e * rep(jnp.exp(jnp.minimum(starts - ref, 0.0)))).astype(BF16))
        ks.append((k_e * rep(jnp.exp(jnp.minimum(ref - ends, 0.0)))).astype(BF16))
    qs.append(q_e.astype(BF16))
    ks.append((k * jnp.exp(starts_r - b)).astype(BF16))

    q_in = (q_e * rep(jnp.exp(starts))).astype(BF16)
    k_out = k_e * rep(jnp.exp(ends[n - 1:n] - ends))

    atts = []
    zero = jnp.zeros((c, DK), BF16)
    for h in range(0, HB, 2):
        pair = slice(h * DK, (h + 2) * DK)
        att2 = jnp.zeros((c, 2 * c), F32)
        for ql, kl, m in zip(qs, ks, masks):
            keys = jnp.concatenate(
                [jnp.concatenate([kl[:, h * DK:(h + 1) * DK], zero], axis=1),
                 jnp.concatenate([zero, kl[:, (h + 1) * DK:(h + 2) * DK]], axis=1)], axis=0)
            att2 = jnp.where(jnp.concatenate([m, m], axis=1), _dot_nt(ql[:, pair], keys), att2)
        atts += [att2[:, :c], att2[:, c:]]

    outs, new_states = [], []
    for h in range(HB):
        kc = slice(h * DK, (h + 1) * DK)
        vc = slice(h * DV, (h + 1) * DV)
        att = atts[h]
        v_h = vb[:, vc]
        s_h = s_heads[h]
        o_h = _dot(att.astype(BF16), v_h) + _dot(q_in[:, kc], s_h.astype(BF16))
        b_t = jnp.transpose(b[:, kc])
        decay = jnp.broadcast_to(jnp.exp(b_t[:, c - 1:c]), (DK, DV))
        k_t = jnp.transpose(k_out[:, kc]).astype(BF16)
        new_states.append(decay * s_h + _dot(k_t, v_h))
        outs.append(o_h)
    return jnp.concatenate(outs, axis=1), new_states


def _gla_decode_rows(q, k, vb, log_a, s0_ref, s1_ref, nb, seq):
    rows = nb * seq
    t = lax.broadcasted_iota(jnp.int32, (rows, rows), 0)
    s = lax.broadcasted_iota(jnp.int32, (rows, rows), 1)
    sh = seq.bit_length() - 1
    causal = ((t >> sh) == (s >> sh)) & (s <= t)
    tril_ones = jnp.where(causal, 1.0, 0.0).astype(BF16)
    hi, lo = _split_bf16(log_a)
    b = _dot(tril_ones, hi) + _dot(tril_ones, lo)

    q_t = q * jnp.exp(b)
    k_t = (k * jnp.exp(-b)).astype(BF16)
    b_last = _bcast_rows(b, [n * seq + seq - 1 for n in range(nb)], seq)
    k_out = k * jnp.exp(b_last - b)
    col = lax.broadcasted_iota(jnp.int32, (DK, rows), 1) >> sh

    outs = []
    for h in range(HB):
        kc = slice(h * DK, (h + 1) * DK)
        vc = slice(h * DV, (h + 1) * DV)
        v_h = vb[:, vc]
        att = jnp.where(causal, _dot_nt(q_t[:, kc].astype(BF16), k_t[:, kc]), 0.0)
        o_intra = _dot(att.astype(BF16), v_h)
        decay_tr = jnp.exp(jnp.transpose(b[:, kc]))
        k_tr = jnp.transpose(k_out[:, kc])
        o_inter = []
        for n in range(nb):
            s_n = s0_ref[n, h]
            r0 = n * seq
            o_inter.append(_dot(q_t[r0:r0 + seq, kc], s_n))
            decay = jnp.broadcast_to(decay_tr[:, r0 + seq - 1:r0 + seq], (DK, DV))
            k_n = jnp.where(col == n, k_tr, 0.0).astype(BF16)
            s1_ref[n, h] = decay * s_n + _dot(k_n, v_h)
        outs.append(o_intra + jnp.concatenate(o_inter, axis=0))
    return jnp.concatenate(outs, axis=1)


def _spatial_masked_weights(w_s_ref, b_col_ref, seq):
    t = lax.broadcasted_iota(jnp.int32, (CHUNK, CHUNK), 0)
    s = lax.broadcasted_iota(jnp.int32, (CHUNK, CHUNK), 1)
    causal = s <= t
    if seq >= CHUNK:
        return [jnp.where(causal, w_s_ref[g], 0.0).astype(BF16) for g in range(HA)], b_col_ref[...]
    sh = seq.bit_length() - 1
    causal = causal & ((t >> sh) == (s >> sh))
    pick_rows = jnp.where((s < seq) & ((t & (seq - 1)) == s), 1.0, 0.0)
    pick_cols = jnp.where((t < seq) & ((s & (seq - 1)) == t), 1.0, 0.0)
    w_sp = [jnp.where(causal, _dot(_dot(pick_rows, w_s_ref[g]), pick_cols), 0.0).astype(BF16)
            for g in range(HA)]
    pos = lax.broadcasted_iota(jnp.int32, (CHUNK, HA), 0) & (seq - 1)
    bias = jnp.zeros((CHUNK, HA), F32)
    for i in range(seq):
        bias = jnp.where(pos == i, b_col_ref[i:i + 1, :], bias)
    return w_sp, bias


def _branch_a_chunk(proj_ref, r, g_v_ref, w_sp, b_col):
    vn = _rms(proj_ref[r, OFF_V:OFF_V + E_A].astype(F32), g_v_ref[...])
    vn_b = vn.astype(BF16)
    gate = jnp.concatenate(
        [_dot(w_sp[g], vn_b[:, g * DA:(g + 1) * DA]) + b_col[:, g:g + 1]
         for g in range(HA)], axis=1)
    z = proj_ref[r, OFF_Z:OFF_Z + E_A].astype(F32)
    return proj_ref[r, OFF_U:OFF_U + E_A].astype(F32) * gate * (z * _sigmoid(z)), vn


def _branch_b_out(o, zb, g_o_ref):
    g_o = g_o_ref[...]
    on = jnp.concatenate(
        [_rms(o[:, h * DV:(h + 1) * DV], g_o[:, h * DV:(h + 1) * DV]) for h in range(HB)], axis=1)
    zb = zb.astype(F32)
    return on * (zb * _sigmoid(zb))


MXU_COLS = 256


def _col_blocks(rows):
    width = MXU_COLS if rows >= 4 * CHUNK else 2 * MXU_COLS
    return [slice(c, c + width) for c in range(0, D_MODEL, width)]


def _merge(y_a, y_b, proj_ref, w_pa_ref, w_pb_ref, m_ref):
    for c in _col_blocks(y_a.shape[0]):
        m_a = proj_ref[:, OFF_MA + c.start:OFF_MA + c.stop].astype(F32)
        m_b = proj_ref[:, OFF_MB + c.start:OFF_MB + c.stop].astype(F32)
        m_ref[:, c] = (_sigmoid(m_a) * _dot(y_a, w_pa_ref[:, c])
                       + _sigmoid(m_b) * _dot(y_b, w_pb_ref[:, c])).astype(m_ref.dtype)


def _chunks(tm):
    return [slice(c * CHUNK, (c + 1) * CHUNK) for c in range(tm // CHUNK)]


def _mix_prompt_kernel(proj_ref, proj_a_ref, g_v_ref, w_s_ref, b_col_ref, w_a2_ref, b_a_ref,
                       g_o_ref, w_pa_ref, w_pb_ref, m_ref, s_out_ref, s_ref, *, tm, seq):
    keep = pl.program_id(1) != 0
    states = [jnp.where(keep, s_ref[h], 0.0) for h in range(HB)]
    w_sp, b_col = _spatial_masked_weights(w_s_ref, b_col_ref, seq)
    t = lax.broadcasted_iota(jnp.int32, (CHUNK, CHUNK), 0)
    s = lax.broadcasted_iota(jnp.int32, (CHUNK, CHUNK), 1)
    tril_ones = jnp.where(s <= t, 1.0, 0.0).astype(BF16)
    masks = _gla_masks(CHUNK)
    log_a = _log_decay(proj_a_ref[...], w_a2_ref[...], b_a_ref[...])
    ya_rows, yb_rows = [], []
    for r in _chunks(tm):
        ya_rows.append(_branch_a_chunk(proj_ref, r, g_v_ref, w_sp, b_col)[0].astype(BF16))
        q = proj_ref[r, OFF_Q:OFF_Q + K_B].astype(F32) * (DK ** -0.5)
        k = proj_ref[r, OFF_K:OFF_K + K_B].astype(F32)
        vb = proj_ref[r, OFF_VB:OFF_VB + E_B]
        o_c, states = _gla_chunk(q, k, vb, log_a[r], states, tril_ones, masks)
        yb_rows.append(_branch_b_out(o_c, proj_ref[r, OFF_ZB:OFF_ZB + E_B], g_o_ref).astype(BF16))
    for h in range(HB):
        s_ref[h] = states[h]
        s_out_ref[0, h] = states[h]
    _merge(jnp.concatenate(ya_rows, axis=0), jnp.concatenate(yb_rows, axis=0),
           proj_ref, w_pa_ref, w_pb_ref, m_ref)


def _mix_decode_kernel(proj_ref, proj_a_ref, s0_ref, g_v_ref, w_s_ref, b_col_ref, w_a2_ref, b_a_ref,
                       g_o_ref, w_pa_ref, w_pb_ref, m_ref, vn_ref, s1_ref, *, nb, seq):
    rows = slice(0, nb * seq)
    log_a = _log_decay(proj_a_ref[...], w_a2_ref[...], b_a_ref[...])
    q = proj_ref[:, OFF_Q:OFF_Q + K_B].astype(F32) * (DK ** -0.5)
    k = proj_ref[:, OFF_K:OFF_K + K_B].astype(F32)
    o = _gla_decode_rows(q, k, proj_ref[:, OFF_VB:OFF_VB + E_B], log_a, s0_ref, s1_ref, nb, seq)
    y_a, vn = _branch_a_chunk(proj_ref, rows, g_v_ref,
                              *_spatial_masked_weights(w_s_ref, b_col_ref, seq))
    vn_ref[...] = vn
    y_b = _branch_b_out(o, proj_ref[:, OFF_ZB:OFF_ZB + E_B], g_o_ref)
    _merge(y_a.astype(BF16), y_b.astype(BF16), proj_ref, w_pa_ref, w_pb_ref, m_ref)


def _out_kernel(m_p_ref, m_s_ref, x_p_ref, x_s_ref, p_p_ref, p_s_ref,
                w_o_ref, g_post_ref, w_pg_ref, w_pe_ref, y_p_ref, y_s_ref, *, n_prompt_tiles):
    is_s = pl.program_id(0) >= n_prompt_tiles
    m = jnp.where(is_s, m_s_ref[...], m_p_ref[...])
    x = jnp.where(is_s, x_s_ref[...], x_p_ref[...])
    p_b = jnp.where(is_s, p_s_ref[...], p_p_ref[...]).astype(BF16)
    blocks = _col_blocks(m.shape[0])
    mo = jnp.concatenate([_dot(m, w_o_ref[:, c]) for c in blocks], axis=1)
    h_res = x + _rms(mo, g_post_ref[...])
    h_b = h_res.astype(BF16)
    for k in range(0, len(blocks), 2):
        cols = slice(blocks[k].start, blocks[k + 1].stop)
        y = jnp.concatenate(
            [h_res[:, c] + _sigmoid(_dot(h_b, w_pg_ref[:, c])) * _dot(p_b, w_pe_ref[:, c])
             for c in blocks[k:k + 2]], axis=1)
        to_s = jnp.broadcast_to(is_s, y.shape)
        pltpu.store(y_p_ref.at[:, cols], y, mask=jnp.logical_not(to_s))
        pltpu.store(y_s_ref.at[:, cols], y, mask=to_s)


def _const_spec(shape):
    nd = len(shape)
    return pl.BlockSpec(shape, lambda *_: (0,) * nd, pipeline_mode=pl.Buffered(1))


def _mix_prompt(proj, proj_a, wts, n_batch, seq, tm):
    nt = seq // tm
    kern = functools.partial(_mix_prompt_kernel, tm=tm, seq=seq)
    row = lambda b, i: (b * nt + i, 0)
    names = ("g_v", "w_s", "b_col", "w_a2", "b_a", "g_o", "w_pa", "w_pb")
    w_args = [wts[n] for n in names]
    return pl.pallas_call(
        kern,
        grid=(n_batch, nt),
        in_specs=[
            pl.BlockSpec((tm, N_MAIN), row),
            pl.BlockSpec((tm, A_PAD), row),
        ] + [_const_spec(w.shape) for w in w_args],
        out_specs=[
            pl.BlockSpec((tm, D_MODEL), row),
            pl.BlockSpec((1, HB, DK, DV), lambda b, i: (b, 0, 0, 0)),
        ],
        out_shape=[
            jax.ShapeDtypeStruct((n_batch * seq, D_MODEL), BF16),
            jax.ShapeDtypeStruct((n_batch, HB, DK, DV), F32),
        ],
        scratch_shapes=[pltpu.VMEM((HB, DK, DV), F32)],
        compiler_params=pltpu.CompilerParams(
            dimension_semantics=("arbitrary", "arbitrary"),
            vmem_limit_bytes=VMEM_LIMIT),
        name="mix_prompt",
    )(proj, proj_a, *w_args)


def _mix_decode(proj, proj_a, s0, wts, nb, seq):
    n_batch = s0.shape[0]
    rows = nb * seq
    assert rows == CHUNK
    kern = functools.partial(_mix_decode_kernel, nb=nb, seq=seq)
    row = lambda i: (i, 0)
    state = lambda i: (i, 0, 0, 0)
    names = ("g_v", "w_s", "b_col", "w_a2", "b_a", "g_o", "w_pa", "w_pb")
    w_args = [wts[n] for n in names]
    return pl.pallas_call(
        kern,
        grid=(n_batch // nb,),
        in_specs=[
            pl.BlockSpec((rows, N_MAIN), row),
            pl.BlockSpec((rows, A_PAD), row),
            pl.BlockSpec((nb, HB, DK, DV), state),
        ] + [_const_spec(w.shape) for w in w_args],
        out_specs=[
            pl.BlockSpec((rows, D_MODEL), row),
            pl.BlockSpec((rows, E_A), row),
            pl.BlockSpec((nb, HB, DK, DV), state),
        ],
        out_shape=[
            jax.ShapeDtypeStruct((n_batch * seq, D_MODEL), BF16),
            jax.ShapeDtypeStruct((n_batch * seq, E_A), F32),
            jax.ShapeDtypeStruct(s0.shape, F32),
        ],
        compiler_params=pltpu.CompilerParams(
            dimension_semantics=("arbitrary",),
            vmem_limit_bytes=VMEM_LIMIT),
        name="mix_decode",
    )(proj, proj_a, s0, *w_args)


def _out_proj(m_p, m_s, x_p, x_s, p_p, p_s, wts, tm):
    n_p, n_s = x_p.shape[0] // tm, x_s.shape[0] // tm
    kern = functools.partial(_out_kernel, n_prompt_tiles=n_p)
    prompt = lambda i: (jnp.minimum(i, n_p - 1), 0)
    sample = lambda i: (jnp.maximum(i - n_p, 0), 0)
    prompt_spec = lambda width: pl.BlockSpec((tm, width), prompt)
    sample_spec = lambda width: pl.BlockSpec((tm, width), sample, pipeline_mode=pl.Buffered(1))
    names = ("w_o", "g_post", "w_pg", "w_pe")
    w_args = [wts[n] for n in names]
    return pl.pallas_call(
        kern,
        grid=(n_p + n_s,),
        in_specs=[
            prompt_spec(D_MODEL), sample_spec(D_MODEL),
            prompt_spec(D_MODEL), sample_spec(D_MODEL),
            prompt_spec(PLE_DIM), sample_spec(PLE_DIM),
        ] + [_const_spec(w.shape) for w in w_args],
        out_specs=[prompt_spec(D_MODEL), sample_spec(D_MODEL)],
        out_shape=[
            jax.ShapeDtypeStruct(x_p.shape, F32),
            jax.ShapeDtypeStruct(x_s.shape, F32),
        ],
        compiler_params=pltpu.CompilerParams(
            dimension_semantics=("arbitrary",),
            vmem_limit_bytes=VMEM_LIMIT),
        name="out_proj",
    )(m_p, m_s, x_p, x_s, p_p, p_s, *w_args)


def _layer_weights(i, g_pre, w_in, g_v, w_s, b_s, w_a2, b_a, g_o, g_post):
    w_a2_p = jnp.concatenate(
        [w_a2[i], jnp.zeros((A_PAD - GATE_RANK, K_B), w_a2.dtype)], axis=0).astype(BF16)
    return dict(
        g_pre=g_pre[i][None, :],
        w_in_t=w_in[i].T,
        g_v=g_v[i][None, :],
        w_s=w_s[i],
        b_col=b_s[i].T,
        w_a2=w_a2_p,
        b_a=b_a[i][None, :],
        g_o=g_o[i].reshape(1, E_B),
        g_post=g_post[i][None, :],
    )


SIDE_WEIGHTS = ("w_pa", "w_pb", "w_o", "w_pg", "w_pe")

INPROJ_TM = 1024
MIX_TM = 512
OUT_TM = 512


def kernel(x_prompt, x_sample, p_prompt, p_sample, state_gla, g_pre, w_in, g_v, w_s, b_s,
           w_a2, b_a, g_o, w_pa, w_pb, w_o, g_post, w_pg, w_pe):
    depth = w_in.shape[0]
    n_batch, seq, _ = x_prompt.shape
    dec_batch, dec_seq, _ = x_sample.shape
    hp = x_prompt.reshape(n_batch * seq, D_MODEL)
    hs = x_sample.reshape(dec_batch * dec_seq, D_MODEL)
    sp_list, ss_list, cv_list = [], [], []
    for i in range(depth):
        wts = _layer_weights(i, g_pre, w_in, g_v, w_s, b_s, w_a2, b_a, g_o, g_post)
        side_f32 = dict(w_pa=w_pa[i], w_pb=w_pb[i], w_o=w_o[i], w_pg=w_pg[i], w_pe=w_pe[i])
        proj_s, proj_a_s, *w_bf = _inproj(hs, wts["g_pre"], wts["w_in_t"], tm=INPROJ_TM)
        proj_p, proj_a_p, *side = _inproj(hp, wts["g_pre"], wts["w_in_t"], tm=INPROJ_TM, w_bf=w_bf,
                                          cast=[side_f32[n] for n in SIDE_WEIGHTS])
        wts.update(zip(SIDE_WEIGHTS, side))
        m_s, v_s, s_s = _mix_decode(proj_s, proj_a_s, state_gla[i], wts,
                                    nb=CHUNK // dec_seq, seq=dec_seq)
        m_p, s_p = _mix_prompt(proj_p, proj_a_p, wts, n_batch, seq, tm=MIX_TM)
        hp, hs = _out_proj(m_p, m_s, hp, hs, p_prompt[i].reshape(n_batch * seq, PLE_DIM),
                           p_sample[i].reshape(dec_batch * dec_seq, PLE_DIM), wts, tm=OUT_TM)
        sp_list.append(s_p)
        ss_list.append(s_s)
        cv_list.append(v_s.reshape(dec_batch, dec_seq, E_A))
    stack = (lambda xs: xs[0][None]) if depth == 1 else jnp.stack
    return (hp.reshape(n_batch, seq, D_MODEL),
            hs.reshape(dec_batch, dec_seq, D_MODEL),
            stack(sp_list), stack(ss_list), stack(cv_list))
```

```python
import functools

import jax
import jax.numpy as jnp
from jax import lax
from jax.experimental import pallas as pl
from jax.experimental.pallas import tpu as pltpu

D_MODEL = 2048
E_A = D_MODEL // 2
HA = 4
DA = E_A // HA
CHUNK = 128
HB = 4
E_B = D_MODEL // 2
DV = E_B // HB
DK = DV // 2
K_B = HB * DK
GATE_RANK = 16
GATE_TAU = 16.0
GLA_SUB = 16
PLE_DIM = 256
EPS = 1e-6

LANES = 128
OFF_U = 0
OFF_V = OFF_U + E_A
OFF_Z = OFF_V + E_A
OFF_Q = OFF_Z + E_A
OFF_K = OFF_Q + K_B
OFF_VB = OFF_K + K_B
OFF_ZB = OFF_VB + E_B
OFF_MA = OFF_ZB + E_B
OFF_MB = OFF_MA + D_MODEL
N_MAIN = OFF_MB + D_MODEL
W_IN_A = OFF_MA
A_PAD = LANES
INPROJ_TN = 1024
INPROJ_TN_BF16 = 2048

F32 = jnp.float32
BF16 = jnp.bfloat16

VMEM_LIMIT = 56 * 1024 * 1024


def _dot(a, b):
    return jnp.dot(a, b, preferred_element_type=F32)


def _dot_nt(a, b):
    return lax.dot_general(a, b, (((1,), (1,)), ((), ())), preferred_element_type=F32)


def _sigmoid(x):
    return jax.nn.sigmoid(x)


def _rms(x, g):
    return x * lax.rsqrt(jnp.mean(x * x, axis=-1, keepdims=True) + EPS) * g


def _split_bf16(x):
    hi = x.astype(BF16)
    lo = (x - hi.astype(F32)).astype(BF16)
    return hi, lo


def _log_decay(a_tile, w_a2, b_a):
    lane = lax.broadcasted_iota(jnp.int32, a_tile.shape, 1)
    a_lr = jnp.where(lane < GATE_RANK, a_tile, jnp.zeros_like(a_tile))
    pre = _dot(a_lr, w_a2) + b_a
    return (jnp.minimum(pre, 0.0) - jnp.log(1.0 + jnp.exp(-jnp.abs(pre)))) * (1.0 / GATE_TAU)


def _bcast_rows(b, idxs, seg):
    return jnp.concatenate(
        [jnp.broadcast_to(b[i:i + 1, :], (seg, b.shape[1])) for i in idxs], axis=0)


BF16_SUBLANES = 16


def _inproj_kernel(*refs, n_cast, emit_w):
    x_ref, g_ref, w_ref, w_a_ref = refs[:4]
    cast_in = refs[4:4 + n_cast]
    o_ref, o_a_ref = refs[4 + n_cast:6 + n_cast]
    cast_out = refs[6 + n_cast:6 + 2 * n_cast]
    rest = refs[6 + 2 * n_cast:]
    xn_ref = rest[-1]

    @pl.when(pl.program_id(1) == 0)
    def _():
        xn = _rms(x_ref[...], g_ref[...]).astype(BF16)
        xn_ref[...] = xn
        w_a = w_a_ref[...]
        if emit_w:
            w_a = w_a.astype(BF16)
            rest[1][...] = w_a
        o_a_ref[...] = _dot_nt(xn, w_a).astype(o_a_ref.dtype)

    for src, dst in zip(cast_in, cast_out):
        dst[...] = src[...].astype(dst.dtype)
    w = w_ref[...]
    if emit_w:
        w = w.astype(BF16)
        rest[0][...] = w
    o_ref[...] = _dot_nt(xn_ref[...], w).astype(o_ref.dtype)


def _inproj(x2d, g_pre, w_in_t, tm, w_bf=None, cast=()):
    t = x2d.shape[0]
    emit_w = w_bf is None
    tn = INPROJ_TN if emit_w else INPROJ_TN_BF16
    n_i, n_j = t // tm, N_MAIN // tn

    def w_rows(i, j):
        row0 = j * tn + jnp.where(j * tn >= OFF_MA, GATE_RANK, 0)
        return (pl.multiple_of(row0, GATE_RANK), 0)

    if emit_w:
        w_args = (w_in_t, w_in_t)
        w_specs = [pl.BlockSpec((pl.Element(tn), pl.Element(D_MODEL)), w_rows),
                   pl.BlockSpec((A_PAD, D_MODEL), lambda i, j: (W_IN_A // A_PAD, 0))]
        w_out_specs = [pl.BlockSpec((tn, D_MODEL), lambda i, j: (j, 0)),
                       pl.BlockSpec((A_PAD, D_MODEL), lambda i, j: (0, 0))]
        w_out_shapes = [jax.ShapeDtypeStruct((N_MAIN, D_MODEL), BF16),
                        jax.ShapeDtypeStruct((A_PAD, D_MODEL), BF16)]
        assert n_i == 1, "the bf16 weight copy is written once, by a single token tile"
    else:
        w_args = w_bf
        w_specs = [pl.BlockSpec((tn, D_MODEL), lambda i, j: (j, 0)),
                   pl.BlockSpec((A_PAD, D_MODEL), lambda i, j: (0, 0))]
        w_out_specs, w_out_shapes = [], []

    cast_specs = []
    for w in cast:
        per_i = w.shape[0] // n_i
        assert per_i * n_i == w.shape[0] and per_i % BF16_SUBLANES == 0
        n_sub = max(n for n in range(1, n_j + 1)
                    if per_i % n == 0 and (per_i // n) % BF16_SUBLANES == 0)
        cast_specs.append(pl.BlockSpec(
            (per_i // n_sub, w.shape[1]),
            lambda i, j, n_sub=n_sub: (i * n_sub + jnp.minimum(j, n_sub - 1), 0)))

    kern = functools.partial(_inproj_kernel, n_cast=len(cast), emit_w=emit_w)
    return pl.pallas_call(
        kern,
        grid=(n_i, n_j),
        in_specs=[
            pl.BlockSpec((tm, D_MODEL), lambda i, j: (i, 0)),
            pl.BlockSpec((1, D_MODEL), lambda i, j: (0, 0)),
        ] + w_specs + cast_specs,
        out_specs=[
            pl.BlockSpec((tm, tn), lambda i, j: (i, j)),
            pl.BlockSpec((tm, A_PAD), lambda i, j: (i, 0)),
        ] + cast_specs + w_out_specs,
        out_shape=[
            jax.ShapeDtypeStruct((t, N_MAIN), BF16),
            jax.ShapeDtypeStruct((t, A_PAD), BF16),
        ] + [jax.ShapeDtypeStruct(w.shape, BF16) for w in cast] + w_out_shapes,
        scratch_shapes=[pltpu.VMEM((tm, D_MODEL), BF16)],
        compiler_params=pltpu.CompilerParams(
            dimension_semantics=("arbitrary", "arbitrary"),
            vmem_limit_bytes=VMEM_LIMIT),
        name="inproj_f32w" if emit_w else "inproj",
    )(x2d, g_pre, *w_args, *cast)


def _gla_levels(c):
    levels = []
    blk = c
    while blk > GLA_SUB:
        per = blk // GLA_SUB
        levels.append((blk, [(j // per) * per + per // 2 - 1 for j in range(c // GLA_SUB)]))
        blk //= 2
    return levels


def _gla_masks(c):
    t = lax.broadcasted_iota(jnp.int32, (c, c), 0)
    s = lax.broadcasted_iota(jnp.int32, (c, c), 1)
    levels = _gla_levels(c)
    masks = []
    for blk, _ in levels:
        sh = blk.bit_length() - 1
        half = blk // 2
        same = (t >> sh) == (s >> sh)
        masks.append(same & ((t & (blk - 1)) >= half) & ((s & (blk - 1)) < half))
    sh = GLA_SUB.bit_length() - 1
    masks.append(((t >> sh) == (s >> sh)) & (s <= t))
    return masks


def _cum_log_decay(log_a, tril_ones):
    hi, lo = _split_bf16(log_a)
    return _dot(tril_ones, hi) + _dot(tril_ones, lo)


def _gla_prepare(q, k, b):
    c = q.shape[0]
    n = c // GLA_SUB

    def rep(x):
        return _bcast_rows(x, list(range(n)), GLA_SUB)

    ends = jnp.concatenate([b[(j + 1) * GLA_SUB - 1:(j + 1) * GLA_SUB, :] for j in range(n)], axis=0)
    starts = jnp.concatenate([jnp.zeros((1, b.shape[1]), F32), ends[:n - 1]], axis=0)
    starts_r = rep(starts)
    q_e = q * jnp.exp(b - starts_r)
    k_e = k * jnp.exp(rep(ends) - b)

    qs, ks = [], []
    for _, ref_blocks in _gla_levels(c):
        ref = jnp.concatenate([ends[r:r + 1] for r in ref_blocks], axis=0)
        qs.append((q_e * rep(jnp.exp(jnp.minimum(starts - ref, 0.0)))).astype(BF16))
        ks.append((k_e * rep(jnp.exp(jnp.minimum(ref - ends, 0.0)))).astype(BF16))
    qs.append(q_e.astype(BF16))
    ks.append((k * jnp.exp(starts_r - b)).astype(BF16))

    q_in = (q_e * rep(jnp.exp(starts))).astype(BF16)
    k_out = k_e * rep(jnp.exp(ends[n - 1:n] - ends))
    return qs, ks, q_in, k_out, b


def _gla_attend(prepared, vb, s_heads, masks):
    qs, ks, q_in, k_out, b = prepared
    c = b.shape[0]
    outs, new_states = [], []
    for h in range(HB):
        kc = slice(h * DK, (h + 1) * DK)
        vc = slice(h * DV, (h + 1) * DV)
        att = jnp.zeros((c, c), F32)
        for ql, kl, m in zip(qs, ks, masks):
            att = jnp.where(m, _dot_nt(ql[:, kc], kl[:, kc]), att)
        v_h = vb[:, vc]
        s_h = s_heads[h]
        o_h = _dot(att.astype(BF16), v_h) + _dot(q_in[:, kc], s_h.astype(BF16))
        b_t = jnp.transpose(b[:, kc])
        decay = jnp.broadcast_to(jnp.exp(b_t[:, c - 1:c]), (DK, DV))
        k_t = jnp.transpose(k_out[:, kc]).astype(BF16)
        new_states.append(decay * s_h + _dot(k_t, v_h))
        outs.append(o_h)
    return jnp.concatenate(outs, axis=1), new_states


def _gla_decode_rows(q, k, vb, log_a, s0_ref, s1_ref, nb, seq):
    rows = nb * seq
    t = lax.broadcasted_iota(jnp.int32, (rows, rows), 0)
    s = lax.broadcasted_iota(jnp.int32, (rows, rows), 1)
    sh = seq.bit_length() - 1
    causal = ((t >> sh) == (s >> sh)) & (s <= t)
    tril_ones = jnp.where(causal, 1.0, 0.0).astype(BF16)
    hi, lo = _split_bf16(log_a)
    b = _dot(tril_ones, hi) + _dot(tril_ones, lo)

    q_t = q * jnp.exp(b)
    k_t = (k * jnp.exp(-b)).astype(BF16)
    b_last = _bcast_rows(b, [n * seq + seq - 1 for n in range(nb)], seq)
    k_out = k * jnp.exp(b_last - b)
    col = lax.broadcasted_iota(jnp.int32, (DK, rows), 1) >> sh

    outs = []
    for h in range(HB):
        kc = slice(h * DK, (h + 1) * DK)
        vc = slice(h * DV, (h + 1) * DV)
        v_h = vb[:, vc]
        att = jnp.where(causal, _dot_nt(q_t[:, kc].astype(BF16), k_t[:, kc]), 0.0)
        o_intra = _dot(att.astype(BF16), v_h)
        decay_tr = jnp.exp(jnp.transpose(b[:, kc]))
        k_tr = jnp.transpose(k_out[:, kc])
        o_inter = []
        for n in range(nb):
            s_n = s0_ref[n, h]
            r0 = n * seq
            o_inter.append(_dot(q_t[r0:r0 + seq, kc], s_n))
            decay = jnp.broadcast_to(decay_tr[:, r0 + seq - 1:r0 + seq], (DK, DV))
            k_n = jnp.where(col == n, k_tr, 0.0).astype(BF16)
            s1_ref[n, h] = decay * s_n + _dot(k_n, v_h)
        outs.append(o_intra + jnp.concatenate(o_inter, axis=0))
    return jnp.concatenate(outs, axis=1)


def _spatial_masked_weights(w_s_ref, b_col_ref, seq):
    t = lax.broadcasted_iota(jnp.int32, (CHUNK, CHUNK), 0)
    s = lax.broadcasted_iota(jnp.int32, (CHUNK, CHUNK), 1)
    causal = s <= t
    if seq >= CHUNK:
        return [jnp.where(causal, w_s_ref[g], 0.0).astype(BF16) for g in range(HA)], b_col_ref[...]
    sh = seq.bit_length() - 1
    causal = causal & ((t >> sh) == (s >> sh))
    pick_rows = jnp.where((s < seq) & ((t & (seq - 1)) == s), 1.0, 0.0)
    pick_cols = jnp.where((t < seq) & ((s & (seq - 1)) == t), 1.0, 0.0)
    w_sp = [jnp.where(causal, _dot(_dot(pick_rows, w_s_ref[g]), pick_cols), 0.0).astype(BF16)
            for g in range(HA)]
    pos = lax.broadcasted_iota(jnp.int32, (CHUNK, HA), 0) & (seq - 1)
    bias = jnp.zeros((CHUNK, HA), F32)
    for i in range(seq):
        bias = jnp.where(pos == i, b_col_ref[i:i + 1, :], bias)
    return w_sp, bias


def _branch_a_chunk(proj_ref, r, g_v_ref, w_sp, b_col):
    vn = _rms(proj_ref[r, OFF_V:OFF_V + E_A].astype(F32), g_v_ref[...])
    vn_b = vn.astype(BF16)
    gate = jnp.concatenate(
        [_dot(w_sp[g], vn_b[:, g * DA:(g + 1) * DA]) + b_col[:, g:g + 1]
         for g in range(HA)], axis=1)
    z = proj_ref[r, OFF_Z:OFF_Z + E_A].astype(F32)
    return proj_ref[r, OFF_U:OFF_U + E_A].astype(F32) * gate * (z * _sigmoid(z)), vn


def _branch_b_out(o, zb, g_o_ref):
    g_o = g_o_ref[...]
    on = jnp.concatenate(
        [_rms(o[:, h * DV:(h + 1) * DV], g_o[:, h * DV:(h + 1) * DV]) for h in range(HB)], axis=1)
    zb = zb.astype(F32)
    return on * (zb * _sigmoid(zb))


MXU_COLS = 256


def _col_blocks(rows):
    width = MXU_COLS if rows >= 4 * CHUNK else 2 * MXU_COLS
    return [slice(c, c + width) for c in range(0, D_MODEL, width)]


def _merge(y_a, y_b, proj_ref, w_pa_ref, w_pb_ref, m_ref):
    for c in _col_blocks(y_a.shape[0]):
        m_a = proj_ref[:, OFF_MA + c.start:OFF_MA + c.stop].astype(F32)
        m_b = proj_ref[:, OFF_MB + c.start:OFF_MB + c.stop].astype(F32)
        m_ref[:, c] = (_sigmoid(m_a) * _dot(y_a, w_pa_ref[:, c])
                       + _sigmoid(m_b) * _dot(y_b, w_pb_ref[:, c])).astype(m_ref.dtype)


def _chunks(tm):
    return [slice(c * CHUNK, (c + 1) * CHUNK) for c in range(tm // CHUNK)]


def _mix_prompt_kernel(proj_ref, proj_a_ref, g_v_ref, w_s_ref, b_col_ref, w_a2_ref, b_a_ref,
                       g_o_ref, w_pa_ref, w_pb_ref, m_ref, s_out_ref, s_ref, part_ref, *, tm, seq):
    keep = pl.program_id(1) != 0
    states = [jnp.where(keep, s_ref[h], 0.0) for h in range(HB)]
    w_sp, b_col = _spatial_masked_weights(w_s_ref, b_col_ref, seq)
    t = lax.broadcasted_iota(jnp.int32, (CHUNK, CHUNK), 0)
    s = lax.broadcasted_iota(jnp.int32, (CHUNK, CHUNK), 1)
    tril_ones = jnp.where(s <= t, 1.0, 0.0).astype(BF16)
    masks = _gla_masks(CHUNK)
    log_a = _log_decay(proj_a_ref[...], w_a2_ref[...], b_a_ref[...])
    chunks, blocks = _chunks(tm), _col_blocks(tm)
    per_chunk = len(blocks) // len(chunks)
    assert per_chunk * len(chunks) == len(blocks)
    y_a = jnp.concatenate(
        [_branch_a_chunk(proj_ref, r, g_v_ref, w_sp, b_col)[0].astype(BF16) for r in chunks], axis=0)
    cum = [_cum_log_decay(log_a[r], tril_ones) for r in chunks]

    def prepare(i):
        q = proj_ref[chunks[i], OFF_Q:OFF_Q + K_B].astype(F32) * (DK ** -0.5)
        k = proj_ref[chunks[i], OFF_K:OFF_K + K_B].astype(F32)
        return _gla_prepare(q, k, cum[i])

    yb_rows = []
    prepared = prepare(0)
    for i, r in enumerate(chunks):
        o_c, states = _gla_attend(prepared, proj_ref[r, OFF_VB:OFF_VB + E_B], states, masks)
        for c in blocks[i * per_chunk:(i + 1) * per_chunk]:
            m_a = proj_ref[:, OFF_MA + c.start:OFF_MA + c.stop].astype(F32)
            part_ref[:, c] = _sigmoid(m_a) * _dot(y_a, w_pa_ref[:, c])
        yb_rows.append(_branch_b_out(o_c, proj_ref[r, OFF_ZB:OFF_ZB + E_B], g_o_ref).astype(BF16))
        if i + 1 < len(chunks):
            prepared = prepare(i + 1)
    for h in range(HB):
        s_ref[h] = states[h]
        s_out_ref[0, h] = states[h]
    y_b = jnp.concatenate(yb_rows, axis=0)
    for c in blocks:
        m_b = proj_ref[:, OFF_MB + c.start:OFF_MB + c.stop].astype(F32)
        m_ref[:, c] = (part_ref[:, c]
                       + _sigmoid(m_b) * _dot(y_b, w_pb_ref[:, c])).astype(m_ref.dtype)


def _mix_decode_kernel(proj_ref, proj_a_ref, s0_ref, g_v_ref, w_s_ref, b_col_ref, w_a2_ref, b_a_ref,
                       g_o_ref, w_pa_ref, w_pb_ref, m_ref, vn_ref, s1_ref, *, nb, seq):
    rows = slice(0, nb * seq)
    log_a = _log_decay(proj_a_ref[...], w_a2_ref[...], b_a_ref[...])
    q = proj_ref[:, OFF_Q:OFF_Q + K_B].astype(F32) * (DK ** -0.5)
    k = proj_ref[:, OFF_K:OFF_K + K_B].astype(F32)
    o = _gla_decode_rows(q, k, proj_ref[:, OFF_VB:OFF_VB + E_B], log_a, s0_ref, s1_ref, nb, seq)
    y_a, vn = _branch_a_chunk(proj_ref, rows, g_v_ref,
                              *_spatial_masked_weights(w_s_ref, b_col_ref, seq))
    vn_ref[...] = vn
    y_b = _branch_b_out(o, proj_ref[:, OFF_ZB:OFF_ZB + E_B], g_o_ref)
    _merge(y_a.astype(BF16), y_b.astype(BF16), proj_ref, w_pa_ref, w_pb_ref, m_ref)


def _out_kernel(m_p_ref, m_s_ref, x_p_ref, x_s_ref, p_p_ref, p_s_ref,
                w_o_ref, g_post_ref, w_pg_ref, w_pe_ref, y_p_ref, y_s_ref, *, n_prompt_tiles):
    is_s = pl.program_id(0) >= n_prompt_tiles
    m = jnp.where(is_s, m_s_ref[...], m_p_ref[...])
    x = jnp.where(is_s, x_s_ref[...], x_p_ref[...])
    p_b = jnp.where(is_s, p_s_ref[...], p_p_ref[...]).astype(BF16)
    blocks = _col_blocks(m.shape[0])
    mo = jnp.concatenate([_dot(m, w_o_ref[:, c]) for c in blocks], axis=1)
    h_res = x + _rms(mo, g_post_ref[...])
    h_b = h_res.astype(BF16)
    for k in range(0, len(blocks), 2):
        cols = slice(blocks[k].start, blocks[k + 1].stop)
        y = jnp.concatenate(
            [h_res[:, c] + _sigmoid(_dot(h_b, w_pg_ref[:, c])) * _dot(p_b, w_pe_ref[:, c])
             for c in blocks[k:k + 2]], axis=1)
        to_s = jnp.broadcast_to(is_s, y.shape)
        pltpu.store(y_p_ref.at[:, cols], y, mask=jnp.logical_not(to_s))
        pltpu.store(y_s_ref.at[:, cols], y, mask=to_s)


def _const_spec(shape):
    nd = len(shape)
    return pl.BlockSpec(shape, lambda *_: (0,) * nd, pipeline_mode=pl.Buffered(1))


def _mix_prompt(proj, proj_a, wts, n_batch, seq, tm):
    nt = seq // tm
    kern = functools.partial(_mix_prompt_kernel, tm=tm, seq=seq)
    row = lambda b, i: (b * nt + i, 0)
    names = ("g_v", "w_s", "b_col", "w_a2", "b_a", "g_o", "w_pa", "w_pb")
    w_args = [wts[n] for n in names]
    return pl.pallas_call(
        kern,
        grid=(n_batch, nt),
        in_specs=[
            pl.BlockSpec((tm, N_MAIN), row),
            pl.BlockSpec((tm, A_PAD), row),
        ] + [_const_spec(w.shape) for w in w_args],
        out_specs=[
            pl.BlockSpec((tm, D_MODEL), row),
            pl.BlockSpec((1, HB, DK, DV), lambda b, i: (b, 0, 0, 0)),
        ],
        out_shape=[
            jax.ShapeDtypeStruct((n_batch * seq, D_MODEL), BF16),
            jax.ShapeDtypeStruct((n_batch, HB, DK, DV), F32),
        ],
        scratch_shapes=[pltpu.VMEM((HB, DK, DV), F32), pltpu.VMEM((tm, D_MODEL), F32)],
        compiler_params=pltpu.CompilerParams(
            dimension_semantics=("arbitrary", "arbitrary"),
            vmem_limit_bytes=VMEM_LIMIT),
        name="mix_prompt",
    )(proj, proj_a, *w_args)


def _mix_decode(proj, proj_a, s0, wts, nb, seq):
    n_batch = s0.shape[0]
    rows = nb * seq
    assert rows == CHUNK
    kern = functools.partial(_mix_decode_kernel, nb=nb, seq=seq)
    row = lambda i: (i, 0)
    state = lambda i: (i, 0, 0, 0)
    names = ("g_v", "w_s", "b_col", "w_a2", "b_a", "g_o", "w_pa", "w_pb")
    w_args = [wts[n] for n in names]
    return pl.pallas_call(
        kern,
        grid=(n_batch // nb,),
        in_specs=[
            pl.BlockSpec((rows, N_MAIN), row),
            pl.BlockSpec((rows, A_PAD), row),
            pl.BlockSpec((nb, HB, DK, DV), state),
        ] + [_const_spec(w.shape) for w in w_args],
        out_specs=[
            pl.BlockSpec((rows, D_MODEL), row),
            pl.BlockSpec((rows, E_A), row),
            pl.BlockSpec((nb, HB, DK, DV), state),
        ],
        out_shape=[
            jax.ShapeDtypeStruct((n_batch * seq, D_MODEL), BF16),
            jax.ShapeDtypeStruct((n_batch * seq, E_A), F32),
            jax.ShapeDtypeStruct(s0.shape, F32),
        ],
        compiler_params=pltpu.CompilerParams(
            dimension_semantics=("arbitrary",),
            vmem_limit_bytes=VMEM_LIMIT),
        name="mix_decode",
    )(proj, proj_a, s0, *w_args)


def _out_proj(m_p, m_s, x_p, x_s, p_p, p_s, wts, tm):
    n_p, n_s = x_p.shape[0] // tm, x_s.shape[0] // tm
    kern = functools.partial(_out_kernel, n_prompt_tiles=n_p)
    prompt = lambda i: (jnp.minimum(i, n_p - 1), 0)
    sample = lambda i: (jnp.maximum(i - n_p, 0), 0)
    prompt_spec = lambda width: pl.BlockSpec((tm, width), prompt)
    sample_spec = lambda width: pl.BlockSpec((tm, width), sample, pipeline_mode=pl.Buffered(1))
    names = ("w_o", "g_post", "w_pg", "w_pe")
    w_args = [wts[n] for n in names]
    return pl.pallas_call(
        kern,
        grid=(n_p + n_s,),
        in_specs=[
            prompt_spec(D_MODEL), sample_spec(D_MODEL),
            prompt_spec(D_MODEL), sample_spec(D_MODEL),
            prompt_spec(PLE_DIM), sample_spec(PLE_DIM),
        ] + [_const_spec(w.shape) for w in w_args],
        out_specs=[prompt_spec(D_MODEL), sample_spec(D_MODEL)],
        out_shape=[
            jax.ShapeDtypeStruct(x_p.shape, F32),
            jax.ShapeDtypeStruct(x_s.shape, F32),
        ],
        compiler_params=pltpu.CompilerParams(
            dimension_semantics=("arbitrary",),
            vmem_limit_bytes=VMEM_LIMIT),
        name="out_proj",
    )(m_p, m_s, x_p, x_s, p_p, p_s, *w_args)


def _layer_weights(i, g_pre, w_in, g_v, w_s, b_s, w_a2, b_a, g_o, g_post):
    w_a2_p = jnp.concatenate(
        [w_a2[i], jnp.zeros((A_PAD - GATE_RANK, K_B), w_a2.dtype)], axis=0).astype(BF16)
    return dict(
        g_pre=g_pre[i][None, :],
        w_in_t=w_in[i].T,
        g_v=g_v[i][None, :],
        w_s=w_s[i],
        b_col=b_s[i].T,
        w_a2=w_a2_p,
        b_a=b_a[i][None, :],
        g_o=g_o[i].reshape(1, E_B),
        g_post=g_post[i][None, :],
    )


SIDE_WEIGHTS = ("w_pa", "w_pb", "w_o", "w_pg", "w_pe")

INPROJ_TM = 1024
MIX_TM = 512
OUT_TM = 512


def kernel(x_prompt, x_sample, p_prompt, p_sample, state_gla, g_pre, w_in, g_v, w_s, b_s,
           w_a2, b_a, g_o, w_pa, w_pb, w_o, g_post, w_pg, w_pe):
    depth = w_in.shape[0]
    n_batch, seq, _ = x_prompt.shape
    dec_batch, dec_seq, _ = x_sample.shape
    hp = x_prompt.reshape(n_batch * seq, D_MODEL)
    hs = x_sample.reshape(dec_batch * dec_seq, D_MODEL)
    sp_list, ss_list, cv_list = [], [], []
    for i in range(depth):
        wts = _layer_weights(i, g_pre, w_in, g_v, w_s, b_s, w_a2, b_a, g_o, g_post)
        side_f32 = dict(w_pa=w_pa[i], w_pb=w_pb[i], w_o=w_o[i], w_pg=w_pg[i], w_pe=w_pe[i])
        proj_s, proj_a_s, *w_bf = _inproj(hs, wts["g_pre"], wts["w_in_t"], tm=INPROJ_TM)
        proj_p, proj_a_p, *side = _inproj(hp, wts["g_pre"], wts["w_in_t"], tm=INPROJ_TM, w_bf=w_bf,
                                          cast=[side_f32[n] for n in SIDE_WEIGHTS])
        wts.update(zip(SIDE_WEIGHTS, side))
        m_s, v_s, s_s = _mix_decode(proj_s, proj_a_s, state_gla[i], wts,
                                    nb=CHUNK // dec_seq, seq=dec_seq)
        m_p, s_p = _mix_prompt(proj_p, proj_a_p, wts, n_batch, seq, tm=MIX_TM)
        hp, hs = _out_proj(m_p, m_s, hp, hs, p_prompt[i].reshape(n_batch * seq, PLE_DIM),
                           p_sample[i].reshape(dec_batch * dec_seq, PLE_DIM), wts, tm=OUT_TM)
        sp_list.append(s_p)
        ss_list.append(s_s)
        cv_list.append(v_s.reshape(dec_batch, dec_seq, E_A))
    stack = (lambda xs: xs[0][None]) if depth == 1 else jnp.stack
    return (hp.reshape(n_batch, seq, D_MODEL),
            hs.reshape(dec_batch, dec_seq, D_MODEL),
            stack(sp_list), stack(ss_list), stack(cv_list))
```

```python
import functools

import jax
import jax.numpy as jnp
from jax import lax
from jax.experimental import pallas as pl
from jax.experimental.pallas import tpu as pltpu

D_MODEL = 2048
E_A = D_MODEL // 2
HA = 4
DA = E_A // HA
CHUNK = 128
HB = 4
E_B = D_MODEL // 2
DV = E_B // HB
DK = DV // 2
K_B = HB * DK
GATE_RANK = 16
GATE_TAU = 16.0
GLA_SUB = 16
PLE_DIM = 256
EPS = 1e-6

LANES = 128
OFF_U = 0
OFF_V = OFF_U + E_A
OFF_Z = OFF_V + E_A
OFF_Q = OFF_Z + E_A
OFF_K = OFF_Q + K_B
OFF_VB = OFF_K + K_B
OFF_ZB = OFF_VB + E_B
OFF_MA = OFF_ZB + E_B
OFF_MB = OFF_MA + D_MODEL
N_MAIN = OFF_MB + D_MODEL
W_IN_A = OFF_MA
A_PAD = LANES
INPROJ_TN = 1024
INPROJ_TN_BF16 = 2048

F32 = jnp.float32
BF16 = jnp.bfloat16

VMEM_LIMIT = 56 * 1024 * 1024


def _dot(a, b):
    return jnp.dot(a, b, preferred_element_type=F32)


def _dot_nt(a, b):
    return lax.dot_general(a, b, (((1,), (1,)), ((), ())), preferred_element_type=F32)


def _sigmoid(x):
    return jax.nn.sigmoid(x)


def _rms(x, g):
    return x * lax.rsqrt(jnp.mean(x * x, axis=-1, keepdims=True) + EPS) * g


def _split_bf16(x):
    hi = x.astype(BF16)
    lo = (x - hi.astype(F32)).astype(BF16)
    return hi, lo


def _log_decay(a_tile, w_a2, b_a):
    lane = lax.broadcasted_iota(jnp.int32, a_tile.shape, 1)
    a_lr = jnp.where(lane < GATE_RANK, a_tile, jnp.zeros_like(a_tile))
    pre = _dot(a_lr, w_a2) + b_a
    return (jnp.minimum(pre, 0.0) - jnp.log(1.0 + jnp.exp(-jnp.abs(pre)))) * (1.0 / GATE_TAU)


def _bcast_rows(b, idxs, seg):
    return jnp.concatenate(
        [jnp.broadcast_to(b[i:i + 1, :], (seg, b.shape[1])) for i in idxs], axis=0)


BF16_SUBLANES = 16


def _inproj_kernel(*refs, n_cast, emit_w):
    x_ref, g_ref, w_ref, w_a_ref = refs[:4]
    cast_in = refs[4:4 + n_cast]
    o_ref, o_a_ref = refs[4 + n_cast:6 + n_cast]
    cast_out = refs[6 + n_cast:6 + 2 * n_cast]
    rest = refs[6 + 2 * n_cast:]
    xn_ref = rest[-1]

    @pl.when(pl.program_id(1) == 0)
    def _():
        xn = _rms(x_ref[...], g_ref[...]).astype(BF16)
        xn_ref[...] = xn
        w_a = w_a_ref[...]
        if emit_w:
            w_a = w_a.astype(BF16)
            rest[1][...] = w_a
        o_a_ref[...] = _dot_nt(xn, w_a).astype(o_a_ref.dtype)

    for src, dst in zip(cast_in, cast_out):
        dst[...] = src[...].astype(dst.dtype)
    w = w_ref[...]
    if emit_w:
        w = w.astype(BF16)
        rest[0][...] = w
    o_ref[...] = _dot_nt(xn_ref[...], w).astype(o_ref.dtype)


def _inproj(x2d, g_pre, w_in_t, tm, w_bf=None, cast=()):
    t = x2d.shape[0]
    emit_w = w_bf is None
    tn = INPROJ_TN if emit_w else INPROJ_TN_BF16
    n_i, n_j = t // tm, N_MAIN // tn

    def w_rows(i, j):
        row0 = j * tn + jnp.where(j * tn >= OFF_MA, GATE_RANK, 0)
        return (pl.multiple_of(row0, GATE_RANK), 0)

    if emit_w:
        w_args = (w_in_t, w_in_t)
        w_specs = [pl.BlockSpec((pl.Element(tn), pl.Element(D_MODEL)), w_rows),
                   pl.BlockSpec((A_PAD, D_MODEL), lambda i, j: (W_IN_A // A_PAD, 0))]
        w_out_specs = [pl.BlockSpec((tn, D_MODEL), lambda i, j: (j, 0)),
                       pl.BlockSpec((A_PAD, D_MODEL), lambda i, j: (0, 0))]
        w_out_shapes = [jax.ShapeDtypeStruct((N_MAIN, D_MODEL), BF16),
                        jax.ShapeDtypeStruct((A_PAD, D_MODEL), BF16)]
        assert n_i == 1, "the bf16 weight copy is written once, by a single token tile"
    else:
        w_args = w_bf
        w_specs = [pl.BlockSpec((tn, D_MODEL), lambda i, j: (j, 0)),
                   pl.BlockSpec((A_PAD, D_MODEL), lambda i, j: (0, 0))]
        w_out_specs, w_out_shapes = [], []

    cast_specs = []
    for w in cast:
        per_i = w.shape[0] // n_i
        assert per_i * n_i == w.shape[0] and per_i % BF16_SUBLANES == 0
        n_sub = max(n for n in range(1, n_j + 1)
                    if per_i % n == 0 and (per_i // n) % BF16_SUBLANES == 0)
        cast_specs.append(pl.BlockSpec(
            (per_i // n_sub, w.shape[1]),
            lambda i, j, n_sub=n_sub: (i * n_sub + jnp.minimum(j, n_sub - 1), 0)))

    kern = functools.partial(_inproj_kernel, n_cast=len(cast), emit_w=emit_w)
    return pl.pallas_call(
        kern,
        grid=(n_i, n_j),
        in_specs=[
            pl.BlockSpec((tm, D_MODEL), lambda i, j: (i, 0)),
            pl.BlockSpec((1, D_MODEL), lambda i, j: (0, 0)),
        ] + w_specs + cast_specs,
        out_specs=[
            pl.BlockSpec((tm, tn), lambda i, j: (i, j)),
            pl.BlockSpec((tm, A_PAD), lambda i, j: (i, 0)),
        ] + cast_specs + w_out_specs,
        out_shape=[
            jax.ShapeDtypeStruct((t, N_MAIN), BF16),
            jax.ShapeDtypeStruct((t, A_PAD), BF16),
        ] + [jax.ShapeDtypeStruct(w.shape, BF16) for w in cast] + w_out_shapes,
        scratch_shapes=[pltpu.VMEM((tm, D_MODEL), BF16)],
        compiler_params=pltpu.CompilerParams(
            dimension_semantics=("arbitrary", "arbitrary"),
            vmem_limit_bytes=VMEM_LIMIT),
        name="inproj_f32w" if emit_w else "inproj",
    )(x2d, g_pre, *w_args, *cast)


def _gla_levels(c):
    levels = []
    blk = c
    while blk > GLA_SUB:
        per = blk // GLA_SUB
        levels.append((blk, [(j // per) * per + per // 2 - 1 for j in range(c // GLA_SUB)]))
        blk //= 2
    return levels


def _gla_masks(c):
    t = lax.broadcasted_iota(jnp.int32, (c, c), 0)
    s = lax.broadcasted_iota(jnp.int32, (c, c), 1)
    levels = _gla_levels(c)
    masks = []
    for blk, _ in levels:
        sh = blk.bit_length() - 1
        half = blk // 2
        same = (t >> sh) == (s >> sh)
        masks.append(same & ((t & (blk - 1)) >= half) & ((s & (blk - 1)) < half))
    sh = GLA_SUB.bit_length() - 1
    masks.append(((t >> sh) == (s >> sh)) & (s <= t))
    return masks


def _cum_log_decay(log_a, tril_ones):
    hi, lo = _split_bf16(log_a)
    return _dot(tril_ones, hi) + _dot(tril_ones, lo)


def _gla_prepare(q, k, b):
    c = q.shape[0]
    n = c // GLA_SUB

    def rep(x):
        return _bcast_rows(x, list(range(n)), GLA_SUB)

    ends = jnp.concatenate([b[(j + 1) * GLA_SUB - 1:(j + 1) * GLA_SUB, :] for j in range(n)], axis=0)
    starts = jnp.concatenate([jnp.zeros((1, b.shape[1]), F32), ends[:n - 1]], axis=0)
    starts_r = rep(starts)
    q_e = q * jnp.exp(b - starts_r)
    k_e = k * jnp.exp(rep(ends) - b)

    qs, ks = [], []
    for _, ref_blocks in _gla_levels(c):
        ref = jnp.concatenate([ends[r:r + 1] for r in ref_blocks], axis=0)
        qs.append((q_e * rep(jnp.exp(jnp.minimum(starts - ref, 0.0)))).astype(BF16))
        ks.append((k_e * rep(jnp.exp(jnp.minimum(ref - ends, 0.0)))).astype(BF16))
    qs.append(q_e.astype(BF16))
    ks.append((k * jnp.exp(starts_r - b)).astype(BF16))

    q_in = (q_e * rep(jnp.exp(starts))).astype(BF16)
    k_out = k_e * rep(jnp.exp(ends[n - 1:n] - ends))
    return qs, ks, q_in, k_out, b


def _gla_attend(prepared, vb, s_heads, masks):
    qs, ks, q_in, k_out, b = prepared
    c = b.shape[0]
    outs, new_states = [], []
    for h in range(HB):
        kc = slice(h * DK, (h + 1) * DK)
        vc = slice(h * DV, (h + 1) * DV)
        att = jnp.zeros((c, c), F32)
        for ql, kl, m in zip(qs, ks, masks):
            att = jnp.where(m, _dot_nt(ql[:, kc], kl[:, kc]), att)
        v_h = vb[:, vc]
        s_h = s_heads[h]
        o_h = _dot(att.astype(BF16), v_h) + _dot(q_in[:, kc], s_h.astype(BF16))
        b_t = jnp.transpose(b[:, kc])
        decay = jnp.broadcast_to(jnp.exp(b_t[:, c - 1:c]), (DK, DV))
        k_t = jnp.transpose(k_out[:, kc]).astype(BF16)
        new_states.append(decay * s_h + _dot(k_t, v_h))
        outs.append(o_h)
    return jnp.concatenate(outs, axis=1), new_states


def _gla_decode_rows(q, k, vb, log_a, s0_ref, s1_ref, nb, seq):
    rows = nb * seq
    t = lax.broadcasted_iota(jnp.int32, (rows, rows), 0)
    s = lax.broadcasted_iota(jnp.int32, (rows, rows), 1)
    sh = seq.bit_length() - 1
    causal = ((t >> sh) == (s >> sh)) & (s <= t)
    tril_ones = jnp.where(causal, 1.0, 0.0).astype(BF16)
    hi, lo = _split_bf16(log_a)
    b = _dot(tril_ones, hi) + _dot(tril_ones, lo)

    q_t = q * jnp.exp(b)
    k_t = (k * jnp.exp(-b)).astype(BF16)
    b_last = _bcast_rows(b, [n * seq + seq - 1 for n in range(nb)], seq)
    k_out = k * jnp.exp(b_last - b)
    col = lax.broadcasted_iota(jnp.int32, (DK, rows), 1) >> sh

    outs = []
    for h in range(HB):
        kc = slice(h * DK, (h + 1) * DK)
        vc = slice(h * DV, (h + 1) * DV)
        v_h = vb[:, vc]
        att = jnp.where(causal, _dot_nt(q_t[:, kc].astype(BF16), k_t[:, kc]), 0.0)
        o_intra = _dot(att.astype(BF16), v_h)
        decay_tr = jnp.exp(jnp.transpose(b[:, kc]))
        k_tr = jnp.transpose(k_out[:, kc])
        o_inter = []
        for n in range(nb):
            s_n = s0_ref[n, h]
            r0 = n * seq
            o_inter.append(_dot(q_t[r0:r0 + seq, kc], s_n))
            decay = jnp.broadcast_to(decay_tr[:, r0 + seq - 1:r0 + seq], (DK, DV))
            k_n = jnp.where(col == n, k_tr, 0.0).astype(BF16)
            s1_ref[n, h] = decay * s_n + _dot(k_n, v_h)
        outs.append(o_intra + jnp.concatenate(o_inter, axis=0))
    return jnp.concatenate(outs, axis=1)


def _spatial_masked_weights(w_s_ref, b_col_ref, seq):
    t = lax.broadcasted_iota(jnp.int32, (CHUNK, CHUNK), 0)
    s = lax.broadcasted_iota(jnp.int32, (CHUNK, CHUNK), 1)
    causal = s <= t
    if seq >= CHUNK:
        return [jnp.where(causal, w_s_ref[g], 0.0).astype(BF16) for g in range(HA)], b_col_ref[...]
    sh = seq.bit_length() - 1
    causal = causal & ((t >> sh) == (s >> sh))
    pick_rows = jnp.where((s < seq) & ((t & (seq - 1)) == s), 1.0, 0.0)
    pick_cols = jnp.where((t < seq) & ((s & (seq - 1)) == t), 1.0, 0.0)
    w_sp = [jnp.where(causal, _dot(_dot(pick_rows, w_s_ref[g]), pick_cols), 0.0).astype(BF16)
            for g in range(HA)]
    pos = lax.broadcasted_iota(jnp.int32, (CHUNK, HA), 0) & (seq - 1)
    bias = jnp.zeros((CHUNK, HA), F32)
    for i in range(seq):
        bias = jnp.where(pos == i, b_col_ref[i:i + 1, :], bias)
    return w_sp, bias


def _branch_a_chunk(proj_ref, r, g_v_ref, w_sp, b_col):
    vn = _rms(proj_ref[r, OFF_V:OFF_V + E_A].astype(F32), g_v_ref[...])
    vn_b = vn.astype(BF16)
    gate = jnp.concatenate(
        [_dot(w_sp[g], vn_b[:, g * DA:(g + 1) * DA]) + b_col[:, g:g + 1]
         for g in range(HA)], axis=1)
    z = proj_ref[r, OFF_Z:OFF_Z + E_A].astype(F32)
    return proj_ref[r, OFF_U:OFF_U + E_A].astype(F32) * gate * (z * _sigmoid(z)), vn


def _branch_b_out(o, zb, g_o_ref):
    g_o = g_o_ref[...]
    on = jnp.concatenate(
        [_rms(o[:, h * DV:(h + 1) * DV], g_o[:, h * DV:(h + 1) * DV]) for h in range(HB)], axis=1)
    zb = zb.astype(F32)
    return on * (zb * _sigmoid(zb))


MXU_COLS = 256


def _col_blocks(rows):
    width = MXU_COLS if rows >= 4 * CHUNK else 2 * MXU_COLS
    return [slice(c, c + width) for c in range(0, D_MODEL, width)]


def _merge(y_a, y_b, proj_ref, w_pa_ref, w_pb_ref, m_ref):
    for c in _col_blocks(y_a.shape[0]):
        m_a = proj_ref[:, OFF_MA + c.start:OFF_MA + c.stop].astype(F32)
        m_b = proj_ref[:, OFF_MB + c.start:OFF_MB + c.stop].astype(F32)
        m_ref[:, c] = (_sigmoid(m_a) * _dot(y_a, w_pa_ref[:, c])
                       + _sigmoid(m_b) * _dot(y_b, w_pb_ref[:, c])).astype(m_ref.dtype)


def _chunks(tm):
    return [slice(c * CHUNK, (c + 1) * CHUNK) for c in range(tm // CHUNK)]


def _mix_prompt_kernel(proj_ref, proj_a_ref, g_v_ref, w_s_ref, b_col_ref, w_a2_ref, b_a_ref,
                       g_o_ref, w_pa_ref, w_pb_ref, m_ref, s_out_ref, s_ref, *, tm, seq):
    keep = pl.program_id(1) != 0
    states = [jnp.where(keep, s_ref[h], 0.0) for h in range(HB)]
    w_sp, b_col = _spatial_masked_weights(w_s_ref, b_col_ref, seq)
    t = lax.broadcasted_iota(jnp.int32, (CHUNK, CHUNK), 0)
    s = lax.broadcasted_iota(jnp.int32, (CHUNK, CHUNK), 1)
    tril_ones = jnp.where(s <= t, 1.0, 0.0).astype(BF16)
    masks = _gla_masks(CHUNK)
    log_a = _log_decay(proj_a_ref[...], w_a2_ref[...], b_a_ref[...])
    chunks = _chunks(tm)
    cum = [_cum_log_decay(log_a[r], tril_ones) for r in chunks]

    def prepare(i):
        q = proj_ref[chunks[i], OFF_Q:OFF_Q + K_B].astype(F32) * (DK ** -0.5)
        k = proj_ref[chunks[i], OFF_K:OFF_K + K_B].astype(F32)
        return _gla_prepare(q, k, cum[i])

    ya_rows, yb_rows = [], []
    prepared = prepare(0)
    for i, r in enumerate(chunks):
        o_c, states = _gla_attend(prepared, proj_ref[r, OFF_VB:OFF_VB + E_B], states, masks)
        ya_rows.append(_branch_a_chunk(proj_ref, r, g_v_ref, w_sp, b_col)[0].astype(BF16))
        yb_rows.append(_branch_b_out(o_c, proj_ref[r, OFF_ZB:OFF_ZB + E_B], g_o_ref).astype(BF16))
        if i + 1 < len(chunks):
            prepared = prepare(i + 1)
    for h in range(HB):
        s_ref[h] = states[h]
        s_out_ref[0, h] = states[h]
    _merge(jnp.concatenate(ya_rows, axis=0), jnp.concatenate(yb_rows, axis=0),
           proj_ref, w_pa_ref, w_pb_ref, m_ref)


def _mix_decode_kernel(proj_ref, proj_a_ref, s0_ref, g_v_ref, w_s_ref, b_col_ref, w_a2_ref, b_a_ref,
                       g_o_ref, w_pa_ref, w_pb_ref, m_ref, vn_ref, s1_ref, *, nb, seq):
    rows = slice(0, nb * seq)
    log_a = _log_decay(proj_a_ref[...], w_a2_ref[...], b_a_ref[...])
    q = proj_ref[:, OFF_Q:OFF_Q + K_B].astype(F32) * (DK ** -0.5)
    k = proj_ref[:, OFF_K:OFF_K + K_B].astype(F32)
    o = _gla_decode_rows(q, k, proj_ref[:, OFF_VB:OFF_VB + E_B], log_a, s0_ref, s1_ref, nb, seq)
    y_a, vn = _branch_a_chunk(proj_ref, rows, g_v_ref,
                              *_spatial_masked_weights(w_s_ref, b_col_ref, seq))
    vn_ref[...] = vn
    y_b = _branch_b_out(o, proj_ref[:, OFF_ZB:OFF_ZB + E_B], g_o_ref)
    _merge(y_a.astype(BF16), y_b.astype(BF16), proj_ref, w_pa_ref, w_pb_ref, m_ref)


def _out_kernel(m_p_ref, m_s_ref, x_p_ref, x_s_ref, p_p_ref, p_s_ref,
                w_o_ref, g_post_ref, w_pg_ref, w_pe_ref, y_p_ref, y_s_ref, *, n_prompt_tiles):
    is_s = pl.program_id(0) >= n_prompt_tiles
    m = jnp.where(is_s, m_s_ref[...], m_p_ref[...])
    x = jnp.where(is_s, x_s_ref[...], x_p_ref[...])
    p_b = jnp.where(is_s, p_s_ref[...], p_p_ref[...]).astype(BF16)
    blocks = _col_blocks(m.shape[0])
    mo = jnp.concatenate([_dot(m, w_o_ref[:, c]) for c in blocks], axis=1)
    h_res = x + _rms(mo, g_post_ref[...])
    h_b = h_res.astype(BF16)
    for k in range(0, len(blocks), 2):
        cols = slice(blocks[k].start, blocks[k + 1].stop)
        y = jnp.concatenate(
            [h_res[:, c] + _sigmoid(_dot(h_b, w_pg_ref[:, c])) * _dot(p_b, w_pe_ref[:, c])
             for c in blocks[k:k + 2]], axis=1)
        to_s = jnp.broadcast_to(is_s, y.shape)
        pltpu.store(y_p_ref.at[:, cols], y, mask=jnp.logical_not(to_s))
        pltpu.store(y_s_ref.at[:, cols], y, mask=to_s)


def _const_spec(shape):
    nd = len(shape)
    return pl.BlockSpec(shape, lambda *_: (0,) * nd, pipeline_mode=pl.Buffered(1))


def _mix_prompt(proj, proj_a, wts, n_batch, seq, tm):
    nt = seq // tm
    kern = functools.partial(_mix_prompt_kernel, tm=tm, seq=seq)
    row = lambda b, i: (b * nt + i, 0)
    names = ("g_v", "w_s", "b_col", "w_a2", "b_a", "g_o", "w_pa", "w_pb")
    w_args = [wts[n] for n in names]
    return pl.pallas_call(
        kern,
        grid=(n_batch, nt),
        in_specs=[
            pl.BlockSpec((tm, N_MAIN), row),
            pl.BlockSpec((tm, A_PAD), row),
        ] + [_const_spec(w.shape) for w in w_args],
        out_specs=[
            pl.BlockSpec((tm, D_MODEL), row),
            pl.BlockSpec((1, HB, DK, DV), lambda b, i: (b, 0, 0, 0)),
        ],
        out_shape=[
            jax.ShapeDtypeStruct((n_batch * seq, D_MODEL), BF16),
            jax.ShapeDtypeStruct((n_batch, HB, DK, DV), F32),
        ],
        scratch_shapes=[pltpu.VMEM((HB, DK, DV), F32)],
        compiler_params=pltpu.CompilerParams(
            dimension_semantics=("arbitrary", "arbitrary"),
            vmem_limit_bytes=VMEM_LIMIT),
        name="mix_prompt",
    )(proj, proj_a, *w_args)


def _mix_decode(proj, proj_a, s0, wts, nb, seq):
    n_batch = s0.shape[0]
    rows = nb * seq
    assert rows == CHUNK
    kern = functools.partial(_mix_decode_kernel, nb=nb, seq=seq)
    row = lambda i: (i, 0)
    state = lambda i: (i, 0, 0, 0)
    names = ("g_v", "w_s", "b_col", "w_a2", "b_a", "g_o", "w_pa", "w_pb")
    w_args = [wts[n] for n in names]
    return pl.pallas_call(
        kern,
        grid=(n_batch // nb,),
        in_specs=[
            pl.BlockSpec((rows, N_MAIN), row),
            pl.BlockSpec((rows, A_PAD), row),
            pl.BlockSpec((nb, HB, DK, DV), state),
        ] + [_const_spec(w.shape) for w in w_args],
        out_specs=[
            pl.BlockSpec((rows, D_MODEL), row),
            pl.BlockSpec((rows, E_A), row),
            pl.BlockSpec((nb, HB, DK, DV), state),
        ],
        out_shape=[
            jax.ShapeDtypeStruct((n_batch * seq, D_MODEL), BF16),
            jax.ShapeDtypeStruct((n_batch * seq, E_A), F32),
            jax.ShapeDtypeStruct(s0.shape, F32),
        ],
        compiler_params=pltpu.CompilerParams(
            dimension_semantics=("arbitrary",),
            vmem_limit_bytes=VMEM_LIMIT),
        name="mix_decode",
    )(proj, proj_a, s0, *w_args)


def _out_proj(m_p, m_s, x_p, x_s, p_p, p_s, wts, tm):
    n_p, n_s = x_p.shape[0] // tm, x_s.shape[0] // tm
    kern = functools.partial(_out_kernel, n_prompt_tiles=n_p)
    prompt = lambda i: (jnp.minimum(i, n_p - 1), 0)
    sample = lambda i: (jnp.maximum(i - n_p, 0), 0)
    prompt_spec = lambda width: pl.BlockSpec((tm, width), prompt)
    sample_spec = lambda width: pl.BlockSpec((tm, width), sample, pipeline_mode=pl.Buffered(1))
    names = ("w_o", "g_post", "w_pg", "w_pe")
    w_args = [wts[n] for n in names]
    return pl.pallas_call(
        kern,
        grid=(n_p + n_s,),
        in_specs=[
            prompt_spec(D_MODEL), sample_spec(D_MODEL),
            prompt_spec(D_MODEL), sample_spec(D_MODEL),
            prompt_spec(PLE_DIM), sample_spec(PLE_DIM),
        ] + [_const_spec(w.shape) for w in w_args],
        out_specs=[prompt_spec(D_MODEL), sample_spec(D_MODEL)],
        out_shape=[
            jax.ShapeDtypeStruct(x_p.shape, F32),
            jax.ShapeDtypeStruct(x_s.shape, F32),
        ],
        compiler_params=pltpu.CompilerParams(
            dimension_semantics=("arbitrary",),
            vmem_limit_bytes=VMEM_LIMIT),
        name="out_proj",
    )(m_p, m_s, x_p, x_s, p_p, p_s, *w_args)


def _layer_weights(i, g_pre, w_in, g_v, w_s, b_s, w_a2, b_a, g_o, g_post):
    w_a2_p = jnp.concatenate(
        [w_a2[i], jnp.zeros((A_PAD - GATE_RANK, K_B), w_a2.dtype)], axis=0).astype(BF16)
    return dict(
        g_pre=g_pre[i][None, :],
        w_in_t=w_in[i].T,
        g_v=g_v[i][None, :],
        w_s=w_s[i],
        b_col=b_s[i].T,
        w_a2=w_a2_p,
        b_a=b_a[i][None, :],
        g_o=g_o[i].reshape(1, E_B),
        g_post=g_post[i][None, :],
    )


SIDE_WEIGHTS = ("w_pa", "w_pb", "w_o", "w_pg", "w_pe")

INPROJ_TM = 1024
MIX_TM = 512
OUT_TM = 512


def kernel(x_prompt, x_sample, p_prompt, p_sample, state_gla, g_pre, w_in, g_v, w_s, b_s,
           w_a2, b_a, g_o, w_pa, w_pb, w_o, g_post, w_pg, w_pe):
    depth = w_in.shape[0]
    n_batch, seq, _ = x_prompt.shape
    dec_batch, dec_seq, _ = x_sample.shape
    hp = x_prompt.reshape(n_batch * seq, D_MODEL)
    hs = x_sample.reshape(dec_batch * dec_seq, D_MODEL)
    sp_list, ss_list, cv_list = [], [], []
    for i in range(depth):
        wts = _layer_weights(i, g_pre, w_in, g_v, w_s, b_s, w_a2, b_a, g_o, g_post)
        side_f32 = dict(w_pa=w_pa[i], w_pb=w_pb[i], w_o=w_o[i], w_pg=w_pg[i], w_pe=w_pe[i])
        proj_s, proj_a_s, *w_bf = _inproj(hs, wts["g_pre"], wts["w_in_t"], tm=INPROJ_TM)
        proj_p, proj_a_p, *side = _inproj(hp, wts["g_pre"], wts["w_in_t"], tm=INPROJ_TM, w_bf=w_bf,
                                          cast=[side_f32[n] for n in SIDE_WEIGHTS])
        wts.update(zip(SIDE_WEIGHTS, side))
        m_s, v_s, s_s = _mix_decode(proj_s, proj_a_s, state_gla[i], wts,
                                    nb=CHUNK // dec_seq, seq=dec_seq)
        m_p, s_p = _mix_prompt(proj_p, proj_a_p, wts, n_batch, seq, tm=MIX_TM)
        hp, hs = _out_proj(m_p, m_s, hp, hs, p_prompt[i].reshape(n_batch * seq, PLE_DIM),
                           p_sample[i].reshape(dec_batch * dec_seq, PLE_DIM), wts, tm=OUT_TM)
        sp_list.append(s_p)
        ss_list.append(s_s)
        cv_list.append(v_s.reshape(dec_batch, dec_seq, E_A))
    stack = (lambda xs: xs[0][None]) if depth == 1 else jnp.stack
    return (hp.reshape(n_batch, seq, D_MODEL),
            hs.reshape(dec_batch, dec_seq, D_MODEL),
            stack(sp_list), stack(ss_list), stack(cv_list))
```

```python
import functools

import jax
import jax.numpy as jnp
from jax import lax
from jax.experimental import pallas as pl
from jax.experimental.pallas import tpu as pltpu

D_MODEL = 2048
E_A = D_MODEL // 2
HA = 4
DA = E_A // HA
CHUNK = 128
HB = 4
E_B = D_MODEL // 2
DV = E_B // HB
DK = DV // 2
K_B = HB * DK
GATE_RANK = 16
GATE_TAU = 16.0
GLA_SUB = 16
PLE_DIM = 256
EPS = 1e-6

LANES = 128
OFF_U = 0
OFF_V = OFF_U + E_A
OFF_Z = OFF_V + E_A
OFF_Q = OFF_Z + E_A
OFF_K = OFF_Q + K_B
OFF_VB = OFF_K + K_B
OFF_ZB = OFF_VB + E_B
OFF_MA = OFF_ZB + E_B
OFF_MB = OFF_MA + D_MODEL
N_MAIN = OFF_MB + D_MODEL
W_IN_A = OFF_MA
A_PAD = LANES
INPROJ_TN = 1024
INPROJ_TN_BF16 = 2048

F32 = jnp.float32
BF16 = jnp.bfloat16

VMEM_LIMIT = 56 * 1024 * 1024


def _dot(a, b):
    return jnp.dot(a, b, preferred_element_type=F32)


def _dot_nt(a, b):
    return lax.dot_general(a, b, (((1,), (1,)), ((), ())), preferred_element_type=F32)


def _sigmoid(x):
    return jax.nn.sigmoid(x)


def _rms(x, g):
    return x * lax.rsqrt(jnp.mean(x * x, axis=-1, keepdims=True) + EPS) * g


def _split_bf16(x):
    hi = x.astype(BF16)
    lo = (x - hi.astype(F32)).astype(BF16)
    return hi, lo


def _log_decay(a_tile, w_a2, b_a):
    lane = lax.broadcasted_iota(jnp.int32, a_tile.shape, 1)
    a_lr = jnp.where(lane < GATE_RANK, a_tile, jnp.zeros_like(a_tile))
    pre = _dot(a_lr, w_a2) + b_a
    return (jnp.minimum(pre, 0.0) - jnp.log(1.0 + jnp.exp(-jnp.abs(pre)))) * (1.0 / GATE_TAU)


def _bcast_rows(b, idxs, seg):
    return jnp.concatenate(
        [jnp.broadcast_to(b[i:i + 1, :], (seg, b.shape[1])) for i in idxs], axis=0)


BF16_SUBLANES = 16


def _inproj_kernel(*refs, n_cast, emit_w):
    x_ref, g_ref, w_ref, w_a_ref = refs[:4]
    cast_in = refs[4:4 + n_cast]
    o_ref, o_a_ref = refs[4 + n_cast:6 + n_cast]
    cast_out = refs[6 + n_cast:6 + 2 * n_cast]
    rest = refs[6 + 2 * n_cast:]
    xn_ref = rest[-1]

    @pl.when(pl.program_id(1) == 0)
    def _():
        xn = _rms(x_ref[...], g_ref[...]).astype(BF16)
        xn_ref[...] = xn
        w_a = w_a_ref[...]
        if emit_w:
            w_a = w_a.astype(BF16)
            rest[1][...] = w_a
        o_a_ref[...] = _dot_nt(xn, w_a).astype(o_a_ref.dtype)

    for src, dst in zip(cast_in, cast_out):
        dst[...] = src[...].astype(dst.dtype)
    w = w_ref[...]
    if emit_w:
        w = w.astype(BF16)
        rest[0][...] = w
    o_ref[...] = _dot_nt(xn_ref[...], w).astype(o_ref.dtype)


def _inproj(x2d, g_pre, w_in_t, tm, w_bf=None, cast=()):
    t = x2d.shape[0]
    emit_w = w_bf is None
    tn = INPROJ_TN if emit_w else INPROJ_TN_BF16
    n_i, n_j = t // tm, N_MAIN // tn

    def w_rows(i, j):
        row0 = j * tn + jnp.where(j * tn >= OFF_MA, GATE_RANK, 0)
        return (pl.multiple_of(row0, GATE_RANK), 0)

    if emit_w:
        w_args = (w_in_t, w_in_t)
        w_specs = [pl.BlockSpec((pl.Element(tn), pl.Element(D_MODEL)), w_rows),
                   pl.BlockSpec((A_PAD, D_MODEL), lambda i, j: (W_IN_A // A_PAD, 0))]
        w_out_specs = [pl.BlockSpec((tn, D_MODEL), lambda i, j: (j, 0)),
                       pl.BlockSpec((A_PAD, D_MODEL), lambda i, j: (0, 0))]
        w_out_shapes = [jax.ShapeDtypeStruct((N_MAIN, D_MODEL), BF16),
                        jax.ShapeDtypeStruct((A_PAD, D_MODEL), BF16)]
        assert n_i == 1, "the bf16 weight copy is written once, by a single token tile"
    else:
        w_args = w_bf
        w_specs = [pl.BlockSpec((tn, D_MODEL), lambda i, j: (j, 0)),
                   pl.BlockSpec((A_PAD, D_MODEL), lambda i, j: (0, 0))]
        w_out_specs, w_out_shapes = [], []

    cast_specs = []
    for w in cast:
        per_i = w.shape[0] // n_i
        assert per_i * n_i == w.shape[0] and per_i % BF16_SUBLANES == 0
        n_sub = max(n for n in range(1, n_j + 1)
                    if per_i % n == 0 and (per_i // n) % BF16_SUBLANES == 0)
        cast_specs.append(pl.BlockSpec(
            (per_i // n_sub, w.shape[1]),
            lambda i, j, n_sub=n_sub: (i * n_sub + jnp.minimum(j, n_sub - 1), 0)))

    kern = functools.partial(_inproj_kernel, n_cast=len(cast), emit_w=emit_w)
    return pl.pallas_call(
        kern,
        grid=(n_i, n_j),
        in_specs=[
            pl.BlockSpec((tm, D_MODEL), lambda i, j: (i, 0)),
            pl.BlockSpec((1, D_MODEL), lambda i, j: (0, 0)),
        ] + w_specs + cast_specs,
        out_specs=[
            pl.BlockSpec((tm, tn), lambda i, j: (i, j)),
            pl.BlockSpec((tm, A_PAD), lambda i, j: (i, 0)),
        ] + cast_specs + w_out_specs,
        out_shape=[
            jax.ShapeDtypeStruct((t, N_MAIN), BF16),
            jax.ShapeDtypeStruct((t, A_PAD), BF16),
        ] + [jax.ShapeDtypeStruct(w.shape, BF16) for w in cast] + w_out_shapes,
        scratch_shapes=[pltpu.VMEM((tm, D_MODEL), BF16)],
        compiler_params=pltpu.CompilerParams(
            dimension_semantics=("arbitrary", "arbitrary"),
            vmem_limit_bytes=VMEM_LIMIT),
        name="inproj_f32w" if emit_w else "inproj",
    )(x2d, g_pre, *w_args, *cast)


def _gla_levels(c):
    levels = []
    blk = c
    while blk > GLA_SUB:
        per = blk // GLA_SUB
        levels.append((blk, [(j // per) * per + per // 2 - 1 for j in range(c // GLA_SUB)]))
        blk //= 2
    return levels


def _gla_masks(c):
    t = lax.broadcasted_iota(jnp.int32, (c, c), 0)
    s = lax.broadcasted_iota(jnp.int32, (c, c), 1)
    levels = _gla_levels(c)
    masks = []
    for blk, _ in levels:
        sh = blk.bit_length() - 1
        half = blk // 2
        same = (t >> sh) == (s >> sh)
        masks.append(same & ((t & (blk - 1)) >= half) & ((s & (blk - 1)) < half))
    sh = GLA_SUB.bit_length() - 1
    masks.append(((t >> sh) == (s >> sh)) & (s <= t))
    return masks


def _cum_log_decay(log_a, tril_ones):
    hi, lo = _split_bf16(log_a)
    return _dot(tril_ones, hi) + _dot(tril_ones, lo)


def _gla_prepare(q, k, b):
    c = q.shape[0]
    n = c // GLA_SUB

    def rep(x):
        return _bcast_rows(x, list(range(n)), GLA_SUB)

    ends = jnp.concatenate([b[(j + 1) * GLA_SUB - 1:(j + 1) * GLA_SUB, :] for j in range(n)], axis=0)
    starts = jnp.concatenate([jnp.zeros((1, b.shape[1]), F32), ends[:n - 1]], axis=0)
    starts_r = rep(starts)
    q_e = q * jnp.exp(b - starts_r)
    k_e = k * jnp.exp(rep(ends) - b)

    qs, ks = [], []
    for _, ref_blocks in _gla_levels(c):
        ref = jnp.concatenate([ends[r:r + 1] for r in ref_blocks], axis=0)
        qs.append((q_e * rep(jnp.exp(jnp.minimum(starts - ref, 0.0)))).astype(BF16))
        ks.append((k_e * rep(jnp.exp(jnp.minimum(ref - ends, 0.0)))).astype(BF16))
    qs.append(q_e.astype(BF16))
    ks.append((k * jnp.exp(starts_r - b)).astype(BF16))

    q_in = (q_e * rep(jnp.exp(starts))).astype(BF16)
    k_out = k_e * rep(jnp.exp(ends[n - 1:n] - ends))
    return qs, ks, q_in, k_out, b


def _gla_attend(prepared, vb, s_heads, masks):
    qs, ks, q_in, k_out, b = prepared
    c = b.shape[0]
    heads = [(slice(h * DK, (h + 1) * DK), slice(h * DV, (h + 1) * DV)) for h in range(HB)]
    atts = []
    for kc, _ in heads:
        att = jnp.zeros((c, c), F32)
        for ql, kl, m in zip(qs, ks, masks):
            att = jnp.where(m, _dot_nt(ql[:, kc], kl[:, kc]), att)
        atts.append(att.astype(BF16))
    outs, new_states = [], []
    for (kc, vc), att, s_h in zip(heads, atts, s_heads):
        v_h = vb[:, vc]
        outs.append(_dot(att, v_h) + _dot(q_in[:, kc], s_h.astype(BF16)))
        b_t = jnp.transpose(b[:, kc])
        decay = jnp.broadcast_to(jnp.exp(b_t[:, c - 1:c]), (DK, DV))
        k_t = jnp.transpose(k_out[:, kc]).astype(BF16)
        new_states.append(decay * s_h + _dot(k_t, v_h))
    return jnp.concatenate(outs, axis=1), new_states


def _gla_decode_rows(q, k, vb, log_a, s0_ref, s1_ref, nb, seq):
    rows = nb * seq
    t = lax.broadcasted_iota(jnp.int32, (rows, rows), 0)
    s = lax.broadcasted_iota(jnp.int32, (rows, rows), 1)
    sh = seq.bit_length() - 1
    causal = ((t >> sh) == (s >> sh)) & (s <= t)
    tril_ones = jnp.where(causal, 1.0, 0.0).astype(BF16)
    hi, lo = _split_bf16(log_a)
    b = _dot(tril_ones, hi) + _dot(tril_ones, lo)

    q_t = q * jnp.exp(b)
    k_t = (k * jnp.exp(-b)).astype(BF16)
    b_last = _bcast_rows(b, [n * seq + seq - 1 for n in range(nb)], seq)
    k_out = k * jnp.exp(b_last - b)
    col = lax.broadcasted_iota(jnp.int32, (DK, rows), 1) >> sh

    outs = []
    for h in range(HB):
        kc = slice(h * DK, (h + 1) * DK)
        vc = slice(h * DV, (h + 1) * DV)
        v_h = vb[:, vc]
        att = jnp.where(causal, _dot_nt(q_t[:, kc].astype(BF16), k_t[:, kc]), 0.0)
        o_intra = _dot(att.astype(BF16), v_h)
        decay_tr = jnp.exp(jnp.transpose(b[:, kc]))
        k_tr = jnp.transpose(k_out[:, kc])
        o_inter = []
        for n in range(nb):
            s_n = s0_ref[n, h]
            r0 = n * seq
            o_inter.append(_dot(q_t[r0:r0 + seq, kc], s_n))
            decay = jnp.broadcast_to(decay_tr[:, r0 + seq - 1:r0 + seq], (DK, DV))
            k_n = jnp.where(col == n, k_tr, 0.0).astype(BF16)
            s1_ref[n, h] = decay * s_n + _dot(k_n, v_h)
        outs.append(o_intra + jnp.concatenate(o_inter, axis=0))
    return jnp.concatenate(outs, axis=1)


def _spatial_masked_weights(w_s_ref, b_col_ref, seq):
    t = lax.broadcasted_iota(jnp.int32, (CHUNK, CHUNK), 0)
    s = lax.broadcasted_iota(jnp.int32, (CHUNK, CHUNK), 1)
    causal = s <= t
    if seq >= CHUNK:
        return [jnp.where(causal, w_s_ref[g], 0.0).astype(BF16) for g in range(HA)], b_col_ref[...]
    sh = seq.bit_length() - 1
    causal = causal & ((t >> sh) == (s >> sh))
    pick_rows = jnp.where((s < seq) & ((t & (seq - 1)) == s), 1.0, 0.0)
    pick_cols = jnp.where((t < seq) & ((s & (seq - 1)) == t), 1.0, 0.0)
    w_sp = [jnp.where(causal, _dot(_dot(pick_rows, w_s_ref[g]), pick_cols), 0.0).astype(BF16)
            for g in range(HA)]
    pos = lax.broadcasted_iota(jnp.int32, (CHUNK, HA), 0) & (seq - 1)
    bias = jnp.zeros((CHUNK, HA), F32)
    for i in range(seq):
        bias = jnp.where(pos == i, b_col_ref[i:i + 1, :], bias)
    return w_sp, bias


def _branch_a_chunk(proj_ref, r, g_v_ref, w_sp, b_col):
    vn = _rms(proj_ref[r, OFF_V:OFF_V + E_A].astype(F32), g_v_ref[...])
    vn_b = vn.astype(BF16)
    gate = jnp.concatenate(
        [_dot(w_sp[g], vn_b[:, g * DA:(g + 1) * DA]) + b_col[:, g:g + 1]
         for g in range(HA)], axis=1)
    z = proj_ref[r, OFF_Z:OFF_Z + E_A].astype(F32)
    return proj_ref[r, OFF_U:OFF_U + E_A].astype(F32) * gate * (z * _sigmoid(z)), vn


def _branch_b_out(o, zb, g_o_ref):
    g_o = g_o_ref[...]
    on = jnp.concatenate(
        [_rms(o[:, h * DV:(h + 1) * DV], g_o[:, h * DV:(h + 1) * DV]) for h in range(HB)], axis=1)
    zb = zb.astype(F32)
    return on * (zb * _sigmoid(zb))


MXU_COLS = 256


def _col_blocks(rows):
    width = MXU_COLS if rows >= 4 * CHUNK else 2 * MXU_COLS
    return [slice(c, c + width) for c in range(0, D_MODEL, width)]


def _merge(y_a, y_b, proj_ref, w_pa_ref, w_pb_ref, m_ref):
    for c in _col_blocks(y_a.shape[0]):
        m_a = proj_ref[:, OFF_MA + c.start:OFF_MA + c.stop].astype(F32)
        m_b = proj_ref[:, OFF_MB + c.start:OFF_MB + c.stop].astype(F32)
        m_ref[:, c] = (_sigmoid(m_a) * _dot(y_a, w_pa_ref[:, c])
                       + _sigmoid(m_b) * _dot(y_b, w_pb_ref[:, c])).astype(m_ref.dtype)


def _chunks(tm):
    return [slice(c * CHUNK, (c + 1) * CHUNK) for c in range(tm // CHUNK)]


def _mix_prompt_kernel(proj_ref, proj_a_ref, g_v_ref, w_s_ref, b_col_ref, w_a2_ref, b_a_ref,
                       g_o_ref, w_pa_ref, w_pb_ref, m_ref, s_out_ref, s_ref, *, tm, seq):
    keep = pl.program_id(1) != 0
    states = [jnp.where(keep, s_ref[h], 0.0) for h in range(HB)]
    w_sp, b_col = _spatial_masked_weights(w_s_ref, b_col_ref, seq)
    t = lax.broadcasted_iota(jnp.int32, (CHUNK, CHUNK), 0)
    s = lax.broadcasted_iota(jnp.int32, (CHUNK, CHUNK), 1)
    tril_ones = jnp.where(s <= t, 1.0, 0.0).astype(BF16)
    masks = _gla_masks(CHUNK)
    log_a = _log_decay(proj_a_ref[...], w_a2_ref[...], b_a_ref[...])
    chunks = _chunks(tm)
    cum = [_cum_log_decay(log_a[r], tril_ones) for r in chunks]

    def prepare(i):
        q = proj_ref[chunks[i], OFF_Q:OFF_Q + K_B].astype(F32) * (DK ** -0.5)
        k = proj_ref[chunks[i], OFF_K:OFF_K + K_B].astype(F32)
        return _gla_prepare(q, k, cum[i])

    ya_rows, yb_rows = [], []
    prepared = prepare(0)
    for i, r in enumerate(chunks):
        o_c, states = _gla_attend(prepared, proj_ref[r, OFF_VB:OFF_VB + E_B], states, masks)
        ya_rows.append(_branch_a_chunk(proj_ref, r, g_v_ref, w_sp, b_col)[0].astype(BF16))
        yb_rows.append(_branch_b_out(o_c, proj_ref[r, OFF_ZB:OFF_ZB + E_B], g_o_ref).astype(BF16))
        if i + 1 < len(chunks):
            prepared = prepare(i + 1)
    for h in range(HB):
        s_ref[h] = states[h]
        s_out_ref[0, h] = states[h]
    _merge(jnp.concatenate(ya_rows, axis=0), jnp.concatenate(yb_rows, axis=0),
           proj_ref, w_pa_ref, w_pb_ref, m_ref)


def _mix_decode_kernel(proj_ref, proj_a_ref, s0_ref, g_v_ref, w_s_ref, b_col_ref, w_a2_ref, b_a_ref,
                       g_o_ref, w_pa_ref, w_pb_ref, m_ref, vn_ref, s1_ref, *, nb, seq):
    rows = slice(0, nb * seq)
    log_a = _log_decay(proj_a_ref[...], w_a2_ref[...], b_a_ref[...])
    q = proj_ref[:, OFF_Q:OFF_Q + K_B].astype(F32) * (DK ** -0.5)
    k = proj_ref[:, OFF_K:OFF_K + K_B].astype(F32)
    o = _gla_decode_rows(q, k, proj_ref[:, OFF_VB:OFF_VB + E_B], log_a, s0_ref, s1_ref, nb, seq)
    y_a, vn = _branch_a_chunk(proj_ref, rows, g_v_ref,
                              *_spatial_masked_weights(w_s_ref, b_col_ref, seq))
    vn_ref[...] = vn
    y_b = _branch_b_out(o, proj_ref[:, OFF_ZB:OFF_ZB + E_B], g_o_ref)
    _merge(y_a.astype(BF16), y_b.astype(BF16), proj_ref, w_pa_ref, w_pb_ref, m_ref)


def _out_kernel(m_p_ref, m_s_ref, x_p_ref, x_s_ref, p_p_ref, p_s_ref,
                w_o_ref, g_post_ref, w_pg_ref, w_pe_ref, y_p_ref, y_s_ref, *, n_prompt_tiles):
    is_s = pl.program_id(0) >= n_prompt_tiles
    m = jnp.where(is_s, m_s_ref[...], m_p_ref[...])
    x = jnp.where(is_s, x_s_ref[...], x_p_ref[...])
    p_b = jnp.where(is_s, p_s_ref[...], p_p_ref[...]).astype(BF16)
    blocks = _col_blocks(m.shape[0])
    mo = jnp.concatenate([_dot(m, w_o_ref[:, c]) for c in blocks], axis=1)
    h_res = x + _rms(mo, g_post_ref[...])
    h_b = h_res.astype(BF16)
    for k in range(0, len(blocks), 2):
        cols = slice(blocks[k].start, blocks[k + 1].stop)
        y = jnp.concatenate(
            [h_res[:, c] + _sigmoid(_dot(h_b, w_pg_ref[:, c])) * _dot(p_b, w_pe_ref[:, c])
             for c in blocks[k:k + 2]], axis=1)
        to_s = jnp.broadcast_to(is_s, y.shape)
        pltpu.store(y_p_ref.at[:, cols], y, mask=jnp.logical_not(to_s))
        pltpu.store(y_s_ref.at[:, cols], y, mask=to_s)


def _const_spec(shape):
    nd = len(shape)
    return pl.BlockSpec(shape, lambda *_: (0,) * nd, pipeline_mode=pl.Buffered(1))


def _mix_prompt(proj, proj_a, wts, n_batch, seq, tm):
    nt = seq // tm
    kern = functools.partial(_mix_prompt_kernel, tm=tm, seq=seq)
    row = lambda b, i: (b * nt + i, 0)
    names = ("g_v", "w_s", "b_col", "w_a2", "b_a", "g_o", "w_pa", "w_pb")
    w_args = [wts[n] for n in names]
    return pl.pallas_call(
        kern,
        grid=(n_batch, nt),
        in_specs=[
            pl.BlockSpec((tm, N_MAIN), row),
            pl.BlockSpec((tm, A_PAD), row),
        ] + [_const_spec(w.shape) for w in w_args],
        out_specs=[
            pl.BlockSpec((tm, D_MODEL), row),
            pl.BlockSpec((1, HB, DK, DV), lambda b, i: (b, 0, 0, 0)),
        ],
        out_shape=[
            jax.ShapeDtypeStruct((n_batch * seq, D_MODEL), BF16),
            jax.ShapeDtypeStruct((n_batch, HB, DK, DV), F32),
        ],
        scratch_shapes=[pltpu.VMEM((HB, DK, DV), F32)],
        compiler_params=pltpu.CompilerParams(
            dimension_semantics=("arbitrary", "arbitrary"),
            vmem_limit_bytes=VMEM_LIMIT),
        name="mix_prompt",
    )(proj, proj_a, *w_args)


def _mix_decode(proj, proj_a, s0, wts, nb, seq):
    n_batch = s0.shape[0]
    rows = nb * seq
    assert rows == CHUNK
    kern = functools.partial(_mix_decode_kernel, nb=nb, seq=seq)
    row = lambda i: (i, 0)
    state = lambda i: (i, 0, 0, 0)
    names = ("g_v", "w_s", "b_col", "w_a2", "b_a", "g_o", "w_pa", "w_pb")
    w_args = [wts[n] for n in names]
    return pl.pallas_call(
        kern,
        grid=(n_batch // nb,),
        in_specs=[
            pl.BlockSpec((rows, N_MAIN), row),
            pl.BlockSpec((rows, A_PAD), row),
            pl.BlockSpec((nb, HB, DK, DV), state),
        ] + [_const_spec(w.shape) for w in w_args],
        out_specs=[
            pl.BlockSpec((rows, D_MODEL), row),
            pl.BlockSpec((rows, E_A), row),
            pl.BlockSpec((nb, HB, DK, DV), state),
        ],
        out_shape=[
            jax.ShapeDtypeStruct((n_batch * seq, D_MODEL), BF16),
            jax.ShapeDtypeStruct((n_batch * seq, E_A), F32),
            jax.ShapeDtypeStruct(s0.shape, F32),
        ],
        compiler_params=pltpu.CompilerParams(
            dimension_semantics=("arbitrary",),
            vmem_limit_bytes=VMEM_LIMIT),
        name="mix_decode",
    )(proj, proj_a, s0, *w_args)


def _out_proj(m_p, m_s, x_p, x_s, p_p, p_s, wts, tm):
    n_p, n_s = x_p.shape[0] // tm, x_s.shape[0] // tm
    kern = functools.partial(_out_kernel, n_prompt_tiles=n_p)
    prompt = lambda i: (jnp.minimum(i, n_p - 1), 0)
    sample = lambda i: (jnp.maximum(i - n_p, 0), 0)
    prompt_spec = lambda width: pl.BlockSpec((tm, width), prompt)
    sample_spec = lambda width: pl.BlockSpec((tm, width), sample, pipeline_mode=pl.Buffered(1))
    names = ("w_o", "g_post", "w_pg", "w_pe")
    w_args = [wts[n] for n in names]
    return pl.pallas_call(
        kern,
        grid=(n_p + n_s,),
        in_specs=[
            prompt_spec(D_MODEL), sample_spec(D_MODEL),
            prompt_spec(D_MODEL), sample_spec(D_MODEL),
            prompt_spec(PLE_DIM), sample_spec(PLE_DIM),
        ] + [_const_spec(w.shape) for w in w_args],
        out_specs=[prompt_spec(D_MODEL), sample_spec(D_MODEL)],
        out_shape=[
            jax.ShapeDtypeStruct(x_p.shape, F32),
            jax.ShapeDtypeStruct(x_s.shape, F32),
        ],
        compiler_params=pltpu.CompilerParams(
            dimension_semantics=("arbitrary",),
            vmem_limit_bytes=VMEM_LIMIT),
        name="out_proj",
    )(m_p, m_s, x_p, x_s, p_p, p_s, *w_args)


def _layer_weights(i, g_pre, w_in, g_v, w_s, b_s, w_a2, b_a, g_o, g_post):
    w_a2_p = jnp.concatenate(
        [w_a2[i], jnp.zeros((A_PAD - GATE_RANK, K_B), w_a2.dtype)], axis=0).astype(BF16)
    return dict(
        g_pre=g_pre[i][None, :],
        w_in_t=w_in[i].T,
        g_v=g_v[i][None, :],
        w_s=w_s[i],
        b_col=b_s[i].T,
        w_a2=w_a2_p,
        b_a=b_a[i][None, :],
        g_o=g_o[i].reshape(1, E_B),
        g_post=g_post[i][None, :],
    )


SIDE_WEIGHTS = ("w_pa", "w_pb", "w_o", "w_pg", "w_pe")

INPROJ_TM = 1024
MIX_TM = 512
OUT_TM = 512


def kernel(x_prompt, x_sample, p_prompt, p_sample, state_gla, g_pre, w_in, g_v, w_s, b_s,
           w_a2, b_a, g_o, w_pa, w_pb, w_o, g_post, w_pg, w_pe):
    depth = w_in.shape[0]
    n_batch, seq, _ = x_prompt.shape
    dec_batch, dec_seq, _ = x_sample.shape
    hp = x_prompt.reshape(n_batch * seq, D_MODEL)
    hs = x_sample.reshape(dec_batch * dec_seq, D_MODEL)
    sp_list, ss_list, cv_list = [], [], []
    for i in range(depth):
        wts = _layer_weights(i, g_pre, w_in, g_v, w_s, b_s, w_a2, b_a, g_o, g_post)
        side_f32 = dict(w_pa=w_pa[i], w_pb=w_pb[i], w_o=w_o[i], w_pg=w_pg[i], w_pe=w_pe[i])
        proj_s, proj_a_s, *w_bf = _inproj(hs, wts["g_pre"], wts["w_in_t"], tm=INPROJ_TM)
        proj_p, proj_a_p, *side = _inproj(hp, wts["g_pre"], wts["w_in_t"], tm=INPROJ_TM, w_bf=w_bf,
                                          cast=[side_f32[n] for n in SIDE_WEIGHTS])
        wts.update(zip(SIDE_WEIGHTS, side))
        m_s, v_s, s_s = _mix_decode(proj_s, proj_a_s, state_gla[i], wts,
                                    nb=CHUNK // dec_seq, seq=dec_seq)
        m_p, s_p = _mix_prompt(proj_p, proj_a_p, wts, n_batch, seq, tm=MIX_TM)
        hp, hs = _out_proj(m_p, m_s, hp, hs, p_prompt[i].reshape(n_batch * seq, PLE_DIM),
                           p_sample[i].reshape(dec_batch * dec_seq, PLE_DIM), wts, tm=OUT_TM)
        sp_list.append(s_p)
        ss_list.append(s_s)
        cv_list.append(v_s.reshape(dec_batch, dec_seq, E_A))
    stack = (lambda xs: xs[0][None]) if depth == 1 else jnp.stack
    return (hp.reshape(n_batch, seq, D_MODEL),
            hs.reshape(dec_batch, dec_seq, D_MODEL),
            stack(sp_list), stack(ss_list), stack(cv_list))
```
